```python
import math
import jax
import jax.numpy as jnp
from jax import lax
import numpy as np


D_MODEL = 2048
BATCH = 8
SEQ = 4096
DEPTH = 4

CHUNK = 64
N_MIXERS = 2
N_S5 = len(range(0, DEPTH, N_MIXERS))
N_HG = len(range(1, DEPTH, N_MIXERS))
N_ADA = 6
EPS = 1e-6
S5_GROUP = 16
S5_GROUPS = D_MODEL // S5_GROUP
S5_STATE = 64
S5_DT_MIN = 0.001
S5_DT_MAX = 0.1
S5_RE_MAX = -1e-4
HG_EXPAND = 128
HG_HEADS = D_MODEL // HG_EXPAND
HG_DK = HG_EXPAND
HG_FDIM = HG_HEADS * HG_DK
HG_DV = D_MODEL // HG_HEADS
HG_VDIM = HG_HEADS * HG_DV
N_EXPERTS = 32
TOP_K = 4
EXPERT_FF = D_MODEL // 4
SWIGLU_LIMIT = 7.0
SWIGLU_ALPHA = 1.702
MOE_BLOCK = 256

kernel_name = 'hybrid_s5_hgrn2_moe_adaln'


def rms_norm(x, g):
    xf = x.astype(jnp.float32)
    y = xf * lax.rsqrt(jnp.mean(xf * xf, axis=-1, keepdims=True) + EPS)
    return (y * g.astype(jnp.float32)).astype(x.dtype)


def s5_mixer(h, lam_re, lam_im, log_dt, b_re, b_im, c_re, c_im, d_skip, w_glu):
    bsz, seq, dm = h.shape
    nc = seq // CHUNK
    f32 = jnp.float32
    u = h.astype(f32)
    lam = lax.complex(jnp.minimum(lam_re.astype(f32), S5_RE_MAX), lam_im.astype(f32))
    dt = jnp.exp(log_dt.astype(f32))[:, None]
    lam_bar = jnp.exp(lam * dt)
    b_bar = ((lam_bar - 1.0) / lam)[:, :, None] * lax.complex(b_re.astype(f32), b_im.astype(f32))
    c_mat = lax.complex(c_re.astype(f32), c_im.astype(f32))
    steps = jnp.arange(1, CHUNK + 1, dtype=f32)[:, None, None]
    lam_pow = jnp.exp(lam[None] * (dt[None] * steps))
    uc = u.reshape(bsz, nc, CHUNK, S5_GROUPS, S5_GROUP).transpose(1, 0, 2, 3, 4)

    def combine(e1, e2):
        a1, s1 = e1
        a2, s2 = e2
        return a1 * a2, a2 * s1 + s2

    def step(state, u_t):
        bu = jnp.einsum('bcgh,gph->bcgp', u_t.astype(jnp.complex64), b_bar)
        a = jnp.broadcast_to(lam_bar, bu.shape)
        _, s_loc = lax.associative_scan(combine, (a, bu), axis=1)
        s_all = s_loc + lam_pow[None] * state[:, None]
        y = jnp.einsum('bcgp,ghp->bcgh', s_all, c_mat).real
        return s_all[:, -1], y

    s0 = jnp.zeros((bsz, S5_GROUPS, S5_STATE), jnp.complex64)
    _, y = lax.scan(step, s0, uc)
    y = y.transpose(1, 0, 2, 3, 4).reshape(bsz, seq, dm) + d_skip.astype(f32) * u
    z = jax.nn.gelu(y).astype(h.dtype)
    ga = z @ w_glu
    return ga[..., :dm] * jax.nn.sigmoid(ga[..., dm:])


def hgrn2_mixer(h, w_in, lb, g_norm, w_out):
    bsz, seq, dm = h.shape
    nc = seq // CHUNK
    f32 = jnp.float32
    proj = h @ w_in
    q = proj[..., :HG_FDIM]
    f_logit = proj[..., HG_FDIM:2 * HG_FDIM]
    v = proj[..., 2 * HG_FDIM:2 * HG_FDIM + HG_VDIM]
    gate = proj[..., 2 * HG_FDIM + HG_VDIM:]
    lb = lb.astype(f32)
    log_f = jnp.logaddexp(jnp.log(lb), jnp.log1p(-lb) + jax.nn.log_sigmoid(f_logit.astype(f32)))
    k = -jnp.expm1(log_f)

    def to_chunks(t, d):
        return t.astype(f32).reshape(bsz, nc, CHUNK, HG_HEADS, d).transpose(1, 0, 3, 2, 4)

    qc, kc, vc, gc = to_chunks(q, HG_DK), to_chunks(k, HG_DK), to_chunks(v, HG_DV), to_chunks(log_f, HG_DK)
    causal = jnp.tril(jnp.ones((CHUNK, CHUNK), dtype=bool))

    def step(state, inp):
        q_t, k_t, v_t, lf = inp
        b = jnp.cumsum(lf, axis=2)
        rel = jnp.where(causal[:, :, None], b[:, :, :, None, :] - b[:, :, None, :, :], -jnp.inf)
        scores = jnp.einsum('bhtd,bhsd,bhtsd->bhts', q_t, k_t, jnp.exp(rel))
        o = jnp.einsum('bhts,bhsv->bhtv', scores, v_t) + jnp.einsum('bhtd,bhdv->bhtv', q_t * jnp.exp(b), state)
        b_end = b[:, :, -1:, :]
        state = jnp.exp(b_end[:, :, 0, :, None]) * state + jnp.einsum('bhsd,bhsv->bhdv', k_t * jnp.exp(b_end - b), v_t)
        return state, o

    s0 = jnp.zeros((bsz, HG_HEADS, HG_DK, HG_DV), f32)
    _, o = lax.scan(step, s0, (qc, kc, vc, gc))
    o = o.transpose(1, 0, 3, 2, 4).reshape(bsz, seq, HG_HEADS, HG_DV)
    o = o * lax.rsqrt(jnp.mean(o * o, axis=-1, keepdims=True) + EPS) * g_norm.astype(f32)
    o = o * jax.nn.silu(gate.astype(f32).reshape(bsz, seq, HG_HEADS, HG_DV))
    return o.reshape(bsz, seq, dm).astype(h.dtype) @ w_out


def moe_ffn(h, w_r, b_r, w_gu, b_gu, w_dn, b_dn):
    bsz, seq, dm = h.shape
    n_tok = bsz * seq
    xt = h.reshape(n_tok, dm)
    logits = (xt @ w_r).astype(jnp.float32) + b_r.astype(jnp.float32)
    top_val, top_idx = lax.top_k(logits, TOP_K)
    gates = jax.nn.softmax(top_val, axis=-1)
    e_flat = top_idx.reshape(-1)
    tok_flat = jnp.repeat(jnp.arange(n_tok, dtype=jnp.int32), TOP_K)
    w_flat = gates.reshape(-1)
    onehot = jax.nn.one_hot(e_flat, N_EXPERTS, dtype=jnp.int32)
    counts = jnp.sum(onehot, axis=0)
    rank = jnp.take_along_axis(jnp.cumsum(onehot, axis=0), e_flat[:, None], axis=1)[:, 0] - 1
    padded = (counts + MOE_BLOCK - 1) // MOE_BLOCK * MOE_BLOCK
    pad_end = jnp.cumsum(padded)
    dest = (pad_end - padded)[e_flat] + rank
    n_blocks = -(-(n_tok * TOP_K) // MOE_BLOCK) + N_EXPERTS
    n_rows = n_blocks * MOE_BLOCK
    row_tok = jnp.zeros((n_rows,), jnp.int32).at[dest].set(tok_flat)
    row_w = jnp.zeros((n_rows,), jnp.float32).at[dest].set(w_flat)
    block_e = jnp.minimum(jnp.searchsorted(pad_end, jnp.arange(n_blocks, dtype=jnp.int32) * MOE_BLOCK, side='right'), N_EXPERTS - 1)

    def block_step(acc, blk):
        tok, wt, e = blk
        gu = xt[tok] @ w_gu[e] + b_gu[e]
        gt = jnp.minimum(gu[:, :EXPERT_FF], SWIGLU_LIMIT)
        up = jnp.clip(gu[:, EXPERT_FF:], -SWIGLU_LIMIT, SWIGLU_LIMIT)
        act = (up + 1.0) * gt * jax.nn.sigmoid(SWIGLU_ALPHA * gt)
        y = (act @ w_dn[e] + b_dn[e]).astype(jnp.float32) * wt[:, None]
        return acc.at[tok].add(y), None

    acc, _ = lax.scan(block_step, jnp.zeros((n_tok, dm), jnp.float32),
                      (row_tok.reshape(n_blocks, MOE_BLOCK), row_w.reshape(n_blocks, MOE_BLOCK), block_e))
    return acc.reshape(bsz, seq, dm).astype(h.dtype)


def setup_inputs(seed: int = 0) -> dict:
    key = jax.random.key(seed)
    ks = jax.random.split(key, 32)
    f32 = jnp.float32
    D = D_MODEL
    G, P, H = S5_GROUPS, S5_STATE, S5_GROUP
    E, F = N_EXPERTS, EXPERT_FF

    def nrm(k, shape, scale):
        return jax.random.normal(k, shape, f32) * scale

    x = nrm(ks[0], (BATCH, SEQ, D), 1.0)
    c = nrm(ks[1], (BATCH, D), 1.0)
    ada_w = nrm(ks[2], (DEPTH, D, N_ADA * D), 0.5 * D ** -0.5)
    ada_b = nrm(ks[3], (DEPTH, N_ADA * D), 0.02)
    norm_mix = 1.0 + nrm(ks[4], (DEPTH, D), 0.02)
    norm_ffn = 1.0 + nrm(ks[5], (DEPTH, D), 0.02)
    norm_final = 1.0 + nrm(ks[6], (D,), 0.02)
    s5_lambda_re = -0.5 + nrm(ks[7], (N_S5, G, P), 0.01)
    s5_lambda_im = math.pi * jnp.arange(P, dtype=f32) + nrm(ks[8], (N_S5, G, P), 0.01)
    s5_log_dt = jax.random.uniform(ks[9], (N_S5, G), f32, math.log(S5_DT_MIN), math.log(S5_DT_MAX))
    s5_b_re = nrm(ks[10], (N_S5, G, P, H), (2 * H) ** -0.5)
    s5_b_im = nrm(ks[11], (N_S5, G, P, H), (2 * H) ** -0.5)
    s5_c_re = nrm(ks[12], (N_S5, G, H, P), P ** -0.5)
    s5_c_im = nrm(ks[13], (N_S5, G, H, P), P ** -0.5)
    s5_d = nrm(ks[14], (N_S5, D), 1.0)
    s5_w_glu = nrm(ks[15], (N_S5, D, 2 * D), D ** -0.5)
    hg_w_in = nrm(ks[16], (N_HG, D, 2 * HG_FDIM + 2 * HG_VDIM), D ** -0.5)
    hg_lb_raw = nrm(ks[17], (DEPTH, HG_FDIM), 0.1)
    hg_norm = 1.0 + nrm(ks[18], (N_HG, HG_DV), 0.02)
    hg_w_out = nrm(ks[19], (N_HG, HG_VDIM, D), HG_VDIM ** -0.5)
    router_w = nrm(ks[20], (DEPTH, D, E), D ** -0.5)
    router_b = nrm(ks[21], (DEPTH, E), 0.01)
    moe_w_gate_up = nrm(ks[22], (DEPTH, E, D, 2 * F), D ** -0.5)
    moe_b_gate_up = nrm(ks[23], (DEPTH, E, 2 * F), 0.01)
    moe_w_down = nrm(ks[24], (DEPTH, E, F, D), F ** -0.5)
    moe_b_down = nrm(ks[25], (DEPTH, E, D), 0.01)
    return {'x': x, 'c': c, 'ada_w': ada_w, 'ada_b': ada_b,
            'norm_mix': norm_mix, 'norm_ffn': norm_ffn, 'norm_final': norm_final,
            's5_lambda_re': s5_lambda_re, 's5_lambda_im': s5_lambda_im, 's5_log_dt': s5_log_dt,
            's5_b_re': s5_b_re, 's5_b_im': s5_b_im, 's5_c_re': s5_c_re, 's5_c_im': s5_c_im,
            's5_d': s5_d, 's5_w_glu': s5_w_glu,
            'hg_w_in': hg_w_in, 'hg_lb_raw': hg_lb_raw, 'hg_norm': hg_norm, 'hg_w_out': hg_w_out,
            'router_w': router_w, 'router_b': router_b,
            'moe_w_gate_up': moe_w_gate_up, 'moe_b_gate_up': moe_b_gate_up,
            'moe_w_down': moe_w_down, 'moe_b_down': moe_b_down}


def reference(x, c, ada_w, ada_b, norm_mix, norm_ffn, norm_final,
              s5_lambda_re, s5_lambda_im, s5_log_dt, s5_b_re, s5_b_im, s5_c_re, s5_c_im,
              s5_d, s5_w_glu, hg_w_in, hg_lb_raw, hg_norm, hg_w_out,
              router_w, router_b, moe_w_gate_up, moe_b_gate_up, moe_w_down, moe_b_down):
    lb_p = jax.nn.softmax(hg_lb_raw.astype(jnp.float32), axis=0)
    lb_all = jnp.cumsum(lb_p, axis=0) - lb_p[0]
    c_act = jax.nn.silu(c)
    for i in range(DEPTH):
        mod = (c_act @ ada_w[i] + ada_b[i])[:, None, :]
        sh1, sc1, g1, sh2, sc2, g2 = jnp.split(mod, N_ADA, axis=-1)
        h = rms_norm(x, norm_mix[i]) * (1.0 + sc1) + sh1
        j = i // N_MIXERS
        if i % N_MIXERS == 0:
            y = s5_mixer(h, s5_lambda_re[j], s5_lambda_im[j], s5_log_dt[j], s5_b_re[j], s5_b_im[j],
                         s5_c_re[j], s5_c_im[j], s5_d[j], s5_w_glu[j])
        else:
            y = hgrn2_mixer(h, hg_w_in[j], lb_all[i], hg_norm[j], hg_w_out[j])
        x = x + g1 * y
        h = rms_norm(x, norm_ffn[i]) * (1.0 + sc2) + sh2
        x = x + g2 * moe_ffn(h, router_w[i], router_b[i], moe_w_gate_up[i], moe_b_gate_up[i],
                             moe_w_down[i], moe_b_down[i])
    return rms_norm(x, norm_final)
```

```python
import functools

import numpy as np
import jax
import jax.numpy as jnp
from jax import lax
from jax.experimental import pallas as pl
from jax.experimental.pallas import tpu as pltpu

F32 = jnp.float32
BF16 = jnp.bfloat16

EPS = 1e-6
N_ADA = 6
TOP_K = 4
S5_RE_MAX = -1e-4
SWIGLU_LIMIT = 7.0
SWIGLU_ALPHA = 1.702
GELU_C0 = 0.7978845608028654
GELU_C1 = 0.044715

LANES = 128
SUBLANES = 8
HG_CHUNK = 64
S5_CHUNK = 128
MOE_ROWS = 256
VMEM_LIMIT = 56 * 1024 * 1024


def _params(*sem):
    return pltpu.CompilerParams(dimension_semantics=sem, vmem_limit_bytes=VMEM_LIMIT)


def _tile(n, pref, align):
    if n <= pref:
        return n
    t = (pref // align) * align
    while t > align and n % t:
        t -= align
    assert n % t == 0, (n, pref, align)
    return t


def _dot(a, b):
    return jnp.dot(a, b, preferred_element_type=F32)


def _dot_nt(a, b):
    return lax.dot_general(a, b, (((1,), (1,)), ((), ())), preferred_element_type=F32)


def _dot_tn(a, b):
    return lax.dot_general(a, b, (((0,), (0,)), ((), ())), preferred_element_type=F32)


def _split(a):
    hi = a.astype(BF16)
    lo = (a - hi.astype(F32)).astype(BF16)
    return hi, lo


def _dot3(a, w):
    ah, al = _split(a)
    wh, wl = _split(w)
    return _dot(ah, wh) + _dot(al, wh) + _dot(ah, wl)


def _norm_mod(x, g, sc, sh):
    tm, d = x.shape
    nb = sc.shape[0]
    ms = jnp.mean(x * x, axis=-1, keepdims=True)
    y = (x * lax.rsqrt(ms + EPS)) * g
    y3 = y.reshape(tm // nb, nb, d)
    return (y3 * (1.0 + sc)[None] + sh[None]).reshape(tm, d)


def _gate_rows(p, g):
    tm, tn = p.shape
    nb = g.shape[0]
    return (p.reshape(tm // nb, nb, tn) * g[None]).reshape(tm, tn)


def _ada_kernel(c_ref, w_ref, b_ref, o_ref):
    c = c_ref[...]
    o_ref[0] = _dot3(c * jax.nn.sigmoid(c), w_ref[0]) + b_ref[0]


def _ada_mod(c, ada_w, ada_b):
    depth, d, nd = ada_w.shape
    nb = c.shape[0]
    tn = _tile(nd, 768, LANES)
    return pl.pallas_call(
        _ada_kernel,
        grid=(depth, nd // tn),
        in_specs=[pl.BlockSpec((nb, d), lambda i, j: (0, 0)),
                  pl.BlockSpec((1, d, tn), lambda i, j: (i, 0, j)),
                  pl.BlockSpec((1, 1, tn), lambda i, j: (i, 0, j))],
        out_specs=pl.BlockSpec((1, nb, tn), lambda i, j: (i, 0, j)),
        out_shape=jax.ShapeDtypeStruct((depth, nb, nd), F32),
        compiler_params=_params("parallel", "parallel"),
        name="ada_mod",
    )(c, ada_w, ada_b.reshape(depth, 1, nd))


def _s5_kernel(x_ref, g_ref, sc_ref, sh_ref, bmat_ref, cmat_ref, lre_ref, lim_ref, d_ref,
               z_ref, h_s, st_s, bu_s, sa_s, *, tc, nb, hw):
    c = pl.program_id(0)
    j = pl.program_id(1)
    n_j = h_s.shape[0]

    @pl.when(j == 0)
    def _():
        h = _norm_mod(x_ref[...], g_ref[...], sc_ref[...], sh_ref[...])
        for jj in range(n_j):
            h_s[jj] = h[:, jj * LANES:(jj + 1) * LANES]

    @pl.when(c == 0)
    def _():
        st_s[j] = jnp.zeros(st_s.shape[1:], F32)

    u = h_s[j]
    bu_s[...] = _dot(u.astype(BF16), bmat_ref[0])
    lre = jnp.broadcast_to(lre_ref[0], (nb, hw))
    lim = jnp.broadcast_to(lim_ref[0], (nb, hw))
    st = st_s[j]

    def step(t, carry):
        s_re, s_im = carry
        r0 = pl.multiple_of(t * nb, nb)
        b = bu_s[pl.ds(r0, nb), :]
        n_re = lre * s_re - lim * s_im + b[:, :hw]
        n_im = lre * s_im + lim * s_re + b[:, hw:]
        sa_s[pl.ds(r0, nb), :hw] = n_re
        sa_s[pl.ds(r0, nb), hw:] = n_im
        return n_re, n_im

    s_re, s_im = lax.fori_loop(0, tc, step, (st[:, :hw], st[:, hw:]), unroll=8)
    st_s[j, :, :hw] = s_re
    st_s[j, :, hw:] = s_im

    y = _dot(sa_s[...].astype(BF16), cmat_ref[0]) + d_ref[...] * u
    z = 0.5 * y * (1.0 + jnp.tanh(GELU_C0 * (y + GELU_C1 * (y * y * y))))
    z_ref[...] = z.astype(BF16)


def _s5_prep(lam_re, lam_im, log_dt, b_re, b_im, c_re, c_im):
    n_g, n_p = lam_re.shape
    n_h = b_re.shape[-1]
    gpt = LANES // n_h
    n_j = n_g // gpt
    lam = lax.complex(jnp.minimum(lam_re.astype(F32), S5_RE_MAX), lam_im.astype(F32))
    dt = jnp.exp(log_dt.astype(F32))[:, None]
    lam_bar = jnp.exp(lam * dt)
    b_bar = ((lam_bar - 1.0) / lam)[:, :, None] * lax.complex(b_re.astype(F32), b_im.astype(F32))
    eye = jnp.eye(gpt, dtype=F32)
    bb = b_bar.reshape(n_j, gpt, n_p, n_h)
    bre = jnp.einsum('jkph,gk->jghkp', bb.real, eye)
    bim = jnp.einsum('jkph,gk->jghkp', bb.imag, eye)
    bmat = jnp.stack([bre, bim], axis=3).reshape(n_j, LANES, 2 * gpt * n_p)
    cre = jnp.einsum('jghp,kg->jkpgh', c_re.astype(F32).reshape(n_j, gpt, n_h, n_p), eye)
    cim = jnp.einsum('jghp,kg->jkpgh', c_im.astype(F32).reshape(n_j, gpt, n_h, n_p), eye)
    cmat = jnp.stack([cre, -cim], axis=1).reshape(n_j, 2 * gpt * n_p, LANES)
    lre = lam_bar.real.reshape(n_j, 1, gpt * n_p)
    lim = lam_bar.imag.reshape(n_j, 1, gpt * n_p)
    return bmat.astype(BF16), cmat.astype(BF16), lre, lim


def _s5_scan(x, g, sc, sh, bmat, cmat, lre, lim, d_skip, nb):
    n, d = x.shape
    n_j = d // LANES
    hw = lre.shape[-1]
    seq = n // nb
    tc = _tile(seq, S5_CHUNK, 8)
    rows = tc * nb
    kern = functools.partial(_s5_kernel, tc=tc, nb=nb, hw=hw)
    return pl.pallas_call(
        kern,
        grid=(seq // tc, n_j),
        in_specs=[pl.BlockSpec((rows, d), lambda c, j: (c, 0)),
                  pl.BlockSpec((1, d), lambda c, j: (0, 0)),
                  pl.BlockSpec((nb, d), lambda c, j: (0, 0)),
                  pl.BlockSpec((nb, d), lambda c, j: (0, 0)),
                  pl.BlockSpec((1, LANES, 2 * hw), lambda c, j: (j, 0, 0)),
                  pl.BlockSpec((1, 2 * hw, LANES), lambda c, j: (j, 0, 0)),
                  pl.BlockSpec((1, 1, hw), lambda c, j: (j, 0, 0)),
                  pl.BlockSpec((1, 1, hw), lambda c, j: (j, 0, 0)),
                  pl.BlockSpec((1, LANES), lambda c, j: (0, j))],
        out_specs=pl.BlockSpec((rows, LANES), lambda c, j: (c, j)),
        out_shape=jax.ShapeDtypeStruct((n, d), BF16),
        scratch_shapes=[pltpu.VMEM((n_j, rows, LANES), F32),
                        pltpu.VMEM((n_j, nb, 2 * hw), F32),
                        pltpu.VMEM((rows, 2 * hw), F32),
                        pltpu.VMEM((rows, 2 * hw), F32)],
        compiler_params=_params("arbitrary", "arbitrary"),
        name="s5_scan",
    )(x, g, sc, sh, bmat, cmat, lre, lim, d_skip)


def _mm_res_kernel(a_ref, *refs, glu):
    if glu:
        wa_ref, wb_ref, x_ref, g_ref, o_ref = refs
    else:
        wa_ref, x_ref, g_ref, o_ref = refs
    a = a_ref[...]
    p = _dot(a, wa_ref[...])
    if glu:
        p = p * jax.nn.sigmoid(_dot(a, wb_ref[...]))
    o_ref[...] = x_ref[...] + _gate_rows(p, g_ref[...])


def _mm_res(a, w, x, gate, glu):
    n, kdim = a.shape
    d = x.shape[1]
    nb = gate.shape[0]
    tm = _tile(n, 1024, 8 * nb)
    tn = _tile(d, 512, LANES)
    nj = d // tn
    w_specs = [pl.BlockSpec((kdim, tn), lambda i, j: (0, j))]
    w_args = [w]
    if glu:
        w_specs.append(pl.BlockSpec((kdim, tn), lambda i, j: (0, j + nj)))
        w_args.append(w)
    return pl.pallas_call(
        functools.partial(_mm_res_kernel, glu=glu),
        grid=(n // tm, nj),
        in_specs=[pl.BlockSpec((tm, kdim), lambda i, j: (i, 0))] + w_specs + [
            pl.BlockSpec((tm, tn), lambda i, j: (i, j)),
            pl.BlockSpec((nb, tn), lambda i, j: (0, j))],
        out_specs=pl.BlockSpec((tm, tn), lambda i, j: (i, j)),
        out_shape=jax.ShapeDtypeStruct((n, d), F32),
        compiler_params=_params("parallel", "arbitrary"),
        name="glu_res" if glu else "proj_res",
    )(a, *w_args, x, gate)


def _hg_proj_kernel(x_ref, g_ref, sc_ref, sh_ref, w_ref, la_ref, l1_ref, o_ref, h_s, *, f_lo, f_hi):
    j = pl.program_id(1)

    @pl.when(j == 0)
    def _():
        h_s[...] = _norm_mod(x_ref[...], g_ref[...], sc_ref[...], sh_ref[...]).astype(BF16)

    p = _dot(h_s[...], w_ref[...])
    is_f = jnp.logical_and(j >= f_lo, j < f_hi)

    @pl.when(is_f)
    def _():
        ls = jnp.minimum(p, 0.0) - jnp.log1p(jnp.exp(-jnp.abs(p)))
        bt = l1_ref[...] + ls
        la = la_ref[...]
        o_ref[...] = jnp.maximum(la, bt) + jnp.log1p(jnp.exp(-jnp.abs(la - bt)))

    @pl.when(jnp.logical_not(is_f))
    def _():
        o_ref[...] = p


def _hg_proj(x, g, sc, sh, w_in, log_lb, log1m_lb, fdim):
    n, d = x.shape
    nb = sc.shape[0]
    cols = w_in.shape[1]
    tm = _tile(n, 1024, 8 * nb)
    tn = _tile(fdim, 512, LANES)
    f_lo = fdim // tn
    f_hi = 2 * fdim // tn

    def lb_map(i, j):
        return (0, jnp.clip(j - f_lo, 0, f_lo - 1))

    return pl.pallas_call(
        functools.partial(_hg_proj_kernel, f_lo=f_lo, f_hi=f_hi),
        grid=(n // tm, cols // tn),
        in_specs=[pl.BlockSpec((tm, d), lambda i, j: (i, 0)),
                  pl.BlockSpec((1, d), lambda i, j: (0, 0)),
                  pl.BlockSpec((nb, d), lambda i, j: (0, 0)),
                  pl.BlockSpec((nb, d), lambda i, j: (0, 0)),
                  pl.BlockSpec((d, tn), lambda i, j: (0, j)),
                  pl.BlockSpec((1, tn), lb_map),
                  pl.BlockSpec((1, tn), lb_map)],
        out_specs=pl.BlockSpec((tm, tn), lambda i, j: (i, j)),
        out_shape=jax.ShapeDtypeStruct((n, cols), F32),
        scratch_shapes=[pltpu.VMEM((tm, d), BF16)],
        compiler_params=_params("parallel", "arbitrary"),
        name="hg_proj",
    )(x, g, sc, sh, w_in, log_lb, log1m_lb)


def _hg_consts(ch):
    levels = []
    m = ch // 2
    while m >= 1:
        levels.append(m)
        m //= 2
    dm = np.zeros((len(levels) + 2, ch, ch), np.float32)
    mk = np.zeros((len(levels) + 1, ch, ch), np.float32)
    for li, m in enumerate(levels):
        for t in range(ch):
            mid = (t // (2 * m)) * 2 * m + m - 1
            if t > mid:
                dm[li, t, mid + 1:t + 1] = 1.0
                mk[li, t, mid - m + 1:mid + 1] = 1.0
            else:
                dm[li, t, t + 1:mid + 1] = 1.0
    for t in range(ch):
        dm[-2, t, :t + 1] = 1.0
        dm[-1, t, t + 1:] = 1.0
        mk[-1, t, t] = 1.0
    return dm.reshape(-1, ch), mk


def _hg_rec_kernel(q_ref, lf_ref, v_ref, gt_ref, dmat_ref, mask_ref, gn_ref, o_ref, st_s, ob_s,
                   *, nb, ch):
    c = pl.program_id(1)

    @pl.when(c == 0)
    def _():
        st_s[...] = jnp.zeros(st_s.shape, F32)

    dmat = dmat_ref[...]
    n_lv = mask_ref.shape[0] - 1
    gn = gn_ref[...]
    for b in range(nb):
        rows = pl.ds(b, ch, stride=nb)
        q = q_ref[rows, :]
        lf = lf_ref[rows, :]
        v = v_ref[rows, :].astype(BF16)
        gate = gt_ref[rows, :]
        k = 1.0 - jnp.exp(lf)
        l1 = lf.astype(BF16)
        r1 = lf - l1.astype(F32)
        l2 = r1.astype(BF16)
        l3 = (r1 - l2.astype(F32)).astype(BF16)
        e = jnp.exp(_dot(dmat, l1) + _dot(dmat, l2) + _dot(dmat, l3))
        s = mask_ref[n_lv] * _dot_nt(q.astype(BF16), k.astype(BF16))
        for li in range(n_lv):
            em = e[li * ch:(li + 1) * ch]
            s = s + mask_ref[li] * _dot_nt((q * em).astype(BF16), (k * em).astype(BF16))
        e_in = e[n_lv * ch:(n_lv + 1) * ch]
        e_out = e[(n_lv + 1) * ch:(n_lv + 2) * ch]
        st = st_s[b]
        o = _dot(s.astype(BF16), v) + _dot_nt((q * e_in).astype(BF16), st.astype(BF16))
        st_s[b] = e_in[ch - 1:ch, :] * st + _dot_tn(v, (k * e_out).astype(BF16))
        o = o * lax.rsqrt(jnp.mean(o * o, axis=-1, keepdims=True) + EPS) * gn
        ob_s[rows, :] = o * (gate * jax.nn.sigmoid(gate))
    o_ref[...] = ob_s[...].astype(BF16)


def _hg_rec(proj, g_norm, nb, heads, dk):
    n = proj.shape[0]
    seq = n // nb
    ch = _tile(seq, HG_CHUNK, 8)
    rows = ch * nb
    dm, mk = _hg_consts(ch)
    blk = lambda off: pl.BlockSpec((rows, dk), lambda h, c: (c, off + h))
    return pl.pallas_call(
        functools.partial(_hg_rec_kernel, nb=nb, ch=ch),
        grid=(heads, seq // ch),
        in_specs=[blk(0), blk(heads), blk(2 * heads), blk(3 * heads),
                  pl.BlockSpec(dm.shape, lambda h, c: (0, 0)),
                  pl.BlockSpec(mk.shape, lambda h, c: (0, 0, 0)),
                  pl.BlockSpec((1, dk), lambda h, c: (0, 0))],
        out_specs=pl.BlockSpec((rows, dk), lambda h, c: (c, h)),
        out_shape=jax.ShapeDtypeStruct((n, heads * dk), BF16),
        scratch_shapes=[pltpu.VMEM((nb, dk, dk), F32), pltpu.VMEM((rows, dk), F32)],
        compiler_params=_params("parallel", "arbitrary"),
        name="hg_rec",
    )(proj, proj, proj, proj, jnp.asarray(dm, BF16), jnp.asarray(mk, F32), g_norm)


def _router_kernel(x_ref, g_ref, sc_ref, sh_ref, wh_ref, wl_ref, br_ref, h_ref, idx_ref, gate_ref):
    h = _norm_mod(x_ref[...], g_ref[...], sc_ref[...], sh_ref[...])
    h_ref[...] = h
    hh, hl = _split(h)
    wh = wh_ref[...]
    vals = _dot(hh, wh) + _dot(hl, wh) + _dot(hh, wl_ref[...]) + br_ref[...]
    tm, n_exp = vals.shape
    lane = lax.broadcasted_iota(jnp.int32, (tm, n_exp), 1)
    tops, idxs = [], []
    for _ in range(TOP_K):
        m = jnp.max(vals, axis=-1, keepdims=True)
        i = jnp.min(jnp.where(vals == m, lane, n_exp), axis=-1, keepdims=True)
        tops.append(m)
        idxs.append(i)
        vals = jnp.where(lane == i, -jnp.inf, vals)
    es = [jnp.exp(t - tops[0]) for t in tops]
    den = es[0] + es[1] + es[2] + es[3]
    lane_o = lax.broadcasted_iota(jnp.int32, idx_ref.shape, 1)
    io = jnp.zeros(idx_ref.shape, jnp.int32)
    go = jnp.zeros(gate_ref.shape, F32)
    for kk in range(TOP_K):
        io = jnp.where(lane_o == kk, idxs[kk], io)
        go = jnp.where(lane_o == kk, es[kk] / den, go)
    idx_ref[...] = io
    gate_ref[...] = go


def _router(x, g, sc, sh, w_r, b_r):
    n, d = x.shape
    nb = sc.shape[0]
    n_exp = w_r.shape[1]
    tm = _tile(n, 512, 8 * nb)
    wh, wl = _split(w_r.astype(F32))
    row = lambda w: pl.BlockSpec((tm, w), lambda i: (i, 0))
    full = lambda a: pl.BlockSpec(a.shape, lambda i: (0, 0))
    b2 = b_r.astype(F32).reshape(1, n_exp)
    return pl.pallas_call(
        _router_kernel,
        grid=(n // tm,),
        in_specs=[row(d), full(g), full(sc), full(sh), full(wh), full(wl), full(b2)],
        out_specs=[row(d), row(LANES), row(LANES)],
        out_shape=[jax.ShapeDtypeStruct((n, d), F32),
                   jax.ShapeDtypeStruct((n, LANES), jnp.int32),
                   jax.ShapeDtypeStruct((n, LANES), F32)],
        compiler_params=_params("parallel"),
        name="router",
    )(x, g, sc, sh, wh, wl, b2)


def _moe_plan(top_idx, n_exp, tmb):
    n = top_idx.shape[0]
    na = n * TOP_K
    e_flat = top_idx.reshape(-1)
    onehot = jax.nn.one_hot(e_flat, n_exp, dtype=jnp.int32)
    csum = jnp.cumsum(onehot, axis=0)
    counts = csum[-1]
    rank = jnp.take_along_axis(csum, e_flat[:, None], axis=1)[:, 0] - 1
    nblk = (counts + tmb - 1) // tmb
    blk_end = jnp.cumsum(nblk)
    blk_start = blk_end - nblk
    n_used = blk_end[-1]
    dest = blk_start[e_flat] * tmb + rank
    n_blocks = na // tmb + n_exp
    flat = jnp.arange(na, dtype=jnp.int32)
    tok = flat // TOP_K
    row_tok = jnp.zeros((n_blocks * tmb,), jnp.int32).at[dest].set(tok)
    spare = na + jnp.arange(n_blocks * tmb, dtype=jnp.int32) % SUBLANES
    row_dst = spare.at[dest].set((flat % TOP_K) * n + tok)
    blk = jnp.arange(n_blocks, dtype=jnp.int32)
    be = jnp.minimum(jnp.searchsorted(blk_end, blk, side='right'), n_exp - 1).astype(jnp.int32)
    used = blk < n_used
    be = jnp.where(used, be, be[n_used - 1])
    cnt = jnp.clip(counts[be] - (blk - blk_start[be]) * tmb, 0, tmb)
    cnt = jnp.where(used, (cnt + SUBLANES - 1) // SUBLANES * SUBLANES, 0).astype(jnp.int32)
    return (be, cnt, n_used.reshape(1).astype(jnp.int32),
            row_tok.reshape(n_blocks, 1, tmb), row_dst.reshape(n_blocks, 1, tmb))


def _moe_kernel(be_ref, cnt_ref, nu_ref, tok_ref, dst_ref, h_hbm, wgu_ref, bgu_ref, wdn_ref, bdn_ref,
                y_hbm, xbuf, ybuf, gsem, ssem, *, ff):
    i = pl.program_id(0)

    @pl.when(i == 0)
    def _():
        xbuf[...] = jnp.zeros(xbuf.shape, F32)
        n_real = y_hbm.shape[0] - SUBLANES
        spare = pltpu.make_async_copy(xbuf.at[pl.ds(0, SUBLANES)], y_hbm.at[pl.ds(n_real, SUBLANES)], ssem)
        spare.start()
        spare.wait()

    @pl.when(i < nu_ref[0])
    def _():
        cnt = pl.multiple_of(cnt_ref[i], SUBLANES)

        def gather(r, carry):
            t = tok_ref[0, 0, r]
            pltpu.make_async_copy(h_hbm.at[pl.ds(t, 1)], xbuf.at[pl.ds(r, 1)], gsem).start()
            return carry

        lax.fori_loop(0, cnt, gather, 0)
        pltpu.make_async_copy(h_hbm.at[pl.ds(0, cnt)], xbuf.at[pl.ds(0, cnt)], gsem).wait()

        gu = _dot(xbuf[...].astype(BF16), wgu_ref[0]) + bgu_ref[0]
        gt = jnp.minimum(gu[:, :ff], SWIGLU_LIMIT)
        up = jnp.clip(gu[:, ff:], -SWIGLU_LIMIT, SWIGLU_LIMIT)
        act = (up + 1.0) * gt * jax.nn.sigmoid(SWIGLU_ALPHA * gt)
        ybuf[...] = _dot(act.astype(BF16), wdn_ref[0]) + bdn_ref[0]

        def scatter(r, carry):
            t = dst_ref[0, 0, r]
            pltpu.make_async_copy(ybuf.at[pl.ds(r, 1)], y_hbm.at[pl.ds(t, 1)], ssem).start()
            return carry

        lax.fori_loop(0, cnt, scatter, 0)
        pltpu.make_async_copy(ybuf.at[pl.ds(0, cnt)], y_hbm.at[pl.ds(0, cnt)], ssem).wait()


def _moe_experts(h, plan, w_gu, b_gu, w_dn, b_dn):
    be, cnt, n_used, row_tok, row_dst = plan
    n, d = h.shape
    n_exp, _, ff2 = w_gu.shape
    ff = ff2 // 2
    n_blocks, _, tmb = row_tok.shape
    smem = lambda: pl.BlockSpec((1, 1, tmb), lambda i, be, cnt, nu: (i, 0, 0), memory_space=pltpu.SMEM)
    grid_spec = pltpu.PrefetchScalarGridSpec(
        num_scalar_prefetch=3,
        grid=(n_blocks,),
        in_specs=[smem(), smem(),
                  pl.BlockSpec(memory_space=pl.ANY),
                  pl.BlockSpec((1, d, ff2), lambda i, be, cnt, nu: (be[i], 0, 0)),
                  pl.BlockSpec((1, 1, ff2), lambda i, be, cnt, nu: (be[i], 0, 0)),
                  pl.BlockSpec((1, ff, d), lambda i, be, cnt, nu: (be[i], 0, 0)),
                  pl.BlockSpec((1, 1, d), lambda i, be, cnt, nu: (be[i], 0, 0))],
        out_specs=pl.BlockSpec(memory_space=pl.ANY),
        scratch_shapes=[pltpu.VMEM((tmb, d), F32), pltpu.VMEM((tmb, d), F32),
                        pltpu.SemaphoreType.DMA(()), pltpu.SemaphoreType.DMA(())],
    )
    return pl.pallas_call(
        functools.partial(_moe_kernel, ff=ff),
        grid_spec=grid_spec,
        out_shape=jax.ShapeDtypeStruct((TOP_K * n + SUBLANES, d), F32),
        compiler_params=_params("arbitrary"),
        name="moe_experts",
    )(be, cnt, n_used, row_tok, row_dst, h, w_gu, b_gu.reshape(n_exp, 1, ff2),
      w_dn, b_dn.reshape(n_exp, 1, d))


def _combine_kernel(x_ref, y0_ref, y1_ref, y2_ref, y3_ref, gate_ref, g_ref, o_ref):
    gt = gate_ref[...]
    acc = gt[:, 0:1] * y0_ref[...]
    for kk, y_ref in ((1, y1_ref), (2, y2_ref), (3, y3_ref)):
        acc = acc + gt[:, kk:kk + 1] * y_ref[...]
    o_ref[...] = x_ref[...] + _gate_rows(acc, g_ref[...])


def _combine(x, y4, gates, g2):
    n, d = x.shape
    nb = g2.shape[0]
    tm = _tile(n, 256, 8 * nb)
    nt = n // tm
    yspec = lambda kk: pl.BlockSpec((tm, d), lambda i: (kk * nt + i, 0))
    return pl.pallas_call(
        _combine_kernel,
        grid=(n // tm,),
        in_specs=[pl.BlockSpec((tm, d), lambda i: (i, 0)),
                  yspec(0), yspec(1), yspec(2), yspec(3),
                  pl.BlockSpec((tm, LANES), lambda i: (i, 0)),
                  pl.BlockSpec((nb, d), lambda i: (0, 0))],
        out_specs=pl.BlockSpec((tm, d), lambda i: (i, 0)),
        out_shape=jax.ShapeDtypeStruct((n, d), F32),
        compiler_params=_params("parallel"),
        name="moe_combine",
    )(x, y4, y4, y4, y4, gates, g2)


def _final_kernel(x_ref, g_ref, o_ref):
    x = x_ref[...]
    ms = jnp.mean(x * x, axis=-1, keepdims=True)
    o_ref[...] = (x * lax.rsqrt(ms + EPS)) * g_ref[...]


def _final_norm(x, g):
    n, d = x.shape
    tm = _tile(n, 512, 8)
    return pl.pallas_call(
        _final_kernel,
        grid=(n // tm,),
        in_specs=[pl.BlockSpec((tm, d), lambda i: (i, 0)), pl.BlockSpec((1, d), lambda i: (0, 0))],
        out_specs=pl.BlockSpec((tm, d), lambda i: (i, 0)),
        out_shape=jax.ShapeDtypeStruct((n, d), F32),
        compiler_params=_params("parallel"),
        name="final_norm",
    )(x, g)


def kernel(x, c, ada_w, ada_b, norm_mix, norm_ffn, norm_final, s5_lambda_re, s5_lambda_im, s5_log_dt,
           s5_b_re, s5_b_im, s5_c_re, s5_c_im, s5_d, s5_w_glu, hg_w_in, hg_lb_raw, hg_norm, hg_w_out,
           router_w, router_b, moe_w_gate_up, moe_b_gate_up, moe_w_down, moe_b_down):
    bsz, seq, d = x.shape
    depth = ada_w.shape[0]
    n = bsz * seq
    n_exp = router_w.shape[-1]
    dv = hg_norm.shape[-1]
    heads = d // dv
    fdim = (hg_w_in.shape[-1] - 2 * d) // 2
    dk = fdim // heads
    assert dk == LANES and dv == LANES, "one HGRN2 head per 128-lane tile"

    lb_p = jax.nn.softmax(hg_lb_raw.astype(F32), axis=0)
    lb_all = jnp.cumsum(lb_p, axis=0) - lb_p[0]
    mod = _ada_mod(c.astype(F32), ada_w, ada_b)

    xs = x.astype(F32).transpose(1, 0, 2).reshape(n, d)
    row = lambda v: v.astype(F32).reshape(1, -1)
    for i in range(depth):
        sh1, sc1, g1, sh2, sc2, g2 = [mod[i, :, k * d:(k + 1) * d] for k in range(N_ADA)]
        j = i // 2
        if i % 2 == 0:
            bmat, cmat, lre, lim = _s5_prep(s5_lambda_re[j], s5_lambda_im[j], s5_log_dt[j],
                                            s5_b_re[j], s5_b_im[j], s5_c_re[j], s5_c_im[j])
            z = _s5_scan(xs, row(norm_mix[i]), sc1, sh1, bmat, cmat, lre, lim, row(s5_d[j]), bsz)
            xs = _mm_res(z, s5_w_glu[j].astype(BF16), xs, g1, glu=True)
        else:
            lb = lb_all[i]
            proj = _hg_proj(xs, row(norm_mix[i]), sc1, sh1, hg_w_in[j].astype(BF16),
                            row(jnp.log(lb)), row(jnp.log1p(-lb)), fdim)
            o = _hg_rec(proj, row(hg_norm[j]), bsz, heads, dk)
            xs = _mm_res(o, hg_w_out[j].astype(BF16), xs, g1, glu=False)
        h2, top_idx, gates = _router(xs, row(norm_ffn[i]), sc2, sh2, router_w[i], router_b[i])
        plan = _moe_plan(top_idx[:, :TOP_K], n_exp, MOE_ROWS)
        y4 = _moe_experts(h2, plan, moe_w_gate_up[i].astype(BF16), moe_b_gate_up[i].astype(F32),
                          moe_w_down[i].astype(BF16), moe_b_down[i].astype(F32))
        xs = _combine(xs, y4, gates, g2)
    out = _final_norm(xs, row(norm_final))
    return out.reshape(seq, bsz, d).transpose(1, 0, 2).astype(x.dtype)
```

```python
import functools

import numpy as np
import jax
import jax.numpy as jnp
from jax import lax
from jax.experimental import pallas as pl
from jax.experimental.pallas import tpu as pltpu

F32 = jnp.float32
BF16 = jnp.bfloat16
I32 = jnp.int32

EPS = 1e-6
N_ADA = 6
TOP_K = 4
S5_RE_MAX = -1e-4
SWIGLU_LIMIT = 7.0
SWIGLU_ALPHA = 1.702
GELU_C0 = 0.7978845608028654
GELU_C1 = 0.044715

LANES = 128
SUBLANES = 8
HG_CHUNK = 64
HG_GROUP = 4
S5_CHUNK = 128
MOE_ROWS = 512
HI16 = -65536
VMEM_LIMIT = 56 * 1024 * 1024


def _params(*sem):
    return pltpu.CompilerParams(dimension_semantics=sem, vmem_limit_bytes=VMEM_LIMIT)


def _tile(n, pref, align):
    if n <= pref:
        return n
    t = (pref // align) * align
    while t > align and n % t:
        t -= align
    assert n % t == 0, (n, pref, align)
    return t


def _dot(a, b):
    return jnp.dot(a, b, preferred_element_type=F32)


def _dot_nt(a, b):
    return lax.dot_general(a, b, (((1,), (1,)), ((), ())), preferred_element_type=F32)


def _dot_tn(a, b):
    return lax.dot_general(a, b, (((0,), (0,)), ((), ())), preferred_element_type=F32)


def _split(a):
    hi = a.astype(BF16)
    lo = (a - hi.astype(F32)).astype(BF16)
    return hi, lo


def _dot3(a, w):
    ah, al = _split(a)
    wh, wl = _split(w)
    return _dot(ah, wh) + _dot(al, wh) + _dot(ah, wl)


def _norm_mod(x, g, sc, sh):
    tm, d = x.shape
    nb = sc.shape[0]
    ms = jnp.mean(x * x, axis=-1, keepdims=True)
    y = (x * lax.rsqrt(ms + EPS)) * g
    y3 = y.reshape(tm // nb, nb, d)
    return (y3 * (1.0 + sc)[None] + sh[None]).reshape(tm, d)


def _gate_rows(p, g):
    tm, tn = p.shape
    nb = g.shape[0]
    return (p.reshape(tm // nb, nb, tn) * g[None]).reshape(tm, tn)


def _pack_pair(lo, hi):
    lo_b = lax.bitcast_convert_type(lo.astype(BF16).astype(F32), I32)
    hi_b = lax.bitcast_convert_type(hi.astype(BF16).astype(F32), I32)
    return lax.shift_right_logical(lo_b, 16) | (hi_b & HI16)


def _unpack_pair(p):
    lo = lax.bitcast_convert_type(lax.shift_left(p, 16), F32)
    hi = lax.bitcast_convert_type(p & HI16, F32)
    return lo, hi


def _ada_kernel(c_ref, w_ref, b_ref, o_ref):
    c = c_ref[...]
    o_ref[0] = _dot3(c * jax.nn.sigmoid(c), w_ref[0]) + b_ref[0]


def _ada_mod(c, ada_w, ada_b):
    depth, d, nd = ada_w.shape
    nb = c.shape[0]
    tn = _tile(nd, 768, LANES)
    return pl.pallas_call(
        _ada_kernel,
        grid=(depth, nd // tn),
        in_specs=[pl.BlockSpec((nb, d), lambda i, j: (0, 0)),
                  pl.BlockSpec((1, d, tn), lambda i, j: (i, 0, j)),
                  pl.BlockSpec((1, 1, tn), lambda i, j: (i, 0, j))],
        out_specs=pl.BlockSpec((1, nb, tn), lambda i, j: (i, 0, j)),
        out_shape=jax.ShapeDtypeStruct((depth, nb, nd), F32),
        compiler_params=_params("parallel", "parallel"),
        name="ada_mod",
    )(c, ada_w, ada_b.reshape(depth, 1, nd))


def _s5_kernel(x_ref, g_ref, sc_ref, sh_ref, bmat_ref, cmat_ref, lre_ref, lim_ref, d_ref,
               z_ref, h_s, st_s, bu_s, sa_s, *, tc, nb, hw):
    c = pl.program_id(0)
    j = pl.program_id(1)
    n_j = h_s.shape[0]

    @pl.when(j == 0)
    def _():
        h = _norm_mod(x_ref[...], g_ref[...], sc_ref[...], sh_ref[...])
        for jj in range(n_j):
            h_s[jj] = h[:, jj * LANES:(jj + 1) * LANES]

    @pl.when(c == 0)
    def _():
        st_s[j] = jnp.zeros(st_s.shape[1:], F32)

    u = h_s[j]
    bu_s[...] = _dot(u.astype(BF16), bmat_ref[0])
    lre = jnp.broadcast_to(lre_ref[0], (nb, hw))
    lim = jnp.broadcast_to(lim_ref[0], (nb, hw))
    st = st_s[j]

    def step(t, carry):
        s_re, s_im = carry
        r0 = pl.multiple_of(t * nb, nb)
        b = bu_s[pl.ds(r0, nb), :]
        n_re = lre * s_re - lim * s_im + b[:, :hw]
        n_im = lre * s_im + lim * s_re + b[:, hw:]
        sa_s[pl.ds(r0, nb), :hw] = n_re
        sa_s[pl.ds(r0, nb), hw:] = n_im
        return n_re, n_im

    s_re, s_im = lax.fori_loop(0, tc, step, (st[:, :hw], st[:, hw:]), unroll=8)
    st_s[j, :, :hw] = s_re
    st_s[j, :, hw:] = s_im

    y = _dot(sa_s[...].astype(BF16), cmat_ref[0]) + d_ref[...] * u
    z = 0.5 * y * (1.0 + jnp.tanh(GELU_C0 * (y + GELU_C1 * (y * y * y))))
    z_ref[...] = z.astype(BF16)


def _s5_prep(lam_re, lam_im, log_dt, b_re, b_im, c_re, c_im):
    n_g, n_p = lam_re.shape
    n_h = b_re.shape[-1]
    gpt = LANES // n_h
    n_j = n_g // gpt
    lam = lax.complex(jnp.minimum(lam_re.astype(F32), S5_RE_MAX), lam_im.astype(F32))
    dt = jnp.exp(log_dt.astype(F32))[:, None]
    lam_bar = jnp.exp(lam * dt)
    b_bar = ((lam_bar - 1.0) / lam)[:, :, None] * lax.complex(b_re.astype(F32), b_im.astype(F32))
    eye = jnp.eye(gpt, dtype=F32)
    bb = b_bar.reshape(n_j, gpt, n_p, n_h)
    bre = jnp.einsum('jkph,gk->jghkp', bb.real, eye)
    bim = jnp.einsum('jkph,gk->jghkp', bb.imag, eye)
    bmat = jnp.stack([bre, bim], axis=3).reshape(n_j, LANES, 2 * gpt * n_p)
    cre = jnp.einsum('jghp,kg->jkpgh', c_re.astype(F32).reshape(n_j, gpt, n_h, n_p), eye)
    cim = jnp.einsum('jghp,kg->jkpgh', c_im.astype(F32).reshape(n_j, gpt, n_h, n_p), eye)
    cmat = jnp.stack([cre, -cim], axis=1).reshape(n_j, 2 * gpt * n_p, LANES)
    lre = lam_bar.real.reshape(n_j, 1, gpt * n_p)
    lim = lam_bar.imag.reshape(n_j, 1, gpt * n_p)
    return bmat.astype(BF16), cmat.astype(BF16), lre, lim


def _s5_scan(x, g, sc, sh, bmat, cmat, lre, lim, d_skip, nb):
    n, d = x.shape
    n_j = d // LANES
    hw = lre.shape[-1]
    seq = n // nb
    tc = _tile(seq, S5_CHUNK, 8)
    rows = tc * nb
    kern = functools.partial(_s5_kernel, tc=tc, nb=nb, hw=hw)
    return pl.pallas_call(
        kern,
        grid=(seq // tc, n_j),
        in_specs=[pl.BlockSpec((rows, d), lambda c, j: (c, 0)),
                  pl.BlockSpec((1, d), lambda c, j: (0, 0)),
                  pl.BlockSpec((nb, d), lambda c, j: (0, 0)),
                  pl.BlockSpec((nb, d), lambda c, j: (0, 0)),
                  pl.BlockSpec((1, LANES, 2 * hw), lambda c, j: (j, 0, 0)),
                  pl.BlockSpec((1, 2 * hw, LANES), lambda c, j: (j, 0, 0)),
                  pl.BlockSpec((1, 1, hw), lambda c, j: (j, 0, 0)),
                  pl.BlockSpec((1, 1, hw), lambda c, j: (j, 0, 0)),
                  pl.BlockSpec((1, LANES), lambda c, j: (0, j))],
        out_specs=pl.BlockSpec((rows, LANES), lambda c, j: (c, j)),
        out_shape=jax.ShapeDtypeStruct((n, d), BF16),
        scratch_shapes=[pltpu.VMEM((n_j, rows, LANES), F32),
                        pltpu.VMEM((n_j, nb, 2 * hw), F32),
                        pltpu.VMEM((rows, 2 * hw), F32),
                        pltpu.VMEM((rows, 2 * hw), F32)],
        compiler_params=_params("arbitrary", "arbitrary"),
        name="s5_scan",
    )(x, g, sc, sh, bmat, cmat, lre, lim, d_skip)


def _mm_res_kernel(a_ref, *refs, glu):
    if glu:
        wa_ref, wb_ref, x_ref, g_ref, o_ref = refs
    else:
        wa_ref, x_ref, g_ref, o_ref = refs
    a = a_ref[...]
    p = _dot(a, wa_ref[...])
    if glu:
        p = p * jax.nn.sigmoid(_dot(a, wb_ref[...]))
    o_ref[...] = x_ref[...] + _gate_rows(p, g_ref[...])


def _mm_res(a, w, x, gate, glu):
    n, kdim = a.shape
    d = x.shape[1]
    nb = gate.shape[0]
    tm = _tile(n, 1024, 8 * nb)
    tn = _tile(d, 512, LANES)
    nj = d // tn
    w_specs = [pl.BlockSpec((kdim, tn), lambda i, j: (0, j))]
    w_args = [w]
    if glu:
        w_specs.append(pl.BlockSpec((kdim, tn), lambda i, j: (0, j + nj)))
        w_args.append(w)
    return pl.pallas_call(
        functools.partial(_mm_res_kernel, glu=glu),
        grid=(n // tm, nj),
        in_specs=[pl.BlockSpec((tm, kdim), lambda i, j: (i, 0))] + w_specs + [
            pl.BlockSpec((tm, tn), lambda i, j: (i, j)),
            pl.BlockSpec((nb, tn), lambda i, j: (0, j))],
        out_specs=pl.BlockSpec((tm, tn), lambda i, j: (i, j)),
        out_shape=jax.ShapeDtypeStruct((n, d), F32),
        compiler_params=_params("parallel", "arbitrary"),
        name="glu_res" if glu else "proj_res",
    )(a, *w_args, x, gate)


def _hg_proj_kernel(x_ref, g_ref, sc_ref, sh_ref, w_ref, la_ref, l1_ref, o_ref, h_s, *, f_lo, f_hi):
    j = pl.program_id(1)

    @pl.when(j == 0)
    def _():
        h_s[...] = _norm_mod(x_ref[...], g_ref[...], sc_ref[...], sh_ref[...]).astype(BF16)

    p = _dot(h_s[...], w_ref[...])
    is_f = jnp.logical_and(j >= f_lo, j < f_hi)

    @pl.when(is_f)
    def _():
        ls = jnp.minimum(p, 0.0) - jnp.log1p(jnp.exp(-jnp.abs(p)))
        bt = l1_ref[...] + ls
        la = la_ref[...]
        o_ref[...] = jnp.maximum(la, bt) + jnp.log1p(jnp.exp(-jnp.abs(la - bt)))

    @pl.when(jnp.logical_not(is_f))
    def _():
        o_ref[...] = p


def _hg_proj(x, g, sc, sh, w_in, log_lb, log1m_lb, fdim):
    n, d = x.shape
    nb = sc.shape[0]
    cols = w_in.shape[1]
    tm = _tile(n, 1024, 8 * nb)
    tn = _tile(fdim, 512, LANES)
    f_lo = fdim // tn
    f_hi = 2 * fdim // tn

    def lb_map(i, j):
        return (0, jnp.clip(j - f_lo, 0, f_lo - 1))

    return pl.pallas_call(
        functools.partial(_hg_proj_kernel, f_lo=f_lo, f_hi=f_hi),
        grid=(n // tm, cols // tn),
        in_specs=[pl.BlockSpec((tm, d), lambda i, j: (i, 0)),
                  pl.BlockSpec((1, d), lambda i, j: (0, 0)),
                  pl.BlockSpec((nb, d), lambda i, j: (0, 0)),
                  pl.BlockSpec((nb, d), lambda i, j: (0, 0)),
                  pl.BlockSpec((d, tn), lambda i, j: (0, j)),
                  pl.BlockSpec((1, tn), lb_map),
                  pl.BlockSpec((1, tn), lb_map)],
        out_specs=pl.BlockSpec((tm, tn), lambda i, j: (i, j)),
        out_shape=jax.ShapeDtypeStruct((n, cols), F32),
        scratch_shapes=[pltpu.VMEM((tm, d), BF16)],
        compiler_params=_params("parallel", "arbitrary"),
        name="hg_proj",
    )(x, g, sc, sh, w_in, log_lb, log1m_lb)


def _hg_levels(ch):
    levels = []
    m = ch // 2
    while m >= 1:
        levels.append(m)
        m //= 2
    return levels


def _hg_masks(ch):
    levels = _hg_levels(ch)
    mk = np.zeros((len(levels) + 1, ch, ch), np.float32)
    for li, m in enumerate(levels):
        for t in range(ch):
            mid = (t // (2 * m)) * 2 * m + m - 1
            if t > mid:
                mk[li, t, mid - m + 1:mid + 1] = 1.0
    mk[-1] = np.eye(ch, dtype=np.float32)
    return mk


def _cumsum_rows(x, t_idx):
    sh = 1
    while sh < x.shape[0]:
        x = x + jnp.where(t_idx >= sh, pltpu.roll(x, sh, 0), 0.0)
        sh *= 2
    return x


def _level_ref(bc, m, sub3):
    ch, dk = bc.shape
    if 2 * m >= SUBLANES:
        bp = bc.reshape(ch // (2 * m), 2 * m, dk)
        return jnp.broadcast_to(bp[:, m - 1:m, :], bp.shape).reshape(ch, dk)
    b3 = bc.reshape(ch // SUBLANES, SUBLANES, dk)
    r = None
    for start in range(0, SUBLANES, 2 * m):
        cand = jnp.broadcast_to(b3[:, start + m - 1:start + m, :], b3.shape)
        r = cand if r is None else jnp.where(sub3 >= start, cand, r)
    return r.reshape(ch, dk)


def _hg_rec_kernel(q_ref, lf_ref, v_ref, gt_ref, mask_ref, gn_ref, o_ref, st_s, ob_s, *, nb, ch):
    c = pl.program_id(1)

    @pl.when(c == 0)
    def _():
        st_s[...] = jnp.zeros(st_s.shape, F32)

    dk = q_ref.shape[-1]
    levels = _hg_levels(ch)
    n_lv = len(levels)
    gn = gn_ref[...]
    t_idx = lax.broadcasted_iota(I32, (ch, dk), 0)
    sub3 = lax.broadcasted_iota(I32, (ch // SUBLANES, SUBLANES, dk), 1)

    for b0 in range(0, nb, HG_GROUP):
        grp = range(b0, min(b0 + HG_GROUP, nb))
        rows = {b: pl.ds(b, ch, stride=nb) for b in grp}
        qk, qms, kms, q_in, k_out, decay = {}, {}, {}, {}, {}, {}
        for b in grp:
            q = q_ref[rows[b], :]
            lf = lf_ref[rows[b], :]
            k = 1.0 - jnp.exp(lf)
            bc = _cumsum_rows(lf, t_idx)
            qk[b] = (q.astype(BF16), k.astype(BF16))
            qms[b], kms[b] = [], []
            for m in levels:
                e = jnp.exp(-jnp.abs(bc - _level_ref(bc, m, sub3)))
                qms[b].append((q * e).astype(BF16))
                kms[b].append((k * e).astype(BF16))
            e_in = jnp.exp(bc)
            b_end = bc[ch - 1:ch, :]
            q_in[b] = (q * e_in).astype(BF16)
            k_out[b] = (k * jnp.exp(b_end - bc)).astype(BF16)
            decay[b] = jnp.exp(b_end)
        scores = {}
        for b in grp:
            s = mask_ref[n_lv] * _dot_nt(*qk[b])
            for li in range(n_lv):
                s = s + mask_ref[li] * _dot_nt(qms[b][li], kms[b][li])
            scores[b] = s.astype(BF16)
        outs = {}
        for b in grp:
            v = v_ref[rows[b], :].astype(BF16)
            st = st_s[b]
            outs[b] = _dot(scores[b], v) + _dot_nt(q_in[b], st.astype(BF16))
            st_s[b] = decay[b] * st + _dot_tn(v, k_out[b])
        for b in grp:
            o = outs[b]
            gate = gt_ref[rows[b], :]
            o = o * lax.rsqrt(jnp.mean(o * o, axis=-1, keepdims=True) + EPS) * gn
            ob_s[rows[b], :] = o * (gate * jax.nn.sigmoid(gate))
    o_ref[...] = ob_s[...].astype(BF16)


def _hg_rec(proj, g_norm, nb, heads, dk):
    n = proj.shape[0]
    seq = n // nb
    ch = _tile(seq, HG_CHUNK, 8)
    rows = ch * nb
    mk = _hg_masks(ch)
    blk = lambda off: pl.BlockSpec((rows, dk), lambda h, c: (c, off + h))
    return pl.pallas_call(
        functools.partial(_hg_rec_kernel, nb=nb, ch=ch),
        grid=(heads, seq // ch),
        in_specs=[blk(0), blk(heads), blk(2 * heads), blk(3 * heads),
                  pl.BlockSpec(mk.shape, lambda h, c: (0, 0, 0)),
                  pl.BlockSpec((1, dk), lambda h, c: (0, 0))],
        out_specs=pl.BlockSpec((rows, dk), lambda h, c: (c, h)),
        out_shape=jax.ShapeDtypeStruct((n, heads * dk), BF16),
        scratch_shapes=[pltpu.VMEM((nb, dk, dk), F32), pltpu.VMEM((rows, dk), F32)],
        compiler_params=_params("parallel", "arbitrary"),
        name="hg_rec",
    )(proj, proj, proj, proj, jnp.asarray(mk, F32), g_norm)


def _router_kernel(x_ref, g_ref, sc_ref, sh_ref, wh_ref, wl_ref, br_ref, tri_ref,
                   hp_ref, idx_ref, gate_ref, rank_ref, cnt_ref, cnt_s):
    @pl.when(pl.program_id(0) == 0)
    def _():
        cnt_s[...] = jnp.zeros(cnt_s.shape, F32)

    h = _norm_mod(x_ref[...], g_ref[...], sc_ref[...], sh_ref[...])
    half = h.shape[1] // 2
    hp_ref[...] = _pack_pair(h[:, :half], h[:, half:])
    hh, hl = _split(h)
    wh = wh_ref[...]
    vals = _dot(hh, wh) + _dot(hl, wh) + _dot(hh, wl_ref[...]) + br_ref[...]
    tm, n_exp = vals.shape
    lane = lax.broadcasted_iota(I32, (tm, n_exp), 1)
    tops, idxs, hots = [], [], []
    for _ in range(TOP_K):
        m = jnp.max(vals, axis=-1, keepdims=True)
        i = jnp.min(jnp.where(vals == m, lane, n_exp), axis=-1, keepdims=True)
        tops.append(m)
        idxs.append(i)
        hots.append(lane == i)
        vals = jnp.where(hots[-1], -jnp.inf, vals)
    es = [jnp.exp(t - tops[0]) for t in tops]
    den = es[0] + es[1] + es[2] + es[3]

    sel = [jnp.where(hm, 1.0, 0.0) for hm in hots]
    multi = sel[0] + sel[1] + sel[2] + sel[3]
    base = cnt_s[...] + _dot(tri_ref[...], multi.astype(BF16))
    cnt_s[...] = cnt_s[...] + jnp.sum(multi, axis=0, keepdims=True)
    cnt_ref[...] = cnt_s[...]

    lane_o = lax.broadcasted_iota(I32, idx_ref.shape, 1)
    io = jnp.zeros(idx_ref.shape, I32)
    ro = jnp.zeros(idx_ref.shape, I32)
    go = jnp.zeros(gate_ref.shape, F32)
    for kk in range(TOP_K):
        rk = jnp.sum(sel[kk] * base, axis=-1, keepdims=True).astype(I32)
        io = jnp.where(lane_o == kk, idxs[kk], io)
        ro = jnp.where(lane_o == kk, rk, ro)
        go = jnp.where(lane_o == kk, es[kk] / den, go)
    idx_ref[...] = io
    rank_ref[...] = ro
    gate_ref[...] = go


def _router(x, g, sc, sh, w_r, b_r):
    n, d = x.shape
    nb = sc.shape[0]
    n_exp = w_r.shape[1]
    tm = _tile(n, 512, 8 * nb)
    wh, wl = _split(w_r.astype(F32))
    tri = jnp.asarray(np.tril(np.ones((tm, tm), np.float32), -1), BF16)
    row = lambda w: pl.BlockSpec((tm, w), lambda i: (i, 0))
    full = lambda a: pl.BlockSpec(a.shape, lambda i: (0, 0))
    b2 = b_r.astype(F32).reshape(1, n_exp)
    return pl.pallas_call(
        _router_kernel,
        grid=(n // tm,),
        in_specs=[row(d), full(g), full(sc), full(sh), full(wh), full(wl), full(b2), full(tri)],
        out_specs=[row(d // 2), row(LANES), row(LANES), row(LANES),
                   pl.BlockSpec((1, n_exp), lambda i: (0, 0))],
        out_shape=[jax.ShapeDtypeStruct((n, d // 2), I32),
                   jax.ShapeDtypeStruct((n, LANES), I32),
                   jax.ShapeDtypeStruct((n, LANES), F32),
                   jax.ShapeDtypeStruct((n, LANES), I32),
                   jax.ShapeDtypeStruct((1, n_exp), F32)],
        scratch_shapes=[pltpu.VMEM((1, n_exp), F32)],
        compiler_params=_params("arbitrary"),
        name="router",
    )(x, g, sc, sh, wh, wl, b2, tri)


def _moe_plan(top_idx, rank, counts_f, tmb):
    n = top_idx.shape[0]
    n_exp = counts_f.shape[-1]
    counts = counts_f.reshape(n_exp).astype(I32)
    nblk = (counts + tmb - 1) // tmb
    blk_end = jnp.cumsum(nblk)
    blk_start = blk_end - nblk
    n_used = blk_end[-1]
    n_blocks = n * TOP_K // tmb + n_exp
    blk = jnp.arange(n_blocks, dtype=I32)
    be = jnp.minimum(jnp.searchsorted(blk_end, blk, side='right'), n_exp - 1).astype(I32)
    be = jnp.where(blk < n_used, be, be[n_used - 1])
    base = blk_start * tmb
    hot = top_idx[:, :, None] == jnp.arange(n_exp, dtype=I32)
    dest = jnp.sum(jnp.where(hot, base, 0), axis=-1) + rank
    pad_start = base + counts
    pad_len = blk_end * tmb - pad_start
    return dict(be=be, n_used=n_used.reshape(1).astype(I32), dest=dest.astype(I32),
                pad_start=pad_start.astype(I32), pad_len=pad_len.astype(I32),
                n_blocks=n_blocks, tmb=tmb)


def _dispatch_kernel(ps_ref, pn_ref, nu_ref, dest_ref, hp_ref, xs_hbm, zero_s, sem, *, tmb):
    tm = hp_ref.shape[0]

    def row(r, carry):
        for kk in range(TOP_K):
            t = dest_ref[0, 0, TOP_K * r + kk]
            pltpu.make_async_copy(hp_ref.at[pl.ds(r, 1)], xs_hbm.at[pl.ds(t, 1)], sem).start()
        return carry

    lax.fori_loop(0, tm, row, 0, unroll=4)
    for _ in range(TOP_K):
        pltpu.make_async_copy(hp_ref, xs_hbm.at[pl.ds(0, tm)], sem).wait()

    @pl.when(pl.program_id(0) == pl.num_programs(0) - 1)
    def _():
        zero_s[...] = jnp.zeros(zero_s.shape, I32)
        zrow = zero_s.at[pl.ds(0, 1)]

        def expert(e, carry):
            p0 = ps_ref[e]
            cnt = pn_ref[e]

            def fill(r, c2):
                pltpu.make_async_copy(zrow, xs_hbm.at[pl.ds(p0 + r, 1)], sem).start()
                return c2

            def drain(r, c2):
                pltpu.make_async_copy(zrow, xs_hbm.at[pl.ds(p0, 1)], sem).wait()
                return c2

            lax.fori_loop(0, cnt, fill, 0)
            lax.fori_loop(0, cnt, drain, 0)
            return carry

        lax.fori_loop(0, ps_ref.shape[0], expert, 0)

        def block(bk, carry):
            cp = pltpu.make_async_copy(zero_s, xs_hbm.at[pl.ds(pl.multiple_of(bk * tmb, tmb), tmb)], sem)
            cp.start()
            cp.wait()
            return carry

        lax.fori_loop(nu_ref[0], xs_hbm.shape[0] // tmb, block, 0)


def _dispatch(hp, plan):
    n, half = hp.shape
    tmb = plan['tmb']
    tm = _tile(n, 1024, SUBLANES)
    dest = plan['dest'].reshape(n // tm, 1, TOP_K * tm)
    grid_spec = pltpu.PrefetchScalarGridSpec(
        num_scalar_prefetch=3,
        grid=(n // tm,),
        in_specs=[pl.BlockSpec((1, 1, TOP_K * tm), lambda i, *_: (i, 0, 0), memory_space=pltpu.SMEM),
                  pl.BlockSpec((tm, half), lambda i, *_: (i, 0))],
        out_specs=pl.BlockSpec(memory_space=pl.ANY),
        scratch_shapes=[pltpu.VMEM((tmb, half), I32), pltpu.SemaphoreType.DMA(())],
    )
    return pl.pallas_call(
        functools.partial(_dispatch_kernel, tmb=tmb),
        grid_spec=grid_spec,
        out_shape=jax.ShapeDtypeStruct((plan['n_blocks'] * tmb, half), I32),
        compiler_params=_params("arbitrary"),
        name="moe_dispatch",
    )(plan['pad_start'], plan['pad_len'], plan['n_used'], dest, hp)


def _moe_kernel(be_ref, nu_ref, x_ref, wgu_ref, bgu_ref, wdn_ref, bdn_ref, y_ref, wgu_s, wdn_s, *, ff):
    i = pl.program_id(0)
    used = i < nu_ref[0]

    @pl.when(used)
    def _():
        e_new = jnp.logical_or(i == 0, be_ref[i] != be_ref[jnp.maximum(i - 1, 0)])

        @pl.when(e_new)
        def _():
            wgu_s[...] = wgu_ref[0, 0].astype(BF16)
            wdn_s[...] = wdn_ref[0, 0].astype(BF16)

        half = x_ref.shape[1]
        x_lo, x_hi = _unpack_pair(x_ref[...])
        gu = (_dot(x_lo.astype(BF16), wgu_s[:half, :]) + _dot(x_hi.astype(BF16), wgu_s[half:, :])
              + bgu_ref[0, 0])
        gt = jnp.minimum(gu[:, :ff], SWIGLU_LIMIT)
        up = jnp.clip(gu[:, ff:], -SWIGLU_LIMIT, SWIGLU_LIMIT)
        act = ((up + 1.0) * gt * jax.nn.sigmoid(SWIGLU_ALPHA * gt)).astype(BF16)
        bdn = bdn_ref[0, 0]
        y_lo = _dot(act, wdn_s[:, :half]) + bdn[:, :half]
        y_hi = _dot(act, wdn_s[:, half:]) + bdn[:, half:]
        y_ref[...] = _pack_pair(y_lo, y_hi)

    @pl.when(jnp.logical_not(used))
    def _():
        y_ref[...] = jnp.zeros(y_ref.shape, I32)


def _moe_experts(xs, plan, w_gu, b_gu, w_dn, b_dn, layer):
    depth, n_exp, d, ff2 = w_gu.shape
    ff = ff2 // 2
    tmb = plan['tmb']
    n_blocks = plan['n_blocks']
    half = xs.shape[1]
    wmap = lambda i, be, nu: (layer, be[i], 0, 0)
    grid_spec = pltpu.PrefetchScalarGridSpec(
        num_scalar_prefetch=2,
        grid=(n_blocks,),
        in_specs=[pl.BlockSpec((tmb, half), lambda i, be, nu: (i, 0)),
                  pl.BlockSpec((1, 1, d, ff2), wmap),
                  pl.BlockSpec((1, 1, 1, ff2), wmap),
                  pl.BlockSpec((1, 1, ff, d), wmap),
                  pl.BlockSpec((1, 1, 1, d), wmap)],
        out_specs=pl.BlockSpec((tmb, half), lambda i, be, nu: (i, 0)),
        scratch_shapes=[pltpu.VMEM((d, ff2), BF16), pltpu.VMEM((ff, d), BF16)],
    )
    return pl.pallas_call(
        functools.partial(_moe_kernel, ff=ff),
        grid_spec=grid_spec,
        out_shape=jax.ShapeDtypeStruct((n_blocks * tmb, half), I32),
        compiler_params=_params("arbitrary"),
        name="moe_experts",
    )(plan['be'], plan['n_used'], xs, w_gu, b_gu.reshape(depth, n_exp, 1, ff2),
      w_dn, b_dn.reshape(depth, n_exp, 1, d))


def _combine_kernel(dest_ref, x_ref, gate_ref, g_ref, ys_hbm, o_ref, ybuf, sem):
    tm = x_ref.shape[0]
    half = ybuf.shape[-1]

    def row(r, carry):
        for kk in range(TOP_K):
            t = dest_ref[0, 0, TOP_K * r + kk]
            pltpu.make_async_copy(ys_hbm.at[pl.ds(t, 1)], ybuf.at[kk, pl.ds(r, 1)], sem).start()
        return carry

    lax.fori_loop(0, tm, row, 0, unroll=4)
    for kk in range(TOP_K):
        pltpu.make_async_copy(ys_hbm.at[pl.ds(0, tm)], ybuf.at[kk], sem).wait()

    gt = gate_ref[...]
    lo = hi = None
    for kk in range(TOP_K):
        w = gt[:, kk:kk + 1]
        y_lo, y_hi = _unpack_pair(ybuf[kk])
        lo = w * y_lo if lo is None else lo + w * y_lo
        hi = w * y_hi if hi is None else hi + w * y_hi
    g = g_ref[...]
    o_ref[:, :half] = x_ref[:, :half] + _gate_rows(lo, g[:, :half])
    o_ref[:, half:] = x_ref[:, half:] + _gate_rows(hi, g[:, half:])


def _combine(x, ys, gates, g2, plan):
    n, d = x.shape
    nb = g2.shape[0]
    half = ys.shape[1]
    tm = _tile(n, 512, 8 * nb)
    dest = plan['dest'].reshape(n // tm, 1, TOP_K * tm)
    return pl.pallas_call(
        _combine_kernel,
        grid=(n // tm,),
        in_specs=[pl.BlockSpec((1, 1, TOP_K * tm), lambda i: (i, 0, 0), memory_space=pltpu.SMEM),
                  pl.BlockSpec((tm, d), lambda i: (i, 0)),
                  pl.BlockSpec((tm, LANES), lambda i: (i, 0)),
                  pl.BlockSpec((nb, d), lambda i: (0, 0)),
                  pl.BlockSpec(memory_space=pl.ANY)],
        out_specs=pl.BlockSpec((tm, d), lambda i: (i, 0)),
        out_shape=jax.ShapeDtypeStruct((n, d), F32),
        scratch_shapes=[pltpu.VMEM((TOP_K, tm, half), I32), pltpu.SemaphoreType.DMA(())],
        compiler_params=_params("arbitrary"),
        name="moe_combine",
    )(dest, x, gates, g2, ys)


def _final_kernel(x_ref, g_ref, o_ref):
    x = x_ref[...]
    ms = jnp.mean(x * x, axis=-1, keepdims=True)
    o_ref[...] = (x * lax.rsqrt(ms + EPS)) * g_ref[...]


def _final_norm(x, g):
    n, d = x.shape
    tm = _tile(n, 512, 8)
    return pl.pallas_call(
        _final_kernel,
        grid=(n // tm,),
        in_specs=[pl.BlockSpec((tm, d), lambda i: (i, 0)), pl.BlockSpec((1, d), lambda i: (0, 0))],
        out_specs=pl.BlockSpec((tm, d), lambda i: (i, 0)),
        out_shape=jax.ShapeDtypeStruct((n, d), F32),
        compiler_params=_params("parallel"),
        name="final_norm",
    )(x, g)


def kernel(x, c, ada_w, ada_b, norm_mix, norm_ffn, norm_final, s5_lambda_re, s5_lambda_im, s5_log_dt,
           s5_b_re, s5_b_im, s5_c_re, s5_c_im, s5_d, s5_w_glu, hg_w_in, hg_lb_raw, hg_norm, hg_w_out,
           router_w, router_b, moe_w_gate_up, moe_b_gate_up, moe_w_down, moe_b_down):
    bsz, seq, d = x.shape
    depth = ada_w.shape[0]
    n = bsz * seq
    dv = hg_norm.shape[-1]
    heads = d // dv
    fdim = (hg_w_in.shape[-1] - 2 * d) // 2
    dk = fdim // heads
    assert dk == LANES and dv == LANES, "one HGRN2 head per 128-lane tile"
    tmb = _tile(n * TOP_K, MOE_ROWS, SUBLANES)

    lb_p = jax.nn.softmax(hg_lb_raw.astype(F32), axis=0)
    lb_all = jnp.cumsum(lb_p, axis=0) - lb_p[0]
    mod = _ada_mod(c.astype(F32), ada_w, ada_b)

    xs = x.astype(F32).transpose(1, 0, 2).reshape(n, d)
    row = lambda v: v.astype(F32).reshape(1, -1)
    for i in range(depth):
        sh1, sc1, g1, sh2, sc2, g2 = [mod[i, :, k * d:(k + 1) * d] for k in range(N_ADA)]
        j = i // 2
        if i % 2 == 0:
            bmat, cmat, lre, lim = _s5_prep(s5_lambda_re[j], s5_lambda_im[j], s5_log_dt[j],
                                            s5_b_re[j], s5_b_im[j], s5_c_re[j], s5_c_im[j])
            z = _s5_scan(xs, row(norm_mix[i]), sc1, sh1, bmat, cmat, lre, lim, row(s5_d[j]), bsz)
            xs = _mm_res(z, s5_w_glu[j].astype(BF16), xs, g1, glu=True)
        else:
            lb = lb_all[i]
            proj = _hg_proj(xs, row(norm_mix[i]), sc1, sh1, hg_w_in[j].astype(BF16),
                            row(jnp.log(lb)), row(jnp.log1p(-lb)), fdim)
            o = _hg_rec(proj, row(hg_norm[j]), bsz, heads, dk)
            xs = _mm_res(o, hg_w_out[j].astype(BF16), xs, g1, glu=False)
        hp, top_idx, gates, rank, counts = _router(xs, row(norm_ffn[i]), sc2, sh2, router_w[i], router_b[i])
        plan = _moe_plan(top_idx[:, :TOP_K], rank[:, :TOP_K], counts, tmb)
        xd = _dispatch(hp, plan)
        ys = _moe_experts(xd, plan, moe_w_gate_up, moe_b_gate_up.astype(F32),
                          moe_w_down, moe_b_down.astype(F32), i)
        xs = _combine(xs, ys, gates, g2, plan)
    out = _final_norm(xs, row(norm_final))
    return out.reshape(seq, bsz, d).transpose(1, 0, 2).astype(x.dtype)
```

```python
import functools

import numpy as np
import jax
import jax.numpy as jnp
from jax import lax
from jax.experimental import pallas as pl
from jax.experimental.pallas import tpu as pltpu

F32 = jnp.float32
BF16 = jnp.bfloat16
I32 = jnp.int32

EPS = 1e-6
N_ADA = 6
TOP_K = 4
S5_RE_MAX = -1e-4
SWIGLU_LIMIT = 7.0
SWIGLU_ALPHA = 1.702
GELU_C0 = 0.7978845608028654
GELU_C1 = 0.044715

LANES = 128
SUBLANES = 8
HG_CHUNK = 64
HG_GROUP = 4
S5_CHUNK = 128
S5_SUB = 4
NORM_ROWS = 32
MOE_SUB = 2
MOE_ROWS = 512
HI16 = -65536
VMEM_LIMIT = 56 * 1024 * 1024


def _params(*sem):
    return pltpu.CompilerParams(dimension_semantics=sem, vmem_limit_bytes=VMEM_LIMIT)


def _tile(n, pref, align):
    if n <= pref:
        return n
    t = (pref // align) * align
    while t > align and n % t:
        t -= align
    assert n % t == 0, (n, pref, align)
    return t


def _dot(a, b):
    return jnp.dot(a, b, preferred_element_type=F32)


def _dot_nt(a, b):
    return lax.dot_general(a, b, (((1,), (1,)), ((), ())), preferred_element_type=F32)


def _dot_tn(a, b):
    return lax.dot_general(a, b, (((0,), (0,)), ((), ())), preferred_element_type=F32)


def _split(a):
    hi = a.astype(BF16)
    lo = (a - hi.astype(F32)).astype(BF16)
    return hi, lo


def _dot3(a, w):
    ah, al = _split(a)
    wh, wl = _split(w)
    return _dot(ah, wh) + _dot(al, wh) + _dot(ah, wl)


def _norm_mod(x, g, sc, sh):
    tm, d = x.shape
    nb = sc.shape[0]
    ms = jnp.mean(x * x, axis=-1, keepdims=True)
    y = (x * lax.rsqrt(ms + EPS)) * g
    y3 = y.reshape(tm // nb, nb, d)
    return (y3 * (1.0 + sc)[None] + sh[None]).reshape(tm, d)


def _norm_mod_rows(x_ref, g_ref, sc_ref, sh_ref, emit):
    rows = x_ref.shape[0]
    ch = min(NORM_ROWS, rows)

    def body(i, carry):
        r0 = pl.multiple_of(i * ch, ch)
        emit(r0, _norm_mod(x_ref[pl.ds(r0, ch), :], g_ref[...], sc_ref[...], sh_ref[...]))
        return carry

    lax.fori_loop(0, rows // ch, body, 0, unroll=4 if (rows // ch) % 4 == 0 else 1)


def _gate_rows(p, g):
    tm, tn = p.shape
    nb = g.shape[0]
    return (p.reshape(tm // nb, nb, tn) * g[None]).reshape(tm, tn)


def _pack_pair(lo, hi):
    lo_b = lax.bitcast_convert_type(lo.astype(BF16).astype(F32), I32)
    hi_b = lax.bitcast_convert_type(hi.astype(BF16).astype(F32), I32)
    return lax.shift_right_logical(lo_b, 16) | (hi_b & HI16)


def _unpack_pair(p):
    lo = lax.bitcast_convert_type(lax.shift_left(p, 16), F32)
    hi = lax.bitcast_convert_type(p & HI16, F32)
    return lo, hi


def _ada_kernel(c_ref, w_ref, b_ref, o_ref):
    c = c_ref[...]
    o_ref[0] = _dot3(c * jax.nn.sigmoid(c), w_ref[0]) + b_ref[0]


def _ada_mod(c, ada_w, ada_b):
    depth, d, nd = ada_w.shape
    nb = c.shape[0]
    tn = _tile(nd, 768, LANES)
    return pl.pallas_call(
        _ada_kernel,
        grid=(depth, nd // tn),
        in_specs=[pl.BlockSpec((nb, d), lambda i, j: (0, 0)),
                  pl.BlockSpec((1, d, tn), lambda i, j: (i, 0, j)),
                  pl.BlockSpec((1, 1, tn), lambda i, j: (i, 0, j))],
        out_specs=pl.BlockSpec((1, nb, tn), lambda i, j: (i, 0, j)),
        out_shape=jax.ShapeDtypeStruct((depth, nb, nd), F32),
        compiler_params=_params("parallel", "parallel"),
        name="ada_mod",
    )(c, ada_w, ada_b.reshape(depth, 1, nd))


def _s5_kernel(x_ref, g_ref, sc_ref, sh_ref, bmat_ref, cmat_ref, lre_ref, lim_ref, d_ref,
               z_ref, h_s, st_s, bu_s, sa_s, *, tc, nb, hw):
    c = pl.program_id(0)
    j = pl.program_id(1)
    n_j = h_s.shape[0]

    @pl.when(j == 0)
    def _():
        def emit(r0, h):
            for jj in range(n_j):
                h_s[jj, pl.ds(r0, h.shape[0]), :] = h[:, jj * LANES:(jj + 1) * LANES]

        _norm_mod_rows(x_ref, g_ref, sc_ref, sh_ref, emit)

    @pl.when(c == 0)
    def _():
        st_s[j] = jnp.zeros(st_s.shape[1:], F32)

    n_sub = bu_s.shape[0]
    ts = tc // n_sub
    rs = ts * nb
    bmat = bmat_ref[0]
    cmat = cmat_ref[0]
    for k in range(n_sub):
        bu_s[k] = _dot(h_s[j, k * rs:(k + 1) * rs, :].astype(BF16), bmat)
    lre = jnp.broadcast_to(lre_ref[0], (nb, hw))
    lim = jnp.broadcast_to(lim_ref[0], (nb, hw))
    st = st_s[j]
    s_re, s_im = st[:, :hw], st[:, hw:]
    for k in range(n_sub):
        for t in range(ts):
            b = bu_s[k, t * nb:(t + 1) * nb, :]
            s_re, s_im = (lre * s_re - lim * s_im + b[:, :hw],
                          lre * s_im + lim * s_re + b[:, hw:])
            sa_s[k, t * nb:(t + 1) * nb, :hw] = s_re
            sa_s[k, t * nb:(t + 1) * nb, hw:] = s_im
        y = _dot(sa_s[k].astype(BF16), cmat) + d_ref[...] * h_s[j, k * rs:(k + 1) * rs, :]
        z = 0.5 * y * (1.0 + jnp.tanh(GELU_C0 * (y + GELU_C1 * (y * y * y))))
        z_ref[k * rs:(k + 1) * rs, :] = z.astype(BF16)
    st_s[j, :, :hw] = s_re
    st_s[j, :, hw:] = s_im


def _s5_prep(lam_re, lam_im, log_dt, b_re, b_im, c_re, c_im):
    n_g, n_p = lam_re.shape
    n_h = b_re.shape[-1]
    gpt = LANES // n_h
    n_j = n_g // gpt
    lam = lax.complex(jnp.minimum(lam_re.astype(F32), S5_RE_MAX), lam_im.astype(F32))
    dt = jnp.exp(log_dt.astype(F32))[:, None]
    lam_bar = jnp.exp(lam * dt)
    b_bar = ((lam_bar - 1.0) / lam)[:, :, None] * lax.complex(b_re.astype(F32), b_im.astype(F32))
    eye = jnp.eye(gpt, dtype=F32)
    bb = b_bar.reshape(n_j, gpt, n_p, n_h)
    bre = jnp.einsum('jkph,gk->jghkp', bb.real, eye)
    bim = jnp.einsum('jkph,gk->jghkp', bb.imag, eye)
    bmat = jnp.stack([bre, bim], axis=3).reshape(n_j, LANES, 2 * gpt * n_p)
    cre = jnp.einsum('jghp,kg->jkpgh', c_re.astype(F32).reshape(n_j, gpt, n_h, n_p), eye)
    cim = jnp.einsum('jghp,kg->jkpgh', c_im.astype(F32).reshape(n_j, gpt, n_h, n_p), eye)
    cmat = jnp.stack([cre, -cim], axis=1).reshape(n_j, 2 * gpt * n_p, LANES)
    lre = lam_bar.real.reshape(n_j, 1, gpt * n_p)
    lim = lam_bar.imag.reshape(n_j, 1, gpt * n_p)
    return bmat.astype(BF16), cmat.astype(BF16), lre, lim


def _s5_scan(x, g, sc, sh, bmat, cmat, lre, lim, d_skip, nb):
    n, d = x.shape
    n_j = d // LANES
    hw = lre.shape[-1]
    seq = n // nb
    tc = _tile(seq, S5_CHUNK, 8)
    rows = tc * nb
    n_sub = S5_SUB if tc % S5_SUB == 0 else 1
    kern = functools.partial(_s5_kernel, tc=tc, nb=nb, hw=hw)
    return pl.pallas_call(
        kern,
        grid=(seq // tc, n_j),
        in_specs=[pl.BlockSpec((rows, d), lambda c, j: (c, 0)),
                  pl.BlockSpec((1, d), lambda c, j: (0, 0)),
                  pl.BlockSpec((nb, d), lambda c, j: (0, 0)),
                  pl.BlockSpec((nb, d), lambda c, j: (0, 0)),
                  pl.BlockSpec((1, LANES, 2 * hw), lambda c, j: (j, 0, 0)),
                  pl.BlockSpec((1, 2 * hw, LANES), lambda c, j: (j, 0, 0)),
                  pl.BlockSpec((1, 1, hw), lambda c, j: (j, 0, 0)),
                  pl.BlockSpec((1, 1, hw), lambda c, j: (j, 0, 0)),
                  pl.BlockSpec((1, LANES), lambda c, j: (0, j))],
        out_specs=pl.BlockSpec((rows, LANES), lambda c, j: (c, j)),
        out_shape=jax.ShapeDtypeStruct((n, d), BF16),
        scratch_shapes=[pltpu.VMEM((n_j, rows, LANES), F32),
                        pltpu.VMEM((n_j, nb, 2 * hw), F32),
                        pltpu.VMEM((n_sub, rows // n_sub, 2 * hw), F32),
                        pltpu.VMEM((n_sub, rows // n_sub, 2 * hw), F32)],
        compiler_params=_params("arbitrary", "arbitrary"),
        name="s5_scan",
    )(x, g, sc, sh, bmat, cmat, lre, lim, d_skip)


def _mm_res_kernel(a_ref, *refs, glu):
    if glu:
        wa_ref, wb_ref, x_ref, g_ref, o_ref = refs
    else:
        wa_ref, x_ref, g_ref, o_ref = refs
    a = a_ref[...]
    p = _dot(a, wa_ref[...])
    if glu:
        p = p * jax.nn.sigmoid(_dot(a, wb_ref[...]))
    o_ref[...] = x_ref[...] + _gate_rows(p, g_ref[...])


def _mm_res(a, w, x, gate, glu):
    n, kdim = a.shape
    d = x.shape[1]
    nb = gate.shape[0]
    tm = _tile(n, 1024, 8 * nb)
    tn = _tile(d, 512, LANES)
    nj = d // tn
    w_specs = [pl.BlockSpec((kdim, tn), lambda i, j: (0, j))]
    w_args = [w]
    if glu:
        w_specs.append(pl.BlockSpec((kdim, tn), lambda i, j: (0, j + nj)))
        w_args.append(w)
    return pl.pallas_call(
        functools.partial(_mm_res_kernel, glu=glu),
        grid=(n // tm, nj),
        in_specs=[pl.BlockSpec((tm, kdim), lambda i, j: (i, 0))] + w_specs + [
            pl.BlockSpec((tm, tn), lambda i, j: (i, j)),
            pl.BlockSpec((nb, tn), lambda i, j: (0, j))],
        out_specs=pl.BlockSpec((tm, tn), lambda i, j: (i, j)),
        out_shape=jax.ShapeDtypeStruct((n, d), F32),
        compiler_params=_params("parallel", "arbitrary"),
        name="glu_res" if glu else "proj_res",
    )(a, *w_args, x, gate)


def _hg_proj_kernel(x_ref, g_ref, sc_ref, sh_ref, w_ref, la_ref, l1_ref, o_ref, h_s, *, f_lo, f_hi):
    j = pl.program_id(1)

    @pl.when(j == 0)
    def _():
        def emit(r0, h):
            h_s[pl.ds(r0, h.shape[0]), :] = h.astype(BF16)

        _norm_mod_rows(x_ref, g_ref, sc_ref, sh_ref, emit)

    p = _dot(h_s[...], w_ref[...])
    is_f = jnp.logical_and(j >= f_lo, j < f_hi)

    @pl.when(is_f)
    def _():
        ls = jnp.minimum(p, 0.0) - jnp.log(1.0 + jnp.exp(-jnp.abs(p)))
        bt = l1_ref[...] + ls
        la = la_ref[...]
        o_ref[...] = jnp.maximum(la, bt) + jnp.log(1.0 + jnp.exp(-jnp.abs(la - bt)))

    @pl.when(jnp.logical_not(is_f))
    def _():
        o_ref[...] = p


def _hg_proj(x, g, sc, sh, w_in, log_lb, log1m_lb, fdim):
    n, d = x.shape
    nb = sc.shape[0]
    cols = w_in.shape[1]
    tm = _tile(n, 1024, 8 * nb)
    tn = _tile(fdim, 512, LANES)
    f_lo = fdim // tn
    f_hi = 2 * fdim // tn

    def lb_map(i, j):
        return (0, jnp.clip(j - f_lo, 0, f_lo - 1))

    return pl.pallas_call(
        functools.partial(_hg_proj_kernel, f_lo=f_lo, f_hi=f_hi),
        grid=(n // tm, cols // tn),
        in_specs=[pl.BlockSpec((tm, d), lambda i, j: (i, 0)),
                  pl.BlockSpec((1, d), lambda i, j: (0, 0)),
                  pl.BlockSpec((nb, d), lambda i, j: (0, 0)),
                  pl.BlockSpec((nb, d), lambda i, j: (0, 0)),
                  pl.BlockSpec((d, tn), lambda i, j: (0, j)),
                  pl.BlockSpec((1, tn), lb_map),
                  pl.BlockSpec((1, tn), lb_map)],
        out_specs=pl.BlockSpec((tm, tn), lambda i, j: (i, j)),
        out_shape=jax.ShapeDtypeStruct((n, cols), F32),
        scratch_shapes=[pltpu.VMEM((tm, d), BF16)],
        compiler_params=_params("parallel", "arbitrary"),
        name="hg_proj",
    )(x, g, sc, sh, w_in, log_lb, log1m_lb)


def _hg_levels(ch):
    levels = []
    m = ch // 2
    while m >= 1:
        levels.append(m)
        m //= 2
    return levels


def _hg_masks(ch):
    levels = _hg_levels(ch)
    mk = np.zeros((len(levels) + 1, ch, ch), np.float32)
    for li, m in enumerate(levels):
        for t in range(ch):
            mid = (t // (2 * m)) * 2 * m + m - 1
            if t > mid:
                mk[li, t, mid - m + 1:mid + 1] = 1.0
    mk[-1] = np.eye(ch, dtype=np.float32)
    return mk


def _cumsum_rows(x, t_idx):
    sh = 1
    while sh < x.shape[0]:
        x = x + jnp.where(t_idx >= sh, pltpu.roll(x, sh, 0), 0.0)
        sh *= 2
    return x


def _level_ref(bc, m, sub3):
    ch, dk = bc.shape
    if 2 * m >= SUBLANES:
        bp = bc.reshape(ch // (2 * m), 2 * m, dk)
        return jnp.broadcast_to(bp[:, m - 1:m, :], bp.shape).reshape(ch, dk)
    b3 = bc.reshape(ch // SUBLANES, SUBLANES, dk)
    r = None
    for start in range(0, SUBLANES, 2 * m):
        cand = jnp.broadcast_to(b3[:, start + m - 1:start + m, :], b3.shape)
        r = cand if r is None else jnp.where(sub3 >= start, cand, r)
    return r.reshape(ch, dk)


def _hg_rec_kernel(q_ref, lf_ref, v_ref, gt_ref, mask_ref, gn_ref, o_ref, st_s, ob_s, *, nb, ch):
    c = pl.program_id(1)

    @pl.when(c == 0)
    def _():
        st_s[...] = jnp.zeros(st_s.shape, F32)

    dk = q_ref.shape[-1]
    levels = _hg_levels(ch)
    n_lv = len(levels)
    gn = gn_ref[...]
    t_idx = lax.broadcasted_iota(I32, (ch, dk), 0)
    sub3 = lax.broadcasted_iota(I32, (ch // SUBLANES, SUBLANES, dk), 1)

    for b0 in range(0, nb, HG_GROUP):
        grp = range(b0, min(b0 + HG_GROUP, nb))
        rows = {b: pl.ds(b, ch, stride=nb) for b in grp}
        qk, qms, kms, q_in, k_out, decay = {}, {}, {}, {}, {}, {}
        for b in grp:
            q = q_ref[rows[b], :]
            lf = lf_ref[rows[b], :]
            k = 1.0 - jnp.exp(lf)
            bc = _cumsum_rows(lf, t_idx)
            qk[b] = (q.astype(BF16), k.astype(BF16))
            qms[b], kms[b] = [], []
            for m in levels:
                e = jnp.exp(-jnp.abs(bc - _level_ref(bc, m, sub3)))
                qms[b].append((q * e).astype(BF16))
                kms[b].append((k * e).astype(BF16))
            e_in = jnp.exp(bc)
            b_end = bc[ch - 1:ch, :]
            q_in[b] = (q * e_in).astype(BF16)
            k_out[b] = (k * jnp.exp(b_end - bc)).astype(BF16)
            decay[b] = jnp.exp(b_end)
        scores = {}
        for b in grp:
            s = mask_ref[n_lv] * _dot_nt(*qk[b])
            for li in range(n_lv):
                s = s + mask_ref[li] * _dot_nt(qms[b][li], kms[b][li])
            scores[b] = s.astype(BF16)
        outs = {}
        for b in grp:
            v = v_ref[rows[b], :].astype(BF16)
            st = st_s[b]
            outs[b] = _dot(scores[b], v) + _dot_nt(q_in[b], st.astype(BF16))
            st_s[b] = decay[b] * st + _dot_tn(v, k_out[b])
        for b in grp:
            o = outs[b]
            gate = gt_ref[rows[b], :]
            o = o * lax.rsqrt(jnp.mean(o * o, axis=-1, keepdims=True) + EPS) * gn
            ob_s[rows[b], :] = o * (gate * jax.nn.sigmoid(gate))
    o_ref[...] = ob_s[...].astype(BF16)


def _hg_rec(proj, g_norm, nb, heads, dk):
    n = proj.shape[0]
    seq = n // nb
    ch = _tile(seq, HG_CHUNK, 8)
    rows = ch * nb
    mk = _hg_masks(ch)
    blk = lambda off: pl.BlockSpec((rows, dk), lambda h, c: (c, off + h))
    return pl.pallas_call(
        functools.partial(_hg_rec_kernel, nb=nb, ch=ch),
        grid=(heads, seq // ch),
        in_specs=[blk(0), blk(heads), blk(2 * heads), blk(3 * heads),
                  pl.BlockSpec(mk.shape, lambda h, c: (0, 0, 0)),
                  pl.BlockSpec((1, dk), lambda h, c: (0, 0))],
        out_specs=pl.BlockSpec((rows, dk), lambda h, c: (c, h)),
        out_shape=jax.ShapeDtypeStruct((n, heads * dk), BF16),
        scratch_shapes=[pltpu.VMEM((nb, dk, dk), F32), pltpu.VMEM((rows, dk), F32)],
        compiler_params=_params("parallel", "arbitrary"),
        name="hg_rec",
    )(proj, proj, proj, proj, jnp.asarray(mk, F32), g_norm)


def _router_kernel(x_ref, g_ref, sc_ref, sh_ref, wh_ref, wl_ref, br_ref, tri_ref,
                   hp_ref, idx_ref, gate_ref, rank_ref, cnt_ref, cnt_s):
    @pl.when(pl.program_id(0) == 0)
    def _():
        cnt_s[...] = jnp.zeros(cnt_s.shape, F32)

    h = _norm_mod(x_ref[...], g_ref[...], sc_ref[...], sh_ref[...])
    half = h.shape[1] // 2
    hp_ref[...] = _pack_pair(h[:, :half], h[:, half:])
    hh, hl = _split(h)
    wh = wh_ref[...]
    vals = _dot(hh, wh) + _dot(hl, wh) + _dot(hh, wl_ref[...]) + br_ref[...]
    tm, n_exp = vals.shape
    lane = lax.broadcasted_iota(I32, (tm, n_exp), 1)
    tops, idxs, hots = [], [], []
    for _ in range(TOP_K):
        m = jnp.max(vals, axis=-1, keepdims=True)
        i = jnp.min(jnp.where(vals == m, lane, n_exp), axis=-1, keepdims=True)
        tops.append(m)
        idxs.append(i)
        hots.append(lane == i)
        vals = jnp.where(hots[-1], -jnp.inf, vals)
    es = [jnp.exp(t - tops[0]) for t in tops]
    den = es[0] + es[1] + es[2] + es[3]

    sel = [jnp.where(hm, 1.0, 0.0) for hm in hots]
    multi = sel[0] + sel[1] + sel[2] + sel[3]
    base = cnt_s[...] + _dot(tri_ref[...], multi.astype(BF16))
    cnt_s[...] = cnt_s[...] + jnp.sum(multi, axis=0, keepdims=True)
    cnt_ref[...] = cnt_s[...]

    lane_o = lax.broadcasted_iota(I32, idx_ref.shape, 1)
    io = jnp.zeros(idx_ref.shape, I32)
    ro = jnp.zeros(idx_ref.shape, I32)
    go = jnp.zeros(gate_ref.shape, F32)
    for kk in range(TOP_K):
        rk = jnp.sum(sel[kk] * base, axis=-1, keepdims=True).astype(I32)
        io = jnp.where(lane_o == kk, idxs[kk], io)
        ro = jnp.where(lane_o == kk, rk, ro)
        go = jnp.where(lane_o == kk, es[kk] / den, go)
    idx_ref[...] = io
    rank_ref[...] = ro
    gate_ref[...] = go


def _router(x, g, sc, sh, w_r, b_r):
    n, d = x.shape
    nb = sc.shape[0]
    n_exp = w_r.shape[1]
    tm = _tile(n, 512, 8 * nb)
    wh, wl = _split(w_r.astype(F32))
    tri = jnp.asarray(np.tril(np.ones((tm, tm), np.float32), -1), BF16)
    row = lambda w: pl.BlockSpec((tm, w), lambda i: (i, 0))
    full = lambda a: pl.BlockSpec(a.shape, lambda i: (0, 0))
    b2 = b_r.astype(F32).reshape(1, n_exp)
    return pl.pallas_call(
        _router_kernel,
        grid=(n // tm,),
        in_specs=[row(d), full(g), full(sc), full(sh), full(wh), full(wl), full(b2), full(tri)],
        out_specs=[row(d // 2), row(LANES), row(LANES), row(LANES),
                   pl.BlockSpec((1, n_exp), lambda i: (0, 0))],
        out_shape=[jax.ShapeDtypeStruct((n, d // 2), I32),
                   jax.ShapeDtypeStruct((n, LANES), I32),
                   jax.ShapeDtypeStruct((n, LANES), F32),
                   jax.ShapeDtypeStruct((n, LANES), I32),
                   jax.ShapeDtypeStruct((1, n_exp), F32)],
        scratch_shapes=[pltpu.VMEM((1, n_exp), F32)],
        compiler_params=_params("arbitrary"),
        name="router",
    )(x, g, sc, sh, wh, wl, b2, tri)


def _moe_plan(top_idx, rank, counts_f, tmb):
    n = top_idx.shape[0]
    n_exp = counts_f.shape[-1]
    counts = counts_f.reshape(n_exp).astype(I32)
    nblk = (counts + tmb - 1) // tmb
    blk_end = jnp.cumsum(nblk)
    blk_start = blk_end - nblk
    n_used = blk_end[-1]
    n_blocks = n * TOP_K // tmb + n_exp
    blk = jnp.arange(n_blocks, dtype=I32)
    be = jnp.minimum(jnp.sum((blk[:, None] >= blk_end[None, :]).astype(I32), axis=1), n_exp - 1)
    last = jnp.sum(jnp.where(blk == n_used - 1, be, 0))
    be = jnp.where(blk < n_used, be, last).astype(I32)
    base = blk_start * tmb
    hot = top_idx[:, :, None] == jnp.arange(n_exp, dtype=I32)
    dest = jnp.sum(jnp.where(hot, base, 0), axis=-1) + rank
    pad_start = base + counts
    pad_len = blk_end * tmb - pad_start
    return dict(be=be, n_used=n_used.reshape(1).astype(I32), dest=dest.astype(I32),
                pad_start=pad_start.astype(I32), pad_len=pad_len.astype(I32),
                n_blocks=n_blocks, tmb=tmb)


def _dispatch_kernel(ps_ref, pn_ref, nu_ref, dest_ref, hp_ref, xs_hbm, zero_s, sem, *, tmb):
    tm = hp_ref.shape[0]

    def row(r, carry):
        for kk in range(TOP_K):
            t = dest_ref[0, 0, TOP_K * r + kk]
            pltpu.make_async_copy(hp_ref.at[pl.ds(r, 1)], xs_hbm.at[pl.ds(t, 1)], sem).start()
        return carry

    lax.fori_loop(0, tm, row, 0, unroll=4)
    for _ in range(TOP_K):
        pltpu.make_async_copy(hp_ref, xs_hbm.at[pl.ds(0, tm)], sem).wait()

    @pl.when(pl.program_id(0) == pl.num_programs(0) - 1)
    def _():
        zero_s[...] = jnp.zeros(zero_s.shape, I32)
        zrow = zero_s.at[pl.ds(0, 1)]

        def expert(e, carry):
            p0 = ps_ref[e]
            cnt = pn_ref[e]

            def fill(r, c2):
                pltpu.make_async_copy(zrow, xs_hbm.at[pl.ds(p0 + r, 1)], sem).start()
                return c2

            def drain(r, c2):
                pltpu.make_async_copy(zrow, xs_hbm.at[pl.ds(p0, 1)], sem).wait()
                return c2

            lax.fori_loop(0, cnt, fill, 0)
            lax.fori_loop(0, cnt, drain, 0)
            return carry

        lax.fori_loop(0, ps_ref.shape[0], expert, 0)

        def block(bk, carry):
            cp = pltpu.make_async_copy(zero_s, xs_hbm.at[pl.ds(pl.multiple_of(bk * tmb, tmb), tmb)], sem)
            cp.start()
            cp.wait()
            return carry

        lax.fori_loop(nu_ref[0], xs_hbm.shape[0] // tmb, block, 0)


def _dispatch(hp, plan):
    n, half = hp.shape
    tmb = plan['tmb']
    tm = _tile(n, 1024, SUBLANES)
    dest = plan['dest'].reshape(n // tm, 1, TOP_K * tm)
    grid_spec = pltpu.PrefetchScalarGridSpec(
        num_scalar_prefetch=3,
        grid=(n // tm,),
        in_specs=[pl.BlockSpec((1, 1, TOP_K * tm), lambda i, *_: (i, 0, 0), memory_space=pltpu.SMEM),
                  pl.BlockSpec((tm, half), lambda i, *_: (i, 0))],
        out_specs=pl.BlockSpec(memory_space=pl.ANY),
        scratch_shapes=[pltpu.VMEM((tmb, half), I32), pltpu.SemaphoreType.DMA(())],
    )
    return pl.pallas_call(
        functools.partial(_dispatch_kernel, tmb=tmb),
        grid_spec=grid_spec,
        out_shape=jax.ShapeDtypeStruct((plan['n_blocks'] * tmb, half), I32),
        compiler_params=_params("arbitrary"),
        name="moe_dispatch",
    )(plan['pad_start'], plan['pad_len'], plan['n_used'], dest, hp)


def _moe_kernel(be_ref, nu_ref, x_ref, wgu_ref, bgu_ref, wdn_ref, bdn_ref, y_ref, wgu_s, wdn_s, *, ff):
    i = pl.program_id(0)
    used = i < nu_ref[0]

    @pl.when(used)
    def _():
        e_new = jnp.logical_or(i == 0, be_ref[i] != be_ref[jnp.maximum(i - 1, 0)])

        @pl.when(e_new)
        def _():
            wgu_s[...] = wgu_ref[0, 0].astype(BF16)
            wdn_s[...] = wdn_ref[0, 0].astype(BF16)

        tmb, half = x_ref.shape
        n_sub = MOE_SUB if tmb % (MOE_SUB * 2 * SUBLANES) == 0 else 1
        rs = tmb // n_sub
        bgu = bgu_ref[0, 0]
        bdn = bdn_ref[0, 0]
        gus = []
        for s in range(n_sub):
            x_lo, x_hi = _unpack_pair(x_ref[s * rs:(s + 1) * rs, :])
            gus.append(_dot(x_lo.astype(BF16), wgu_s[:half, :]) + _dot(x_hi.astype(BF16), wgu_s[half:, :]) + bgu)
        for s in range(n_sub):
            gt = jnp.minimum(gus[s][:, :ff], SWIGLU_LIMIT)
            up = jnp.clip(gus[s][:, ff:], -SWIGLU_LIMIT, SWIGLU_LIMIT)
            act = ((up + 1.0) * gt * jax.nn.sigmoid(SWIGLU_ALPHA * gt)).astype(BF16)
            y_lo = _dot(act, wdn_s[:, :half]) + bdn[:, :half]
            y_hi = _dot(act, wdn_s[:, half:]) + bdn[:, half:]
            y_ref[s * rs:(s + 1) * rs, :] = _pack_pair(y_lo, y_hi)

    @pl.when(jnp.logical_not(used))
    def _():
        y_ref[...] = jnp.zeros(y_ref.shape, I32)


def _moe_experts(xs, plan, w_gu, b_gu, w_dn, b_dn, layer):
    depth, n_exp, d, ff2 = w_gu.shape
    ff = ff2 // 2
    tmb = plan['tmb']
    n_blocks = plan['n_blocks']
    half = xs.shape[1]
    wmap = lambda i, be, nu: (layer, be[i], 0, 0)
    grid_spec = pltpu.PrefetchScalarGridSpec(
        num_scalar_prefetch=2,
        grid=(n_blocks,),
        in_specs=[pl.BlockSpec((tmb, half), lambda i, be, nu: (i, 0)),
                  pl.BlockSpec((1, 1, d, ff2), wmap),
                  pl.BlockSpec((1, 1, 1, ff2), wmap),
                  pl.BlockSpec((1, 1, ff, d), wmap),
                  pl.BlockSpec((1, 1, 1, d), wmap)],
        out_specs=pl.BlockSpec((tmb, half), lambda i, be, nu: (i, 0)),
        scratch_shapes=[pltpu.VMEM((d, ff2), BF16), pltpu.VMEM((ff, d), BF16)],
    )
    return pl.pallas_call(
        functools.partial(_moe_kernel, ff=ff),
        grid_spec=grid_spec,
        out_shape=jax.ShapeDtypeStruct((n_blocks * tmb, half), I32),
        compiler_params=_params("arbitrary"),
        name="moe_experts",
    )(plan['be'], plan['n_used'], xs, w_gu, b_gu.reshape(depth, n_exp, 1, ff2),
      w_dn, b_dn.reshape(depth, n_exp, 1, d))


def _combine_kernel(dest_ref, dnext_ref, x_ref, gate_ref, g_ref, ys_hbm, o_ref, ybuf, sems):
    i = pl.program_id(0)
    tm = x_ref.shape[0]
    half = ybuf.shape[-1]
    slot = i % 2

    def gather(d_ref, s):
        def row(r, carry):
            for kk in range(TOP_K):
                t = d_ref[0, 0, TOP_K * r + kk]
                pltpu.make_async_copy(ys_hbm.at[pl.ds(t, 1)], ybuf.at[s, kk, pl.ds(r, 1)], sems.at[s]).start()
            return carry

        lax.fori_loop(0, tm, row, 0, unroll=4)

    @pl.when(i == 0)
    def _():
        gather(dest_ref, 0)

    @pl.when(i + 1 < pl.num_programs(0))
    def _():
        gather(dnext_ref, 1 - slot)

    for kk in range(TOP_K):
        pltpu.make_async_copy(ys_hbm.at[pl.ds(0, tm)], ybuf.at[slot, kk], sems.at[slot]).wait()

    gt = gate_ref[...]
    lo = hi = None
    for kk in range(TOP_K):
        w = gt[:, kk:kk + 1]
        y_lo, y_hi = _unpack_pair(ybuf[slot, kk])
        lo = w * y_lo if lo is None else lo + w * y_lo
        hi = w * y_hi if hi is None else hi + w * y_hi
    g = g_ref[...]
    o_ref[:, :half] = x_ref[:, :half] + _gate_rows(lo, g[:, :half])
    o_ref[:, half:] = x_ref[:, half:] + _gate_rows(hi, g[:, half:])


def _combine(x, ys, gates, g2, plan):
    n, d = x.shape
    nb = g2.shape[0]
    half = ys.shape[1]
    tm = _tile(n, 512, 8 * nb)
    nt = n // tm
    dest = plan['dest'].reshape(nt, 1, TOP_K * tm)
    return pl.pallas_call(
        _combine_kernel,
        grid=(nt,),
        in_specs=[pl.BlockSpec((1, 1, TOP_K * tm), lambda i: (i, 0, 0), memory_space=pltpu.SMEM),
                  pl.BlockSpec((1, 1, TOP_K * tm), lambda i: (jnp.minimum(i + 1, nt - 1), 0, 0),
                               memory_space=pltpu.SMEM),
                  pl.BlockSpec((tm, d), lambda i: (i, 0)),
                  pl.BlockSpec((tm, LANES), lambda i: (i, 0)),
                  pl.BlockSpec((nb, d), lambda i: (0, 0)),
                  pl.BlockSpec(memory_space=pl.ANY)],
        out_specs=pl.BlockSpec((tm, d), lambda i: (i, 0)),
        out_shape=jax.ShapeDtypeStruct((n, d), F32),
        scratch_shapes=[pltpu.VMEM((2, TOP_K, tm, half), I32), pltpu.SemaphoreType.DMA((2,))],
        compiler_params=_params("arbitrary"),
        name="moe_combine",
    )(dest, dest, x, gates, g2, ys)


def _final_kernel(x_ref, g_ref, o_ref):
    x = x_ref[...]
    ms = jnp.mean(x * x, axis=-1, keepdims=True)
    o_ref[...] = (x * lax.rsqrt(ms + EPS)) * g_ref[...]


def _final_norm(x, g):
    n, d = x.shape
    tm = _tile(n, 512, 8)
    return pl.pallas_call(
        _final_kernel,
        grid=(n // tm,),
        in_specs=[pl.BlockSpec((tm, d), lambda i: (i, 0)), pl.BlockSpec((1, d), lambda i: (0, 0))],
        out_specs=pl.BlockSpec((tm, d), lambda i: (i, 0)),
        out_shape=jax.ShapeDtypeStruct((n, d), F32),
        compiler_params=_params("parallel"),
        name="final_norm",
    )(x, g)


def kernel(x, c, ada_w, ada_b, norm_mix, norm_ffn, norm_final, s5_lambda_re, s5_lambda_im, s5_log_dt,
           s5_b_re, s5_b_im, s5_c_re, s5_c_im, s5_d, s5_w_glu, hg_w_in, hg_lb_raw, hg_norm, hg_w_out,
           router_w, router_b, moe_w_gate_up, moe_b_gate_up, moe_w_down, moe_b_down):
    bsz, seq, d = x.shape
    depth = ada_w.shape[0]
    n = bsz * seq
    dv = hg_norm.shape[-1]
    heads = d // dv
    fdim = (hg_w_in.shape[-1] - 2 * d) // 2
    dk = fdim // heads
    assert dk == LANES and dv == LANES, "one HGRN2 head per 128-lane tile"
    tmb = _tile(n * TOP_K, MOE_ROWS, SUBLANES)

    lb_p = jax.nn.softmax(hg_lb_raw.astype(F32), axis=0)
    lb_all = jnp.cumsum(lb_p, axis=0) - lb_p[0]
    mod = _ada_mod(c.astype(F32), ada_w, ada_b)

    xs = x.astype(F32).transpose(1, 0, 2).reshape(n, d)
    row = lambda v: v.astype(F32).reshape(1, -1)
    for i in range(depth):
        sh1, sc1, g1, sh2, sc2, g2 = [mod[i, :, k * d:(k + 1) * d] for k in range(N_ADA)]
        j = i // 2
        if i % 2 == 0:
            bmat, cmat, lre, lim = _s5_prep(s5_lambda_re[j], s5_lambda_im[j], s5_log_dt[j],
                                            s5_b_re[j], s5_b_im[j], s5_c_re[j], s5_c_im[j])
            z = _s5_scan(xs, row(norm_mix[i]), sc1, sh1, bmat, cmat, lre, lim, row(s5_d[j]), bsz)
            xs = _mm_res(z, s5_w_glu[j].astype(BF16), xs, g1, glu=True)
        else:
            lb = lb_all[i]
            proj = _hg_proj(xs, row(norm_mix[i]), sc1, sh1, hg_w_in[j].astype(BF16),
                            row(jnp.log(lb)), row(jnp.log1p(-lb)), fdim)
            o = _hg_rec(proj, row(hg_norm[j]), bsz, heads, dk)
            xs = _mm_res(o, hg_w_out[j].astype(BF16), xs, g1, glu=False)
        hp, top_idx, gates, rank, counts = _router(xs, row(norm_ffn[i]), sc2, sh2, router_w[i], router_b[i])
        plan = _moe_plan(top_idx[:, :TOP_K], rank[:, :TOP_K], counts, tmb)
        xd = _dispatch(hp, plan)
        ys = _moe_experts(xd, plan, moe_w_gate_up, moe_b_gate_up.astype(F32),
                          moe_w_down, moe_b_down.astype(F32), i)
        xs = _combine(xs, ys, gates, g2, plan)
    out = _final_norm(xs, row(norm_final))
    return out.reshape(seq, bsz, d).transpose(1, 0, 2).astype(x.dtype)
```

```python
import functools

import numpy as np
import jax
import jax.numpy as jnp
from jax import lax
from jax.experimental import pallas as pl
from jax.experimental.pallas import tpu as pltpu

F32 = jnp.float32
BF16 = jnp.bfloat16
I32 = jnp.int32

EPS = 1e-6
N_ADA = 6
TOP_K = 4
S5_RE_MAX = -1e-4
SWIGLU_LIMIT = 7.0
SWIGLU_ALPHA = 1.702
GELU_C0 = 0.7978845608028654
GELU_C1 = 0.044715

LANES = 128
SUBLANES = 8
HG_CHUNK = 64
HG_GROUP = 4
S5_CHUNK = 128
S5_SUB = 4
NORM_ROWS = 32
MOE_ROWS = 512
HI16 = -65536
VMEM_LIMIT = 56 * 1024 * 1024


def _params(*sem):
    return pltpu.CompilerParams(dimension_semantics=sem, vmem_limit_bytes=VMEM_LIMIT)


def _tile(n, pref, align):
    if n <= pref:
        return n
    t = (pref // align) * align
    while t > align and n % t:
        t -= align
    assert n % t == 0, (n, pref, align)
    return t


def _dot(a, b):
    return jnp.dot(a, b, preferred_element_type=F32)


def _dot_nt(a, b):
    return lax.dot_general(a, b, (((1,), (1,)), ((), ())), preferred_element_type=F32)


def _dot_tn(a, b):
    return lax.dot_general(a, b, (((0,), (0,)), ((), ())), preferred_element_type=F32)


def _split(a):
    hi = a.astype(BF16)
    lo = (a - hi.astype(F32)).astype(BF16)
    return hi, lo


def _dot3(a, w):
    ah, al = _split(a)
    wh, wl = _split(w)
    return _dot(ah, wh) + _dot(al, wh) + _dot(ah, wl)


def _norm_mod(x, g, sc, sh):
    tm, d = x.shape
    nb = sc.shape[0]
    ms = jnp.mean(x * x, axis=-1, keepdims=True)
    y = (x * lax.rsqrt(ms + EPS)) * g
    y3 = y.reshape(tm // nb, nb, d)
    return (y3 * (1.0 + sc)[None] + sh[None]).reshape(tm, d)


def _norm_mod_rows(x_ref, g_ref, sc_ref, sh_ref, emit):
    rows = x_ref.shape[0]
    ch = min(NORM_ROWS, rows)

    def body(i, carry):
        r0 = pl.multiple_of(i * ch, ch)
        emit(r0, _norm_mod(x_ref[pl.ds(r0, ch), :], g_ref[...], sc_ref[...], sh_ref[...]))
        return carry

    lax.fori_loop(0, rows // ch, body, 0, unroll=4 if (rows // ch) % 4 == 0 else 1)


def _gate_rows(p, g):
    tm, tn = p.shape
    nb = g.shape[0]
    return (p.reshape(tm // nb, nb, tn) * g[None]).reshape(tm, tn)


def _pack_pair(lo, hi):
    lo_b = lax.bitcast_convert_type(lo.astype(BF16).astype(F32), I32)
    hi_b = lax.bitcast_convert_type(hi.astype(BF16).astype(F32), I32)
    return lax.shift_right_logical(lo_b, 16) | (hi_b & HI16)


def _unpack_pair(p):
    lo = lax.bitcast_convert_type(lax.shift_left(p, 16), F32)
    hi = lax.bitcast_convert_type(p & HI16, F32)
    return lo, hi


def _store_tiles(ref, row0, val):
    rows, width = val.shape
    pcs = width // LANES
    for jj in range(pcs):
        ref[pl.ds(row0 * pcs + jj, rows, stride=pcs), :] = val[:, jj * LANES:(jj + 1) * LANES]


def _load_tiles(ref, row0, rows, pcs):
    return jnp.concatenate([ref[pl.ds(row0 * pcs + jj, rows, stride=pcs), :] for jj in range(pcs)], axis=1)


def _ada_kernel(c_ref, w_ref, b_ref, o_ref):
    c = c_ref[...]
    o_ref[0] = _dot3(c * jax.nn.sigmoid(c), w_ref[0]) + b_ref[0]


def _ada_mod(c, ada_w, ada_b):
    depth, d, nd = ada_w.shape
    nb = c.shape[0]
    tn = _tile(nd, 768, LANES)
    return pl.pallas_call(
        _ada_kernel,
        grid=(depth, nd // tn),
        in_specs=[pl.BlockSpec((nb, d), lambda i, j: (0, 0)),
                  pl.BlockSpec((1, d, tn), lambda i, j: (i, 0, j)),
                  pl.BlockSpec((1, 1, tn), lambda i, j: (i, 0, j))],
        out_specs=pl.BlockSpec((1, nb, tn), lambda i, j: (i, 0, j)),
        out_shape=jax.ShapeDtypeStruct((depth, nb, nd), F32),
        compiler_params=_params("parallel", "parallel"),
        name="ada_mod",
    )(c, ada_w, ada_b.reshape(depth, 1, nd))


def _s5_kernel(x_ref, g_ref, sc_ref, sh_ref, bmat_ref, cmat_ref, lre_ref, lim_ref, d_ref,
               z_ref, h_s, st_s, bu_s, sa_s, *, tc, nb, hw):
    c = pl.program_id(0)
    j = pl.program_id(1)
    n_j = h_s.shape[0]

    @pl.when(j == 0)
    def _():
        def emit(r0, h):
            for jj in range(n_j):
                h_s[jj, pl.ds(r0, h.shape[0]), :] = h[:, jj * LANES:(jj + 1) * LANES]

        _norm_mod_rows(x_ref, g_ref, sc_ref, sh_ref, emit)

    @pl.when(c == 0)
    def _():
        st_s[j] = jnp.zeros(st_s.shape[1:], F32)

    n_sub = bu_s.shape[0]
    ts = tc // n_sub
    rs = ts * nb
    bmat = bmat_ref[0]
    cmat = cmat_ref[0]
    for k in range(n_sub):
        bu_s[k] = _dot(h_s[j, k * rs:(k + 1) * rs, :].astype(BF16), bmat)
    lre = jnp.broadcast_to(lre_ref[0], (nb, hw))
    lim = jnp.broadcast_to(lim_ref[0], (nb, hw))
    st = st_s[j]
    s_re, s_im = st[:, :hw], st[:, hw:]
    for k in range(n_sub):
        for t in range(ts):
            b = bu_s[k, t * nb:(t + 1) * nb, :]
            s_re, s_im = (lre * s_re - lim * s_im + b[:, :hw],
                          lre * s_im + lim * s_re + b[:, hw:])
            sa_s[k, t * nb:(t + 1) * nb, :hw] = s_re
            sa_s[k, t * nb:(t + 1) * nb, hw:] = s_im
        y = _dot(sa_s[k].astype(BF16), cmat) + d_ref[...] * h_s[j, k * rs:(k + 1) * rs, :]
        z = 0.5 * y * (1.0 + jnp.tanh(GELU_C0 * (y + GELU_C1 * (y * y * y))))
        z_ref[k * rs:(k + 1) * rs, :] = z.astype(BF16)
    st_s[j, :, :hw] = s_re
    st_s[j, :, hw:] = s_im


def _s5_prep(lam_re, lam_im, log_dt, b_re, b_im, c_re, c_im):
    n_g, n_p = lam_re.shape
    n_h = b_re.shape[-1]
    gpt = LANES // n_h
    n_j = n_g // gpt
    lam = lax.complex(jnp.minimum(lam_re.astype(F32), S5_RE_MAX), lam_im.astype(F32))
    dt = jnp.exp(log_dt.astype(F32))[:, None]
    lam_bar = jnp.exp(lam * dt)
    b_bar = ((lam_bar - 1.0) / lam)[:, :, None] * lax.complex(b_re.astype(F32), b_im.astype(F32))
    eye = jnp.eye(gpt, dtype=F32)
    bb = b_bar.reshape(n_j, gpt, n_p, n_h)
    bre = jnp.einsum('jkph,gk->jghkp', bb.real, eye)
    bim = jnp.einsum('jkph,gk->jghkp', bb.imag, eye)
    bmat = jnp.stack([bre, bim], axis=3).reshape(n_j, LANES, 2 * gpt * n_p)
    cre = jnp.einsum('jghp,kg->jkpgh', c_re.astype(F32).reshape(n_j, gpt, n_h, n_p), eye)
    cim = jnp.einsum('jghp,kg->jkpgh', c_im.astype(F32).reshape(n_j, gpt, n_h, n_p), eye)
    cmat = jnp.stack([cre, -cim], axis=1).reshape(n_j, 2 * gpt * n_p, LANES)
    lre = lam_bar.real.reshape(n_j, 1, gpt * n_p)
    lim = lam_bar.imag.reshape(n_j, 1, gpt * n_p)
    return bmat.astype(BF16), cmat.astype(BF16), lre, lim


def _s5_scan(x, g, sc, sh, bmat, cmat, lre, lim, d_skip, nb):
    n, d = x.shape
    n_j = d // LANES
    hw = lre.shape[-1]
    seq = n // nb
    tc = _tile(seq, S5_CHUNK, 8)
    rows = tc * nb
    n_sub = S5_SUB if tc % S5_SUB == 0 else 1
    kern = functools.partial(_s5_kernel, tc=tc, nb=nb, hw=hw)
    return pl.pallas_call(
        kern,
        grid=(seq // tc, n_j),
        in_specs=[pl.BlockSpec((rows, d), lambda c, j: (c, 0)),
                  pl.BlockSpec((1, d), lambda c, j: (0, 0)),
                  pl.BlockSpec((nb, d), lambda c, j: (0, 0)),
                  pl.BlockSpec((nb, d), lambda c, j: (0, 0)),
                  pl.BlockSpec((1, LANES, 2 * hw), lambda c, j: (j, 0, 0)),
                  pl.BlockSpec((1, 2 * hw, LANES), lambda c, j: (j, 0, 0)),
                  pl.BlockSpec((1, 1, hw), lambda c, j: (j, 0, 0)),
                  pl.BlockSpec((1, 1, hw), lambda c, j: (j, 0, 0)),
                  pl.BlockSpec((1, LANES), lambda c, j: (0, j))],
        out_specs=pl.BlockSpec((rows, LANES), lambda c, j: (c, j)),
        out_shape=jax.ShapeDtypeStruct((n, d), BF16),
        scratch_shapes=[pltpu.VMEM((n_j, rows, LANES), F32),
                        pltpu.VMEM((n_j, nb, 2 * hw), F32),
                        pltpu.VMEM((n_sub, rows // n_sub, 2 * hw), F32),
                        pltpu.VMEM((n_sub, rows // n_sub, 2 * hw), F32)],
        compiler_params=_params("arbitrary", "arbitrary"),
        name="s5_scan",
    )(x, g, sc, sh, bmat, cmat, lre, lim, d_skip)


def _mm_res_kernel(a_ref, *refs, glu):
    if glu:
        wa_ref, wb_ref, x_ref, g_ref, o_ref = refs
    else:
        wa_ref, x_ref, g_ref, o_ref = refs
    a = a_ref[...]
    p = _dot(a, wa_ref[...])
    if glu:
        p = p * jax.nn.sigmoid(_dot(a, wb_ref[...]))
    o_ref[...] = x_ref[...] + _gate_rows(p, g_ref[...])


def _mm_res(a, w, x, gate, glu):
    n, kdim = a.shape
    d = x.shape[1]
    nb = gate.shape[0]
    tm = _tile(n, 1024, 8 * nb)
    tn = _tile(d, 512, LANES)
    nj = d // tn
    w_specs = [pl.BlockSpec((kdim, tn), lambda i, j: (0, j))]
    w_args = [w]
    if glu:
        w_specs.append(pl.BlockSpec((kdim, tn), lambda i, j: (0, j + nj)))
        w_args.append(w)
    return pl.pallas_call(
        functools.partial(_mm_res_kernel, glu=glu),
        grid=(n // tm, nj),
        in_specs=[pl.BlockSpec((tm, kdim), lambda i, j: (i, 0))] + w_specs + [
            pl.BlockSpec((tm, tn), lambda i, j: (i, j)),
            pl.BlockSpec((nb, tn), lambda i, j: (0, j))],
        out_specs=pl.BlockSpec((tm, tn), lambda i, j: (i, j)),
        out_shape=jax.ShapeDtypeStruct((n, d), F32),
        compiler_params=_params("parallel", "arbitrary"),
        name="glu_res" if glu else "proj_res",
    )(a, *w_args, x, gate)


def _hg_proj_kernel(x_ref, g_ref, sc_ref, sh_ref, w_ref, la_ref, l1_ref, o_ref, h_s, *, f_lo, f_hi):
    j = pl.program_id(1)

    @pl.when(j == 0)
    def _():
        def emit(r0, h):
            h_s[pl.ds(r0, h.shape[0]), :] = h.astype(BF16)

        _norm_mod_rows(x_ref, g_ref, sc_ref, sh_ref, emit)

    p = _dot(h_s[...], w_ref[...])
    is_f = jnp.logical_and(j >= f_lo, j < f_hi)

    @pl.when(is_f)
    def _():
        ls = jnp.minimum(p, 0.0) - jnp.log(1.0 + jnp.exp(-jnp.abs(p)))
        bt = l1_ref[...] + ls
        la = la_ref[...]
        o_ref[...] = jnp.maximum(la, bt) + jnp.log(1.0 + jnp.exp(-jnp.abs(la - bt)))

    @pl.when(jnp.logical_not(is_f))
    def _():
        o_ref[...] = p


def _hg_proj(x, g, sc, sh, w_in, log_lb, log1m_lb, fdim):
    n, d = x.shape
    nb = sc.shape[0]
    cols = w_in.shape[1]
    tm = _tile(n, 1024, 8 * nb)
    tn = _tile(fdim, 512, LANES)
    f_lo = fdim // tn
    f_hi = 2 * fdim // tn

    def lb_map(i, j):
        return (0, jnp.clip(j - f_lo, 0, f_lo - 1))

    return pl.pallas_call(
        functools.partial(_hg_proj_kernel, f_lo=f_lo, f_hi=f_hi),
        grid=(n // tm, cols // tn),
        in_specs=[pl.BlockSpec((tm, d), lambda i, j: (i, 0)),
                  pl.BlockSpec((1, d), lambda i, j: (0, 0)),
                  pl.BlockSpec((nb, d), lambda i, j: (0, 0)),
                  pl.BlockSpec((nb, d), lambda i, j: (0, 0)),
                  pl.BlockSpec((d, tn), lambda i, j: (0, j)),
                  pl.BlockSpec((1, tn), lb_map),
                  pl.BlockSpec((1, tn), lb_map)],
        out_specs=pl.BlockSpec((tm, tn), lambda i, j: (i, j)),
        out_shape=jax.ShapeDtypeStruct((n, cols), F32),
        scratch_shapes=[pltpu.VMEM((tm, d), BF16)],
        compiler_params=_params("parallel", "arbitrary"),
        name="hg_proj",
    )(x, g, sc, sh, w_in, log_lb, log1m_lb)


def _hg_levels(ch):
    levels = []
    m = ch // 2
    while m >= 1:
        levels.append(m)
        m //= 2
    return levels


def _hg_masks(ch):
    levels = _hg_levels(ch)
    mk = np.zeros((len(levels) + 1, ch, ch), np.float32)
    for li, m in enumerate(levels):
        for t in range(ch):
            mid = (t // (2 * m)) * 2 * m + m - 1
            if t > mid:
                mk[li, t, mid - m + 1:mid + 1] = 1.0
    mk[-1] = np.eye(ch, dtype=np.float32)
    return mk


def _cumsum_rows(x, t_idx):
    sh = 1
    while sh < x.shape[0]:
        x = x + jnp.where(t_idx >= sh, pltpu.roll(x, sh, 0), 0.0)
        sh *= 2
    return x


def _level_ref(bc, m, sub3):
    ch, dk = bc.shape
    if 2 * m >= SUBLANES:
        bp = bc.reshape(ch // (2 * m), 2 * m, dk)
        return jnp.broadcast_to(bp[:, m - 1:m, :], bp.shape).reshape(ch, dk)
    b3 = bc.reshape(ch // SUBLANES, SUBLANES, dk)
    r = None
    for start in range(0, SUBLANES, 2 * m):
        cand = jnp.broadcast_to(b3[:, start + m - 1:start + m, :], b3.shape)
        r = cand if r is None else jnp.where(sub3 >= start, cand, r)
    return r.reshape(ch, dk)


def _hg_rec_kernel(q_ref, lf_ref, v_ref, gt_ref, mask_ref, gn_ref, o_ref, st_s, ob_s, *, nb, ch):
    c = pl.program_id(1)

    @pl.when(c == 0)
    def _():
        st_s[...] = jnp.zeros(st_s.shape, F32)

    dk = q_ref.shape[-1]
    levels = _hg_levels(ch)
    n_lv = len(levels)
    gn = gn_ref[...]
    t_idx = lax.broadcasted_iota(I32, (ch, dk), 0)
    sub3 = lax.broadcasted_iota(I32, (ch // SUBLANES, SUBLANES, dk), 1)

    for b0 in range(0, nb, HG_GROUP):
        grp = range(b0, min(b0 + HG_GROUP, nb))
        rows = {b: pl.ds(b, ch, stride=nb) for b in grp}
        qk, qms, kms, q_in, k_out, decay = {}, {}, {}, {}, {}, {}
        for b in grp:
            q = q_ref[rows[b], :]
            lf = lf_ref[rows[b], :]
            k = 1.0 - jnp.exp(lf)
            bc = _cumsum_rows(lf, t_idx)
            qk[b] = (q.astype(BF16), k.astype(BF16))
            qms[b], kms[b] = [], []
            for m in levels:
                e = jnp.exp(-jnp.abs(bc - _level_ref(bc, m, sub3)))
                qms[b].append((q * e).astype(BF16))
                kms[b].append((k * e).astype(BF16))
            e_in = jnp.exp(bc)
            b_end = bc[ch - 1:ch, :]
            q_in[b] = (q * e_in).astype(BF16)
            k_out[b] = (k * jnp.exp(b_end - bc)).astype(BF16)
            decay[b] = jnp.exp(b_end)
        scores = {}
        for b in grp:
            s = mask_ref[n_lv] * _dot_nt(*qk[b])
            for li in range(n_lv):
                s = s + mask_ref[li] * _dot_nt(qms[b][li], kms[b][li])
            scores[b] = s.astype(BF16)
        outs = {}
        for b in grp:
            v = v_ref[rows[b], :].astype(BF16)
            st = st_s[b]
            outs[b] = _dot(scores[b], v) + _dot_nt(q_in[b], st.astype(BF16))
            st_s[b] = decay[b] * st + _dot_tn(v, k_out[b])
        for b in grp:
            o = outs[b]
            gate = gt_ref[rows[b], :]
            o = o * lax.rsqrt(jnp.mean(o * o, axis=-1, keepdims=True) + EPS) * gn
            ob_s[rows[b], :] = o * (gate * jax.nn.sigmoid(gate))
    o_ref[...] = ob_s[...].astype(BF16)


def _hg_rec(proj, g_norm, nb, heads, dk):
    n = proj.shape[0]
    seq = n // nb
    ch = _tile(seq, HG_CHUNK, 8)
    rows = ch * nb
    mk = _hg_masks(ch)
    blk = lambda off: pl.BlockSpec((rows, dk), lambda h, c: (c, off + h))
    return pl.pallas_call(
        functools.partial(_hg_rec_kernel, nb=nb, ch=ch),
        grid=(heads, seq // ch),
        in_specs=[blk(0), blk(heads), blk(2 * heads), blk(3 * heads),
                  pl.BlockSpec(mk.shape, lambda h, c: (0, 0, 0)),
                  pl.BlockSpec((1, dk), lambda h, c: (0, 0))],
        out_specs=pl.BlockSpec((rows, dk), lambda h, c: (c, h)),
        out_shape=jax.ShapeDtypeStruct((n, heads * dk), BF16),
        scratch_shapes=[pltpu.VMEM((nb, dk, dk), F32), pltpu.VMEM((rows, dk), F32)],
        compiler_params=_params("parallel", "arbitrary"),
        name="hg_rec",
    )(proj, proj, proj, proj, jnp.asarray(mk, F32), g_norm)


def _router_kernel(x_ref, g_ref, sc_ref, sh_ref, wh_ref, wl_ref, br_ref, tri_ref,
                   hp_ref, idx_ref, gate_ref, rank_ref, cnt_ref, cnt_s):
    @pl.when(pl.program_id(0) == 0)
    def _():
        cnt_s[...] = jnp.zeros(cnt_s.shape, F32)

    h = _norm_mod(x_ref[...], g_ref[...], sc_ref[...], sh_ref[...])
    half = h.shape[1] // 2
    _store_tiles(hp_ref, 0, _pack_pair(h[:, :half], h[:, half:]))
    hh, hl = _split(h)
    wh = wh_ref[...]
    vals = _dot(hh, wh) + _dot(hl, wh) + _dot(hh, wl_ref[...]) + br_ref[...]
    tm, n_exp = vals.shape
    lane = lax.broadcasted_iota(I32, (tm, n_exp), 1)
    tops, idxs, hots = [], [], []
    for _ in range(TOP_K):
        m = jnp.max(vals, axis=-1, keepdims=True)
        i = jnp.min(jnp.where(vals == m, lane, n_exp), axis=-1, keepdims=True)
        tops.append(m)
        idxs.append(i)
        hots.append(lane == i)
        vals = jnp.where(hots[-1], -jnp.inf, vals)
    es = [jnp.exp(t - tops[0]) for t in tops]
    den = es[0] + es[1] + es[2] + es[3]

    sel = [jnp.where(hm, 1.0, 0.0) for hm in hots]
    multi = sel[0] + sel[1] + sel[2] + sel[3]
    base = cnt_s[...] + _dot(tri_ref[...], multi.astype(BF16))
    cnt_s[...] = cnt_s[...] + jnp.sum(multi, axis=0, keepdims=True)
    cnt_ref[...] = cnt_s[...]

    lane_o = lax.broadcasted_iota(I32, idx_ref.shape, 1)
    io = jnp.zeros(idx_ref.shape, I32)
    ro = jnp.zeros(idx_ref.shape, I32)
    go = jnp.zeros(gate_ref.shape, F32)
    for kk in range(TOP_K):
        rk = jnp.sum(sel[kk] * base, axis=-1, keepdims=True).astype(I32)
        io = jnp.where(lane_o == kk, idxs[kk], io)
        ro = jnp.where(lane_o == kk, rk, ro)
        go = jnp.where(lane_o == kk, es[kk] / den, go)
    idx_ref[...] = io
    rank_ref[...] = ro
    gate_ref[...] = go


def _router(x, g, sc, sh, w_r, b_r):
    n, d = x.shape
    nb = sc.shape[0]
    n_exp = w_r.shape[1]
    tm = _tile(n, 512, 8 * nb)
    pcs = d // 2 // LANES
    wh, wl = _split(w_r.astype(F32))
    tri =jnp.asarray(np.tril(np.ones((tm, tm), np.float32), -1), BF16)
    row = lambda w: pl.BlockSpec((tm, w), lambda i: (i, 0))
    full = lambda a: pl.BlockSpec(a.shape, lambda i: (0, 0))
    b2 = b_r.astype(F32).reshape(1, n_exp)
    return pl.pallas_call(
        _router_kernel,
        grid=(n // tm,),
        in_specs=[row(d), full(g), full(sc), full(sh), full(wh), full(wl), full(b2), full(tri)],
        out_specs=[pl.BlockSpec((tm * pcs, LANES), lambda i: (i, 0)), row(LANES), row(LANES), row(LANES),
                   pl.BlockSpec((1, n_exp), lambda i: (0, 0))],
        out_shape=[jax.ShapeDtypeStruct((n * pcs, LANES), I32),
                   jax.ShapeDtypeStruct((n, LANES), I32),
                   jax.ShapeDtypeStruct((n, LANES), F32),
                   jax.ShapeDtypeStruct((n, LANES), I32),
                   jax.ShapeDtypeStruct((1, n_exp), F32)],
        scratch_shapes=[pltpu.VMEM((1, n_exp), F32)],
        compiler_params=_params("arbitrary"),
        name="router",
    )(x, g, sc, sh, wh, wl, b2, tri)


def _moe_plan(top_idx, rank, counts_f, tmb):
    n = top_idx.shape[0]
    n_exp = counts_f.shape[-1]
    counts = counts_f.reshape(n_exp).astype(I32)
    nblk = (counts + tmb - 1) // tmb
    blk_end = jnp.cumsum(nblk)
    blk_start = blk_end - nblk
    n_used = blk_end[-1]
    n_blocks = n * TOP_K // tmb + n_exp
    blk = jnp.arange(n_blocks, dtype=I32)
    be = jnp.minimum(jnp.sum((blk[:, None] >= blk_end[None, :]).astype(I32), axis=1), n_exp - 1)
    last = jnp.sum(jnp.where(blk == n_used - 1, be, 0))
    be = jnp.where(blk < n_used, be, last).astype(I32)
    base = blk_start * tmb
    hot = top_idx[:, :, None] == jnp.arange(n_exp, dtype=I32)
    dest = jnp.sum(jnp.where(hot, base, 0), axis=-1) + rank
    pad_start = base + counts
    pad_len = blk_end * tmb - pad_start
    return dict(be=be, n_used=n_used.reshape(1).astype(I32), dest=dest.astype(I32),
                pad_start=pad_start.astype(I32), pad_len=pad_len.astype(I32),
                n_blocks=n_blocks, tmb=tmb)


def _tile_at(ref, row, pcs):
    return ref.at[pl.ds(pl.multiple_of(row * pcs, pcs), pcs)]


def _dispatch_kernel(ps_ref, pn_ref, nu_ref, dest_ref, hp_ref, xs_hbm, zero_s, sem, *, tmb, pcs):
    tm = hp_ref.shape[0] // pcs

    def row(r, carry):
        src = _tile_at(hp_ref, r, pcs)
        for kk in range(TOP_K):
            t = dest_ref[0, 0, TOP_K * r + kk]
            pltpu.make_async_copy(src, _tile_at(xs_hbm, t, pcs), sem).start(priority=kk % 2)
        return carry

    lax.fori_loop(0, tm, row, 0, unroll=4)
    for _ in range(TOP_K):
        pltpu.make_async_copy(hp_ref, xs_hbm.at[pl.ds(0, tm * pcs)], sem).wait()

    @pl.when(pl.program_id(0) == pl.num_programs(0) - 1)
    def _():
        zero_s[...] = jnp.zeros(zero_s.shape, I32)
        ztile = zero_s.at[pl.ds(0, pcs)]

        def expert(e, carry):
            p0 = ps_ref[e]
            cnt = pn_ref[e]

            def fill(r, c2):
                pltpu.make_async_copy(ztile, _tile_at(xs_hbm, p0 + r, pcs), sem).start()
                return c2

            def drain(r, c2):
                pltpu.make_async_copy(ztile, _tile_at(xs_hbm, p0, pcs), sem).wait()
                return c2

            lax.fori_loop(0, cnt, fill, 0)
            lax.fori_loop(0, cnt, drain, 0)
            return carry

        lax.fori_loop(0, ps_ref.shape[0], expert, 0)

        def block(bk, carry):
            cp = pltpu.make_async_copy(zero_s, _tile_at(xs_hbm, bk, tmb * pcs), sem)
            cp.start()
            cp.wait()
            return carry

        lax.fori_loop(nu_ref[0], xs_hbm.shape[0] // (tmb * pcs), block, 0)


def _dispatch(hp, plan, n):
    pcs = hp.shape[0] // n
    tmb = plan['tmb']
    tm = _tile(n, 1024, SUBLANES)
    dest = plan['dest'].reshape(n // tm, 1, TOP_K * tm)
    grid_spec = pltpu.PrefetchScalarGridSpec(
        num_scalar_prefetch=3,
        grid=(n // tm,),
        in_specs=[pl.BlockSpec((1, 1, TOP_K * tm), lambda i, *_: (i, 0, 0), memory_space=pltpu.SMEM),
                  pl.BlockSpec((tm * pcs, LANES), lambda i, *_: (i, 0))],
        out_specs=pl.BlockSpec(memory_space=pl.ANY),
        scratch_shapes=[pltpu.VMEM((tmb * pcs, LANES), I32), pltpu.SemaphoreType.DMA(())],
    )
    return pl.pallas_call(
        functools.partial(_dispatch_kernel, tmb=tmb, pcs=pcs),
        grid_spec=grid_spec,
        out_shape=jax.ShapeDtypeStruct((plan['n_blocks'] * tmb * pcs, LANES), I32),
        compiler_params=_params("arbitrary"),
        name="moe_dispatch",
    )(plan['pad_start'], plan['pad_len'], plan['n_used'], dest, hp)


def _moe_kernel(be_ref, nu_ref, x_ref, wgu_ref, bgu_ref, wdn_ref, bdn_ref, y_ref, wgu_s, wdn_s, *, ff):
    i = pl.program_id(0)
    used = i < nu_ref[0]

    @pl.when(used)
    def _():
        e_new = jnp.logical_or(i == 0, be_ref[i] != be_ref[jnp.maximum(i - 1, 0)])

        @pl.when(e_new)
        def _():
            wgu_s[...] = wgu_ref[0, 0].astype(BF16)
            wdn_s[...] = wdn_ref[0, 0].astype(BF16)

        half = wgu_s.shape[0] // 2
        pcs = half // LANES
        tmb = x_ref.shape[0] // pcs
        x_lo, x_hi = _unpack_pair(_load_tiles(x_ref, 0, tmb, pcs))
        gu = (_dot(x_lo.astype(BF16), wgu_s[:half, :]) + _dot(x_hi.astype(BF16), wgu_s[half:, :])
              + bgu_ref[0, 0])
        gt = jnp.minimum(gu[:, :ff], SWIGLU_LIMIT)
        up = jnp.clip(gu[:, ff:], -SWIGLU_LIMIT, SWIGLU_LIMIT)
        act = ((up + 1.0) * gt * jax.nn.sigmoid(SWIGLU_ALPHA * gt)).astype(BF16)
        bdn = bdn_ref[0, 0]
        y_lo = _dot(act, wdn_s[:, :half]) + bdn[:, :half]
        y_hi = _dot(act, wdn_s[:, half:]) + bdn[:, half:]
        _store_tiles(y_ref, 0, _pack_pair(y_lo, y_hi))

    @pl.when(jnp.logical_not(used))
    def _():
        y_ref[...] = jnp.zeros(y_ref.shape, I32)


def _moe_experts(xs, plan, w_gu, b_gu, w_dn, b_dn, layer):
    depth, n_exp, d, ff2 = w_gu.shape
    ff = ff2 // 2
    tmb = plan['tmb']
    n_blocks = plan['n_blocks']
    rows = xs.shape[0] // n_blocks
    wmap = lambda i, be, nu: (layer, be[i], 0, 0)
    grid_spec = pltpu.PrefetchScalarGridSpec(
        num_scalar_prefetch=2,
        grid=(n_blocks,),
        in_specs=[pl.BlockSpec((rows, LANES), lambda i, be, nu: (i, 0)),
                  pl.BlockSpec((1, 1, d, ff2), wmap),
                  pl.BlockSpec((1, 1, 1, ff2), wmap),
                  pl.BlockSpec((1, 1, ff, d), wmap),
                  pl.BlockSpec((1, 1, 1, d), wmap)],
        out_specs=pl.BlockSpec((rows, LANES), lambda i, be, nu: (i, 0)),
        scratch_shapes=[pltpu.VMEM((d, ff2), BF16), pltpu.VMEM((ff, d), BF16)],
    )
    return pl.pallas_call(
        functools.partial(_moe_kernel, ff=ff),
        grid_spec=grid_spec,
        out_shape=jax.ShapeDtypeStruct(xs.shape, I32),
        compiler_params=_params("arbitrary"),
        name="moe_experts",
    )(plan['be'], plan['n_used'], xs, w_gu, b_gu.reshape(depth, n_exp, 1, ff2),
      w_dn, b_dn.reshape(depth, n_exp, 1, d))


def _combine_kernel(dest_ref, dnext_ref, x_ref, gate_ref, g_ref, ys_hbm, o_ref, ybuf, sems):
    i = pl.program_id(0)
    tm, d = x_ref.shape
    half = d // 2
    pcs = half // LANES
    slot = i % 2

    def gather(d_ref, s):
        def row(r, carry):
            for kk in range(TOP_K):
                t = d_ref[0, 0, TOP_K * r + kk]
                pltpu.make_async_copy(_tile_at(ys_hbm, t, pcs), _tile_at(ybuf.at[s, kk], r, pcs),
                                      sems.at[s]).start(priority=kk % 2)
            return carry

        lax.fori_loop(0, tm, row, 0, unroll=4)

    @pl.when(i == 0)
    def _():
        gather(dest_ref, 0)

    @pl.when(i + 1 < pl.num_programs(0))
    def _():
        gather(dnext_ref, 1 - slot)

    for kk in range(TOP_K):
        pltpu.make_async_copy(ys_hbm.at[pl.ds(0, tm * pcs)], ybuf.at[slot, kk], sems.at[slot]).wait()

    gt = gate_ref[...]
    lo = hi = None
    for kk in range(TOP_K):
        w = gt[:, kk:kk + 1]
        y_lo, y_hi = _unpack_pair(_load_tiles(ybuf.at[slot, kk], 0, tm, pcs))
        lo = w * y_lo if lo is None else lo + w * y_lo
        hi = w * y_hi if hi is None else hi + w * y_hi
    g = g_ref[...]
    o_ref[:, :half] = x_ref[:, :half] + _gate_rows(lo, g[:, :half])
    o_ref[:, half:] = x_ref[:, half:] + _gate_rows(hi, g[:, half:])


def _combine(x, ys, gates, g2, plan):
    n, d = x.shape
    nb = g2.shape[0]
    pcs = d // 2 // LANES
    tm = _tile(n, 512, 8 * nb)
    nt = n // tm
    dest = plan['dest'].reshape(nt, 1, TOP_K * tm)
    return pl.pallas_call(
        _combine_kernel,
        grid=(nt,),
        in_specs=[pl.BlockSpec((1, 1, TOP_K * tm), lambda i: (i, 0, 0), memory_space=pltpu.SMEM),
                  pl.BlockSpec((1, 1, TOP_K * tm), lambda i: (jnp.minimum(i + 1, nt - 1), 0, 0),
                               memory_space=pltpu.SMEM),
                  pl.BlockSpec((tm, d), lambda i: (i, 0)),
                  pl.BlockSpec((tm, LANES), lambda i: (i, 0)),
                  pl.BlockSpec((nb, d), lambda i: (0, 0)),
                  pl.BlockSpec(memory_space=pl.ANY)],
        out_specs=pl.BlockSpec((tm, d), lambda i: (i, 0)),
        out_shape=jax.ShapeDtypeStruct((n, d), F32),
        scratch_shapes=[pltpu.VMEM((2, TOP_K, tm * pcs, LANES), I32), pltpu.SemaphoreType.DMA((2,))],
        compiler_params=_params("arbitrary"),
        name="moe_combine",
    )(dest, dest, x, gates, g2, ys)


def _final_kernel(x_ref, g_ref, o_ref):
    x = x_ref[...]
    ms = jnp.mean(x * x, axis=-1, keepdims=True)
    o_ref[...] = (x * lax.rsqrt(ms + EPS)) * g_ref[...]


def _final_norm(x, g):
    n, d = x.shape
    tm = _tile(n, 512, 8)
    return pl.pallas_call(
        _final_kernel,
        grid=(n // tm,),
        in_specs=[pl.BlockSpec((tm, d), lambda i: (i, 0)), pl.BlockSpec((1, d), lambda i: (0, 0))],
        out_specs=pl.BlockSpec((tm, d), lambda i: (i, 0)),
        out_shape=jax.ShapeDtypeStruct((n, d), F32),
        compiler_params=_params("parallel"),
        name="final_norm",
    )(x, g)


def kernel(x, c, ada_w, ada_b, norm_mix, norm_ffn, norm_final, s5_lambda_re, s5_lambda_im, s5_log_dt,
           s5_b_re, s5_b_im, s5_c_re, s5_c_im, s5_d, s5_w_glu, hg_w_in, hg_lb_raw, hg_norm, hg_w_out,
           router_w, router_b, moe_w_gate_up, moe_b_gate_up, moe_w_down, moe_b_down):
    bsz, seq, d = x.shape
    depth = ada_w.shape[0]
    n = bsz * seq
    dv = hg_norm.shape[-1]
    heads = d // dv
    fdim = (hg_w_in.shape[-1] - 2 * d) // 2
    dk = fdim // heads
    assert dk == LANES and dv == LANES, "one HGRN2 head per 128-lane tile"
    tmb = _tile(n * TOP_K, MOE_ROWS, SUBLANES)

    lb_p = jax.nn.softmax(hg_lb_raw.astype(F32), axis=0)
    lb_all = jnp.cumsum(lb_p, axis=0) - lb_p[0]
    mod = _ada_mod(c.astype(F32), ada_w, ada_b)

    xs = x.astype(F32).transpose(1, 0, 2).reshape(n, d)
    row = lambda v: v.astype(F32).reshape(1, -1)
    for i in range(depth):
        sh1, sc1, g1, sh2, sc2, g2 = [mod[i, :, k * d:(k + 1) * d] for k in range(N_ADA)]
        j = i // 2
        if i % 2 == 0:
            bmat, cmat, lre, lim = _s5_prep(s5_lambda_re[j], s5_lambda_im[j], s5_log_dt[j],
                                            s5_b_re[j], s5_b_im[j], s5_c_re[j], s5_c_im[j])
            z = _s5_scan(xs, row(norm_mix[i]), sc1, sh1, bmat, cmat, lre, lim, row(s5_d[j]), bsz)
            xs = _mm_res(z, s5_w_glu[j].astype(BF16), xs, g1, glu=True)
        else:
            lb = lb_all[i]
            proj = _hg_proj(xs, row(norm_mix[i]), sc1, sh1, hg_w_in[j].astype(BF16),
                            row(jnp.log(lb)), row(jnp.log1p(-lb)), fdim)
            o = _hg_rec(proj, row(hg_norm[j]), bsz, heads, dk)
            xs = _mm_res(o, hg_w_out[j].astype(BF16), xs, g1, glu=False)
        hp, top_idx, gates, rank, counts = _router(xs, row(norm_ffn[i]), sc2, sh2, router_w[i], router_b[i])
        plan = _moe_plan(top_idx[:, :TOP_K], rank[:, :TOP_K], counts, tmb)
        xd = _dispatch(hp, plan, n)
        ys = _moe_experts(xd, plan, moe_w_gate_up, moe_b_gate_up.astype(F32),
                          moe_w_down, moe_b_down.astype(F32), i)
        xs = _combine(xs, ys, gates, g2, plan)
    out = _final_norm(xs, row(norm_final))
    return out.reshape(seq, bsz, d).transpose(1, 0, 2).astype(x.dtype)
```

```python
import functools

import numpy as np
import jax
import jax.numpy as jnp
from jax import lax
from jax.experimental import pallas as pl
from jax.experimental.pallas import tpu as pltpu

F32 = jnp.float32
BF16 = jnp.bfloat16
I32 = jnp.int32

EPS = 1e-6
N_ADA = 6
TOP_K = 4
S5_RE_MAX = -1e-4
SWIGLU_LIMIT = 7.0
SWIGLU_ALPHA = 1.702
GELU_C0 = 0.7978845608028654
GELU_C1 = 0.044715

LANES = 128
SUBLANES = 8
HG_CHUNK = 64
HG_GROUP = 4
S5_CHUNK = 128
S5_SUB = 4
NORM_ROWS = 32
MOE_ROWS = 512
HI16 = -65536
VMEM_LIMIT = 56 * 1024 * 1024


def _params(*sem):
    return pltpu.CompilerParams(dimension_semantics=sem, vmem_limit_bytes=VMEM_LIMIT)


def _tile(n, pref, align):
    if n <= pref:
        return n
    t = (pref // align) * align
    while t > align and n % t:
        t -= align
    assert n % t == 0, (n, pref, align)
    return t


def _dot(a, b):
    return jnp.dot(a, b, preferred_element_type=F32)


def _dot_nt(a, b):
    return lax.dot_general(a, b, (((1,), (1,)), ((), ())), preferred_element_type=F32)


def _dot_tn(a, b):
    return lax.dot_general(a, b, (((0,), (0,)), ((), ())), preferred_element_type=F32)


def _split(a):
    hi = a.astype(BF16)
    lo = (a - hi.astype(F32)).astype(BF16)
    return hi, lo


def _dot3(a, w):
    ah, al = _split(a)
    wh, wl = _split(w)
    return _dot(ah, wh) + _dot(al, wh) + _dot(ah, wl)


def _norm_mod(x, g, sc, sh):
    tm, d = x.shape
    nb = sc.shape[0]
    ms = jnp.mean(x * x, axis=-1, keepdims=True)
    y = (x * lax.rsqrt(ms + EPS)) * g
    y3 = y.reshape(tm // nb, nb, d)
    return (y3 * (1.0 + sc)[None] + sh[None]).reshape(tm, d)


def _norm_mod_rows(x_ref, g_ref, sc_ref, sh_ref, emit):
    rows = x_ref.shape[0]
    ch = min(NORM_ROWS, rows)

    def body(i, carry):
        r0 = pl.multiple_of(i * ch, ch)
        emit(r0, _norm_mod(x_ref[pl.ds(r0, ch), :], g_ref[...], sc_ref[...], sh_ref[...]))
        return carry

    lax.fori_loop(0, rows // ch, body, 0, unroll=4 if (rows // ch) % 4 == 0 else 1)


def _gate_rows(p, g):
    tm, tn = p.shape
    nb = g.shape[0]
    return (p.reshape(tm // nb, nb, tn) * g[None]).reshape(tm, tn)


def _pack_pair(lo, hi):
    lo_b = lax.bitcast_convert_type(lo.astype(BF16).astype(F32), I32)
    hi_b = lax.bitcast_convert_type(hi.astype(BF16).astype(F32), I32)
    return lax.shift_right_logical(lo_b, 16) | (hi_b & HI16)


def _unpack_pair(p):
    lo = lax.bitcast_convert_type(lax.shift_left(p, 16), F32)
    hi = lax.bitcast_convert_type(p & HI16, F32)
    return lo, hi


def _store_tiles(ref, row0, val):
    rows, width = val.shape
    pcs = width // LANES
    for jj in range(pcs):
        ref[pl.ds(row0 * pcs + jj, rows, stride=pcs), :] = val[:, jj * LANES:(jj + 1) * LANES]


def _load_tiles(ref, row0, rows, pcs):
    return jnp.concatenate([ref[pl.ds(row0 * pcs + jj, rows, stride=pcs), :] for jj in range(pcs)], axis=1)


def _ada_kernel(c_ref, w_ref, b_ref, o_ref):
    c = c_ref[...]
    o_ref[0] = _dot3(c * jax.nn.sigmoid(c), w_ref[0]) + b_ref[0]


def _ada_mod(c, ada_w, ada_b):
    depth, d, nd = ada_w.shape
    nb = c.shape[0]
    tn = _tile(nd, 768, LANES)
    return pl.pallas_call(
        _ada_kernel,
        grid=(depth, nd // tn),
        in_specs=[pl.BlockSpec((nb, d), lambda i, j: (0, 0)),
                  pl.BlockSpec((1, d, tn), lambda i, j: (i, 0, j)),
                  pl.BlockSpec((1, 1, tn), lambda i, j: (i, 0, j))],
        out_specs=pl.BlockSpec((1, nb, tn), lambda i, j: (i, 0, j)),
        out_shape=jax.ShapeDtypeStruct((depth, nb, nd), F32),
        compiler_params=_params("parallel", "parallel"),
        name="ada_mod",
    )(c, ada_w, ada_b.reshape(depth, 1, nd))


def _s5_kernel(x_ref, g_ref, sc_ref, sh_ref, bmat_ref, cmat_ref, lre_ref, lim_ref, d_ref,
               z_ref, h_s, st_s, bu_s, sa_s, *, tc, nb, hw):
    c = pl.program_id(0)
    j = pl.program_id(1)
    n_j = h_s.shape[0]

    @pl.when(j == 0)
    def _():
        def emit(r0, h):
            for jj in range(n_j):
                h_s[jj, pl.ds(r0, h.shape[0]), :] = h[:, jj * LANES:(jj + 1) * LANES]

        _norm_mod_rows(x_ref, g_ref, sc_ref, sh_ref, emit)

    @pl.when(c == 0)
    def _():
        st_s[j] = jnp.zeros(st_s.shape[1:], F32)

    n_sub = bu_s.shape[0]
    ts = tc // n_sub
    rs = ts * nb
    bmat = bmat_ref[0]
    cmat = cmat_ref[0]
    for k in range(n_sub):
        bu_s[k] = _dot(h_s[j, k * rs:(k + 1) * rs, :].astype(BF16), bmat)
    lre = jnp.broadcast_to(lre_ref[0], (nb, hw))
    lim = jnp.broadcast_to(lim_ref[0], (nb, hw))
    st = st_s[j]
    s_re, s_im = st[:, :hw], st[:, hw:]
    for k in range(n_sub):
        for t in range(ts):
            b = bu_s[k, t * nb:(t + 1) * nb, :]
            s_re, s_im = (lre * s_re - lim * s_im + b[:, :hw],
                          lre * s_im + lim * s_re + b[:, hw:])
            sa_s[k, t * nb:(t + 1) * nb, :hw] = s_re
            sa_s[k, t * nb:(t + 1) * nb, hw:] = s_im
        y = _dot(sa_s[k].astype(BF16), cmat) + d_ref[...] * h_s[j, k * rs:(k + 1) * rs, :]
        z = 0.5 * y * (1.0 + jnp.tanh(GELU_C0 * (y + GELU_C1 * (y * y * y))))
        z_ref[k * rs:(k + 1) * rs, :] = z.astype(BF16)
    st_s[j, :, :hw] = s_re
    st_s[j, :, hw:] = s_im


def _s5_prep(lam_re, lam_im, log_dt, b_re, b_im, c_re, c_im):
    n_g, n_p = lam_re.shape
    n_h = b_re.shape[-1]
    gpt = LANES // n_h
    n_j = n_g // gpt
    lam = lax.complex(jnp.minimum(lam_re.astype(F32), S5_RE_MAX), lam_im.astype(F32))
    dt = jnp.exp(log_dt.astype(F32))[:, None]
    lam_bar = jnp.exp(lam * dt)
    b_bar = ((lam_bar - 1.0) / lam)[:, :, None] * lax.complex(b_re.astype(F32), b_im.astype(F32))
    eye = jnp.eye(gpt, dtype=F32)
    bb = b_bar.reshape(n_j, gpt, n_p, n_h)
    bre = jnp.einsum('jkph,gk->jghkp', bb.real, eye)
    bim = jnp.einsum('jkph,gk->jghkp', bb.imag, eye)
    bmat = jnp.stack([bre, bim], axis=3).reshape(n_j, LANES, 2 * gpt * n_p)
    cre = jnp.einsum('jghp,kg->jkpgh', c_re.astype(F32).reshape(n_j, gpt, n_h, n_p), eye)
    cim = jnp.einsum('jghp,kg->jkpgh', c_im.astype(F32).reshape(n_j, gpt, n_h, n_p), eye)
    cmat = jnp.stack([cre, -cim], axis=1).reshape(n_j, 2 * gpt * n_p, LANES)
    lre = lam_bar.real.reshape(n_j, 1, gpt * n_p)
    lim = lam_bar.imag.reshape(n_j, 1, gpt * n_p)
    return bmat.astype(BF16), cmat.astype(BF16), lre, lim


def _s5_scan(x, g, sc, sh, bmat, cmat, lre, lim, d_skip, nb):
    n, d = x.shape
    n_j = d // LANES
    hw = lre.shape[-1]
    seq = n // nb
    tc = _tile(seq, S5_CHUNK, 8)
    rows = tc * nb
    n_sub = S5_SUB if tc % S5_SUB == 0 else 1
    kern = functools.partial(_s5_kernel, tc=tc, nb=nb, hw=hw)
    return pl.pallas_call(
        kern,
        grid=(seq // tc, n_j),
        in_specs=[pl.BlockSpec((rows, d), lambda c, j: (c, 0)),
                  pl.BlockSpec((1, d), lambda c, j: (0, 0)),
                  pl.BlockSpec((nb, d), lambda c, j: (0, 0)),
                  pl.BlockSpec((nb, d), lambda c, j: (0, 0)),
                  pl.BlockSpec((1, LANES, 2 * hw), lambda c, j: (j, 0, 0)),
                  pl.BlockSpec((1, 2 * hw, LANES), lambda c, j: (j, 0, 0)),
                  pl.BlockSpec((1, 1, hw), lambda c, j: (j, 0, 0)),
                  pl.BlockSpec((1, 1, hw), lambda c, j: (j, 0, 0)),
                  pl.BlockSpec((1, LANES), lambda c, j: (0, j))],
        out_specs=pl.BlockSpec((rows, LANES), lambda c, j: (c, j)),
        out_shape=jax.ShapeDtypeStruct((n, d), BF16),
        scratch_shapes=[pltpu.VMEM((n_j, rows, LANES), F32),
                        pltpu.VMEM((n_j, nb, 2 * hw), F32),
                        pltpu.VMEM((n_sub, rows // n_sub, 2 * hw), F32),
                        pltpu.VMEM((n_sub, rows // n_sub, 2 * hw), F32)],
        compiler_params=_params("arbitrary", "arbitrary"),
        name="s5_scan",
    )(x, g, sc, sh, bmat, cmat, lre, lim, d_skip)


def _mm_res_kernel(a_ref, w_ref, x_ref, g_ref, o_ref, *, glu, tn):
    a = a_ref[...]
    d = o_ref.shape[1]
    for c0 in range(0, d, tn):
        p = _dot(a, w_ref[:, c0:c0 + tn])
        if glu:
            p = p * jax.nn.sigmoid(_dot(a, w_ref[:, d + c0:d + c0 + tn]))
        o_ref[:, c0:c0 + tn] = x_ref[:, c0:c0 + tn] + _gate_rows(p, g_ref[:, c0:c0 + tn])


def _mm_res(a, w, x, gate, glu):
    n, kdim = a.shape
    d = x.shape[1]
    nb = gate.shape[0]
    tm = _tile(n, 512, 8 * nb)
    tn = _tile(d, 512, LANES)
    return pl.pallas_call(
        functools.partial(_mm_res_kernel, glu=glu, tn=tn),
        grid=(n // tm,),
        in_specs=[pl.BlockSpec((tm, kdim), lambda i: (i, 0)),
                  pl.BlockSpec(w.shape, lambda i: (0, 0), pipeline_mode=pl.Buffered(1)),
                  pl.BlockSpec((tm, d), lambda i: (i, 0)),
                  pl.BlockSpec((nb, d), lambda i: (0, 0))],
        out_specs=pl.BlockSpec((tm, d), lambda i: (i, 0)),
        out_shape=jax.ShapeDtypeStruct((n, d), F32),
        compiler_params=_params("parallel"),
        name="glu_res" if glu else "proj_res",
    )(a, w, x, gate)


def _hg_proj_kernel(x_ref, g_ref, sc_ref, sh_ref, w_ref, la_ref, l1_ref, o_ref, h_s, *, f_lo, f_hi):
    j = pl.program_id(1)

    @pl.when(j == 0)
    def _():
        def emit(r0, h):
            h_s[pl.ds(r0, h.shape[0]), :] = h.astype(BF16)

        _norm_mod_rows(x_ref, g_ref, sc_ref, sh_ref, emit)

    p = _dot(h_s[...], w_ref[...])
    is_f = jnp.logical_and(j >= f_lo, j < f_hi)

    @pl.when(is_f)
    def _():
        ls = jnp.minimum(p, 0.0) - jnp.log(1.0 + jnp.exp(-jnp.abs(p)))
        bt = l1_ref[...] + ls
        la = la_ref[...]
        o_ref[...] = jnp.maximum(la, bt) + jnp.log(1.0 + jnp.exp(-jnp.abs(la - bt)))

    @pl.when(jnp.logical_not(is_f))
    def _():
        o_ref[...] = p


def _hg_proj(x, g, sc, sh, w_in, log_lb, log1m_lb, fdim):
    n, d = x.shape
    nb = sc.shape[0]
    cols = w_in.shape[1]
    tm = _tile(n, 1024, 8 * nb)
    tn = _tile(fdim, 512, LANES)
    f_lo = fdim // tn
    f_hi = 2 * fdim // tn

    def lb_map(i, j):
        return (0, jnp.clip(j - f_lo, 0, f_lo - 1))

    return pl.pallas_call(
        functools.partial(_hg_proj_kernel, f_lo=f_lo, f_hi=f_hi),
        grid=(n // tm, cols // tn),
        in_specs=[pl.BlockSpec((tm, d), lambda i, j: (i, 0)),
                  pl.BlockSpec((1, d), lambda i, j: (0, 0)),
                  pl.BlockSpec((nb, d), lambda i, j: (0, 0)),
                  pl.BlockSpec((nb, d), lambda i, j: (0, 0)),
                  pl.BlockSpec((d, tn), lambda i, j: (0, j)),
                  pl.BlockSpec((1, tn), lb_map),
                  pl.BlockSpec((1, tn), lb_map)],
        out_specs=pl.BlockSpec((tm, tn), lambda i, j: (i, j)),
        out_shape=jax.ShapeDtypeStruct((n, cols), F32),
        scratch_shapes=[pltpu.VMEM((tm, d), BF16)],
        compiler_params=_params("parallel", "arbitrary"),
        name="hg_proj",
    )(x, g, sc, sh, w_in, log_lb, log1m_lb)


def _hg_levels(ch):
    levels = []
    m = ch // 2
    while m >= 1:
        levels.append(m)
        m //= 2
    return levels


def _hg_masks(ch):
    levels = _hg_levels(ch)
    mk = np.zeros((len(levels) + 1, ch, ch), np.float32)
    for li, m in enumerate(levels):
        for t in range(ch):
            mid = (t // (2 * m)) * 2 * m + m - 1
            if t > mid:
                mk[li, t, mid - m + 1:mid + 1] = 1.0
    mk[-1] = np.eye(ch, dtype=np.float32)
    return mk


def _cumsum_rows(x, t_idx):
    sh = 1
    while sh < x.shape[0]:
        x = x + jnp.where(t_idx >= sh, pltpu.roll(x, sh, 0), 0.0)
        sh *= 2
    return x


def _level_ref(bc, m, sub3):
    ch, dk = bc.shape
    if 2 * m >= SUBLANES:
        bp = bc.reshape(ch // (2 * m), 2 * m, dk)
        return jnp.broadcast_to(bp[:, m - 1:m, :], bp.shape).reshape(ch, dk)
    b3 = bc.reshape(ch // SUBLANES, SUBLANES, dk)
    r = None
    for start in range(0, SUBLANES, 2 * m):
        cand = jnp.broadcast_to(b3[:, start + m - 1:start + m, :], b3.shape)
        r = cand if r is None else jnp.where(sub3 >= start, cand, r)
    return r.reshape(ch, dk)


def _hg_rec_kernel(q_ref, lf_ref, v_ref, gt_ref, mask_ref, gn_ref, o_ref, st_s, ob_s, *, nb, ch):
    c = pl.program_id(1)

    @pl.when(c == 0)
    def _():
        st_s[...] = jnp.zeros(st_s.shape, F32)

    dk = q_ref.shape[-1]
    levels = _hg_levels(ch)
    n_lv = len(levels)
    gn = gn_ref[...]
    t_idx = lax.broadcasted_iota(I32, (ch, dk), 0)
    sub3 = lax.broadcasted_iota(I32, (ch // SUBLANES, SUBLANES, dk), 1)
    masks = [mask_ref[li] > 0.5 for li in range(n_lv + 1)]

    for b0 in range(0, nb, HG_GROUP):
        grp = range(b0, min(b0 + HG_GROUP, nb))
        rows = {b: pl.ds(b, ch, stride=nb) for b in grp}
        qk, qms, kms, q_in, k_out, decay = {}, {}, {}, {}, {}, {}
        for b in grp:
            q = q_ref[rows[b], :]
            lf = lf_ref[rows[b], :]
            bc = _cumsum_rows(lf, t_idx)
            qb = q.astype(BF16)
            kb = (1.0 - jnp.exp(lf)).astype(BF16)
            qk[b] = (qb, kb)
            qms[b], kms[b] = [], []
            for m in levels:
                e = jnp.exp(-jnp.abs(bc - _level_ref(bc, m, sub3))).astype(BF16)
                qms[b].append(qb * e)
                kms[b].append(kb * e)
            b_end = bc[ch - 1:ch, :]
            q_in[b] = qb * jnp.exp(bc).astype(BF16)
            k_out[b] = kb * jnp.exp(b_end - bc).astype(BF16)
            decay[b] = jnp.exp(b_end)
        scores = {}
        for b in grp:
            s = jnp.where(masks[n_lv], _dot_nt(*qk[b]), 0.0)
            for li in range(n_lv):
                s = jnp.where(masks[li], _dot_nt(qms[b][li], kms[b][li]), s)
            scores[b] = s.astype(BF16)
        outs = {}
        for b in grp:
            v = v_ref[rows[b], :].astype(BF16)
            st = st_s[b]
            outs[b] = _dot(scores[b], v) + _dot_nt(q_in[b], st.astype(BF16))
            st_s[b] = decay[b] * st + _dot_tn(v, k_out[b])
        for b in grp:
            o = outs[b]
            gate = gt_ref[rows[b], :]
            o = o * lax.rsqrt(jnp.mean(o * o, axis=-1, keepdims=True) + EPS) * gn
            ob_s[rows[b], :] = o * (gate * jax.nn.sigmoid(gate))
    o_ref[...] = ob_s[...].astype(BF16)


def _hg_rec(proj, g_norm, nb, heads, dk):
    n = proj.shape[0]
    seq = n // nb
    ch = _tile(seq, HG_CHUNK, 8)
    rows = ch * nb
    mk = _hg_masks(ch)
    blk = lambda off: pl.BlockSpec((rows, dk), lambda h, c: (c, off + h))
    return pl.pallas_call(
        functools.partial(_hg_rec_kernel, nb=nb, ch=ch),
        grid=(heads, seq // ch),
        in_specs=[blk(0), blk(heads), blk(2 * heads), blk(3 * heads),
                  pl.BlockSpec(mk.shape, lambda h, c: (0, 0, 0)),
                  pl.BlockSpec((1, dk), lambda h, c: (0, 0))],
        out_specs=pl.BlockSpec((rows, dk), lambda h, c: (c, h)),
        out_shape=jax.ShapeDtypeStruct((n, heads * dk), BF16),
        scratch_shapes=[pltpu.VMEM((nb, dk, dk), F32), pltpu.VMEM((rows, dk), F32)],
        compiler_params=_params("parallel", "arbitrary"),
        name="hg_rec",
    )(proj, proj, proj, proj, jnp.asarray(mk, F32), g_norm)


def _router_kernel(x_ref, g_ref, sc_ref, sh_ref, wh_ref, wl_ref, br_ref, tri_ref,
                   hp_ref, idx_ref, gate_ref, rank_ref, cnt_ref, cnt_s):
    @pl.when(pl.program_id(0) == 0)
    def _():
        cnt_s[...] = jnp.zeros(cnt_s.shape, F32)

    h = _norm_mod(x_ref[...], g_ref[...], sc_ref[...], sh_ref[...])
    half = h.shape[1] // 2
    _store_tiles(hp_ref, 0, _pack_pair(h[:, :half], h[:, half:]))
    hh, hl = _split(h)
    wh = wh_ref[...]
    vals = _dot(hh, wh) + _dot(hl, wh) + _dot(hh, wl_ref[...]) + br_ref[...]
    tm, n_exp = vals.shape
    lane = lax.broadcasted_iota(I32, (tm, n_exp), 1)
    tops, idxs, hots = [], [], []
    for _ in range(TOP_K):
        m = jnp.max(vals, axis=-1, keepdims=True)
        i = jnp.min(jnp.where(vals == m, lane, n_exp), axis=-1, keepdims=True)
        tops.append(m)
        idxs.append(i)
        hots.append(lane == i)
        vals = jnp.where(hots[-1], -jnp.inf, vals)
    es = [jnp.exp(t - tops[0]) for t in tops]
    den = es[0] + es[1] + es[2] + es[3]

    sel = [jnp.where(hm, 1.0, 0.0) for hm in hots]
    multi = sel[0] + sel[1] + sel[2] + sel[3]
    base = cnt_s[...] + _dot(tri_ref[...], multi.astype(BF16))
    cnt_s[...] = cnt_s[...] + jnp.sum(multi, axis=0, keepdims=True)
    cnt_ref[...] = cnt_s[...]

    lane_o = lax.broadcasted_iota(I32, idx_ref.shape, 1)
    io = jnp.zeros(idx_ref.shape, I32)
    ro = jnp.zeros(idx_ref.shape, I32)
    go = jnp.zeros(gate_ref.shape, F32)
    for kk in range(TOP_K):
        rk = jnp.sum(sel[kk] * base, axis=-1, keepdims=True).astype(I32)
        io = jnp.where(lane_o == kk, idxs[kk], io)
        ro = jnp.where(lane_o == kk, rk, ro)
        go = jnp.where(lane_o == kk, es[kk] / den, go)
    idx_ref[...] = io
    rank_ref[...] = ro
    gate_ref[...] = go


def _router(x, g, sc, sh, w_r, b_r):
    n, d = x.shape
    nb = sc.shape[0]
    n_exp = w_r.shape[1]
    tm = _tile(n, 512, 8 * nb)
    pcs = d // 2 // LANES
    wh, wl = _split(w_r.astype(F32))
    tri =jnp.asarray(np.tril(np.ones((tm, tm), np.float32), -1), BF16)
    row = lambda w: pl.BlockSpec((tm, w), lambda i: (i, 0))
    full = lambda a: pl.BlockSpec(a.shape, lambda i: (0, 0))
    b2 = b_r.astype(F32).reshape(1, n_exp)
    return pl.pallas_call(
        _router_kernel,
        grid=(n // tm,),
        in_specs=[row(d), full(g), full(sc), full(sh), full(wh), full(wl), full(b2), full(tri)],
        out_specs=[pl.BlockSpec((tm * pcs, LANES), lambda i: (i, 0)), row(LANES), row(LANES), row(LANES),
                   pl.BlockSpec((1, n_exp), lambda i: (0, 0))],
        out_shape=[jax.ShapeDtypeStruct((n * pcs, LANES), I32),
                   jax.ShapeDtypeStruct((n, LANES), I32),
                   jax.ShapeDtypeStruct((n, LANES), F32),
                   jax.ShapeDtypeStruct((n, LANES), I32),
                   jax.ShapeDtypeStruct((1, n_exp), F32)],
        scratch_shapes=[pltpu.VMEM((1, n_exp), F32)],
        compiler_params=_params("arbitrary"),
        name="router",
    )(x, g, sc, sh, wh, wl, b2, tri)


def _moe_plan(top_idx, rank, counts_f, tmb):
    n = top_idx.shape[0]
    n_exp = counts_f.shape[-1]
    counts = counts_f.reshape(n_exp).astype(I32)
    nblk = (counts + tmb - 1) // tmb
    blk_end = jnp.cumsum(nblk)
    blk_start = blk_end - nblk
    n_used = blk_end[-1]
    n_blocks = n * TOP_K // tmb + n_exp
    blk = jnp.arange(n_blocks, dtype=I32)
    be = jnp.minimum(jnp.sum((blk[:, None] >= blk_end[None, :]).astype(I32), axis=1), n_exp - 1)
    last = jnp.sum(jnp.where(blk == n_used - 1, be, 0))
    be = jnp.where(blk < n_used, be, last).astype(I32)
    base = blk_start * tmb
    hot = top_idx[:, :, None] == jnp.arange(n_exp, dtype=I32)
    dest = jnp.sum(jnp.where(hot, base, 0), axis=-1) + rank
    pad_start = base + counts
    pad_len = blk_end * tmb - pad_start
    return dict(be=be, n_used=n_used.reshape(1).astype(I32), dest=dest.astype(I32),
                pad_start=pad_start.astype(I32), pad_len=pad_len.astype(I32),
                n_blocks=n_blocks, tmb=tmb)


def _tile_at(ref, row, pcs):
    return ref.at[pl.ds(pl.multiple_of(row * pcs, pcs), pcs)]


def _dispatch_kernel(ps_ref, pn_ref, nu_ref, dest_ref, hp_ref, xs_hbm, zero_s, sem, *, tmb, pcs):
    tm = hp_ref.shape[0] // pcs

    def row(r, carry):
        src = _tile_at(hp_ref, r, pcs)
        for kk in range(TOP_K):
            t = dest_ref[0, 0, TOP_K * r + kk]
            pltpu.make_async_copy(src, _tile_at(xs_hbm, t, pcs), sem).start(priority=kk % 2)
        return carry

    lax.fori_loop(0, tm, row, 0, unroll=4)
    for _ in range(TOP_K):
        pltpu.make_async_copy(hp_ref, xs_hbm.at[pl.ds(0, tm * pcs)], sem).wait()

    @pl.when(pl.program_id(0) == pl.num_programs(0) - 1)
    def _():
        zero_s[...] = jnp.zeros(zero_s.shape, I32)
        ztile = zero_s.at[pl.ds(0, pcs)]

        def expert(e, carry):
            p0 = ps_ref[e]
            cnt = pn_ref[e]

            def fill(r, c2):
                pltpu.make_async_copy(ztile, _tile_at(xs_hbm, p0 + r, pcs), sem).start()
                return c2

            def drain(r, c2):
                pltpu.make_async_copy(ztile, _tile_at(xs_hbm, p0, pcs), sem).wait()
                return c2

            lax.fori_loop(0, cnt, fill, 0)
            lax.fori_loop(0, cnt, drain, 0)
            return carry

        lax.fori_loop(0, ps_ref.shape[0], expert, 0)

        def block(bk, carry):
            cp = pltpu.make_async_copy(zero_s, _tile_at(xs_hbm, bk, tmb * pcs), sem)
            cp.start()
            cp.wait()
            return carry

        lax.fori_loop(nu_ref[0], xs_hbm.shape[0] // (tmb * pcs), block, 0)


def _dispatch(hp, plan, n):
    pcs = hp.shape[0] // n
    tmb = plan['tmb']
    tm = _tile(n, 1024, SUBLANES)
    dest = plan['dest'].reshape(n // tm, 1, TOP_K * tm)
    grid_spec = pltpu.PrefetchScalarGridSpec(
        num_scalar_prefetch=3,
        grid=(n // tm,),
        in_specs=[pl.BlockSpec((1, 1, TOP_K * tm), lambda i, *_: (i, 0, 0), memory_space=pltpu.SMEM),
                  pl.BlockSpec((tm * pcs, LANES), lambda i, *_: (i, 0))],
        out_specs=pl.BlockSpec(memory_space=pl.ANY),
        scratch_shapes=[pltpu.VMEM((tmb * pcs, LANES), I32), pltpu.SemaphoreType.DMA(())],
    )
    return pl.pallas_call(
        functools.partial(_dispatch_kernel, tmb=tmb, pcs=pcs),
        grid_spec=grid_spec,
        out_shape=jax.ShapeDtypeStruct((plan['n_blocks'] * tmb * pcs, LANES), I32),
        compiler_params=_params("arbitrary"),
        name="moe_dispatch",
    )(plan['pad_start'], plan['pad_len'], plan['n_used'], dest, hp)


def _moe_kernel(be_ref, nu_ref, x_ref, wgu_ref, bgu_ref, wdn_ref, bdn_ref, y_ref, wgu_s, wdn_s, *, ff):
    i = pl.program_id(0)
    used = i < nu_ref[0]

    @pl.when(used)
    def _():
        e_new = jnp.logical_or(i == 0, be_ref[i] != be_ref[jnp.maximum(i - 1, 0)])

        @pl.when(e_new)
        def _():
            wgu_s[...] = wgu_ref[0, 0].astype(BF16)
            wdn_s[...] = wdn_ref[0, 0].astype(BF16)

        half = wgu_s.shape[0] // 2
        pcs = half // LANES
        tmb = x_ref.shape[0] // pcs
        x_lo, x_hi = _unpack_pair(_load_tiles(x_ref, 0, tmb, pcs))
        gu = (_dot(x_lo.astype(BF16), wgu_s[:half, :]) + _dot(x_hi.astype(BF16), wgu_s[half:, :])
              + bgu_ref[0, 0])
        gt = jnp.minimum(gu[:, :ff], SWIGLU_LIMIT)
        up = jnp.clip(gu[:, ff:], -SWIGLU_LIMIT, SWIGLU_LIMIT)
        act = ((up + 1.0) * gt * jax.nn.sigmoid(SWIGLU_ALPHA * gt)).astype(BF16)
        bdn = bdn_ref[0, 0]
        y_lo = _dot(act, wdn_s[:, :half]) + bdn[:, :half]
        y_hi = _dot(act, wdn_s[:, half:]) + bdn[:, half:]
        _store_tiles(y_ref, 0, _pack_pair(y_lo, y_hi))

    @pl.when(jnp.logical_not(used))
    def _():
        y_ref[...] = jnp.zeros(y_ref.shape, I32)


def _moe_experts(xs, plan, w_gu, b_gu, w_dn, b_dn, layer):
    depth, n_exp, d, ff2 = w_gu.shape
    ff = ff2 // 2
    tmb = plan['tmb']
    n_blocks = plan['n_blocks']
    rows = xs.shape[0] // n_blocks
    wmap = lambda i, be, nu: (layer, be[i], 0, 0)
    grid_spec = pltpu.PrefetchScalarGridSpec(
        num_scalar_prefetch=2,
        grid=(n_blocks,),
        in_specs=[pl.BlockSpec((rows, LANES), lambda i, be, nu: (i, 0)),
                  pl.BlockSpec((1, 1, d, ff2), wmap),
                  pl.BlockSpec((1, 1, 1, ff2), wmap),
                  pl.BlockSpec((1, 1, ff, d), wmap),
                  pl.BlockSpec((1, 1, 1, d), wmap)],
        out_specs=pl.BlockSpec((rows, LANES), lambda i, be, nu: (i, 0)),
        scratch_shapes=[pltpu.VMEM((d, ff2), BF16), pltpu.VMEM((ff, d), BF16)],
    )
    return pl.pallas_call(
        functools.partial(_moe_kernel, ff=ff),
        grid_spec=grid_spec,
        out_shape=jax.ShapeDtypeStruct(xs.shape, I32),
        compiler_params=_params("arbitrary"),
        name="moe_experts",
    )(plan['be'], plan['n_used'], xs, w_gu, b_gu.reshape(depth, n_exp, 1, ff2),
      w_dn, b_dn.reshape(depth, n_exp, 1, d))


def _combine_kernel(dest_ref, dnext_ref, x_ref, gate_ref, g_ref, ys_hbm, o_ref, ybuf, sems):
    i = pl.program_id(0)
    tm, d = x_ref.shape
    half = d // 2
    pcs = half // LANES
    slot = i % 2

    def gather(d_ref, s):
        def row(r, carry):
            for kk in range(TOP_K):
                t = d_ref[0, 0, TOP_K * r + kk]
                pltpu.make_async_copy(_tile_at(ys_hbm, t, pcs), _tile_at(ybuf.at[s, kk], r, pcs),
                                      sems.at[s]).start(priority=kk % 2)
            return carry

        lax.fori_loop(0, tm, row, 0, unroll=4)

    @pl.when(i == 0)
    def _():
        gather(dest_ref, 0)

    @pl.when(i + 1 < pl.num_programs(0))
    def _():
        gather(dnext_ref, 1 - slot)

    for kk in range(TOP_K):
        pltpu.make_async_copy(ys_hbm.at[pl.ds(0, tm * pcs)], ybuf.at[slot, kk], sems.at[slot]).wait()

    gt = gate_ref[...]
    lo = hi = None
    for kk in range(TOP_K):
        w = gt[:, kk:kk + 1]
        y_lo, y_hi = _unpack_pair(_load_tiles(ybuf.at[slot, kk], 0, tm, pcs))
        lo = w * y_lo if lo is None else lo + w * y_lo
        hi = w * y_hi if hi is None else hi + w * y_hi
    g = g_ref[...]
    o_ref[:, :half] = x_ref[:, :half] + _gate_rows(lo, g[:, :half])
    o_ref[:, half:] = x_ref[:, half:] + _gate_rows(hi, g[:, half:])


def _combine(x, ys, gates, g2, plan):
    n, d = x.shape
    nb = g2.shape[0]
    pcs = d // 2 // LANES
    tm = _tile(n, 512, 8 * nb)
    nt = n // tm
    dest = plan['dest'].reshape(nt, 1, TOP_K * tm)
    return pl.pallas_call(
        _combine_kernel,
        grid=(nt,),
        in_specs=[pl.BlockSpec((1, 1, TOP_K * tm), lambda i: (i, 0, 0), memory_space=pltpu.SMEM),
                  pl.BlockSpec((1, 1, TOP_K * tm), lambda i: (jnp.minimum(i + 1, nt - 1), 0, 0),
                               memory_space=pltpu.SMEM),
                  pl.BlockSpec((tm, d), lambda i: (i, 0)),
                  pl.BlockSpec((tm, LANES), lambda i: (i, 0)),
                  pl.BlockSpec((nb, d), lambda i: (0, 0)),
                  pl.BlockSpec(memory_space=pl.ANY)],
        out_specs=pl.BlockSpec((tm, d), lambda i: (i, 0)),
        out_shape=jax.ShapeDtypeStruct((n, d), F32),
        scratch_shapes=[pltpu.VMEM((2, TOP_K, tm * pcs, LANES), I32), pltpu.SemaphoreType.DMA((2,))],
        compiler_params=_params("arbitrary"),
        name="moe_combine",
    )(dest, dest, x, gates, g2, ys)


def _final_kernel(x_ref, g_ref, o_ref):
    x = x_ref[...]
    ms = jnp.mean(x * x, axis=-1, keepdims=True)
    o_ref[...] = (x * lax.rsqrt(ms + EPS)) * g_ref[...]


def _final_norm(x, g):
    n, d = x.shape
    tm = _tile(n, 512, 8)
    return pl.pallas_call(
        _final_kernel,
        grid=(n // tm,),
        in_specs=[pl.BlockSpec((tm, d), lambda i: (i, 0)), pl.BlockSpec((1, d), lambda i: (0, 0))],
        out_specs=pl.BlockSpec((tm, d), lambda i: (i, 0)),
        out_shape=jax.ShapeDtypeStruct((n, d), F32),
        compiler_params=_params("parallel"),
        name="final_norm",
    )(x, g)


def kernel(x, c, ada_w, ada_b, norm_mix, norm_ffn, norm_final, s5_lambda_re, s5_lambda_im, s5_log_dt,
           s5_b_re, s5_b_im, s5_c_re, s5_c_im, s5_d, s5_w_glu, hg_w_in, hg_lb_raw, hg_norm, hg_w_out,
           router_w, router_b, moe_w_gate_up, moe_b_gate_up, moe_w_down, moe_b_down):
    bsz, seq, d = x.shape
    depth = ada_w.shape[0]
    n = bsz * seq
    dv = hg_norm.shape[-1]
    heads = d // dv
    fdim = (hg_w_in.shape[-1] - 2 * d) // 2
    dk = fdim // heads
    assert dk == LANES and dv == LANES, "one HGRN2 head per 128-lane tile"
    tmb = _tile(n * TOP_K, MOE_ROWS, SUBLANES)

    lb_p = jax.nn.softmax(hg_lb_raw.astype(F32), axis=0)
    lb_all = jnp.cumsum(lb_p, axis=0) - lb_p[0]
    mod = _ada_mod(c.astype(F32), ada_w, ada_b)

    xs = x.astype(F32).transpose(1, 0, 2).reshape(n, d)
    row = lambda v: v.astype(F32).reshape(1, -1)
    for i in range(depth):
        sh1, sc1, g1, sh2, sc2, g2 = [mod[i, :, k * d:(k + 1) * d] for k in range(N_ADA)]
        j = i // 2
        if i % 2 == 0:
            bmat, cmat, lre, lim = _s5_prep(s5_lambda_re[j], s5_lambda_im[j], s5_log_dt[j],
                                            s5_b_re[j], s5_b_im[j], s5_c_re[j], s5_c_im[j])
            z = _s5_scan(xs, row(norm_mix[i]), sc1, sh1, bmat, cmat, lre, lim, row(s5_d[j]), bsz)
            xs = _mm_res(z, s5_w_glu[j].astype(BF16), xs, g1, glu=True)
        else:
            lb = lb_all[i]
            proj = _hg_proj(xs, row(norm_mix[i]), sc1, sh1, hg_w_in[j].astype(BF16),
                            row(jnp.log(lb)), row(jnp.log1p(-lb)), fdim)
            o = _hg_rec(proj, row(hg_norm[j]), bsz, heads, dk)
            xs = _mm_res(o, hg_w_out[j].astype(BF16), xs, g1, glu=False)
        hp, top_idx, gates, rank, counts = _router(xs, row(norm_ffn[i]), sc2, sh2, router_w[i], router_b[i])
        plan = _moe_plan(top_idx[:, :TOP_K], rank[:, :TOP_K], counts, tmb)
        xd = _dispatch(hp, plan, n)
        ys = _moe_experts(xd, plan, moe_w_gate_up, moe_b_gate_up.astype(F32),
                          moe_w_down, moe_b_down.astype(F32), i)
        xs = _combine(xs, ys, gates, g2, plan)
    out = _final_norm(xs, row(norm_final))
    return out.reshape(seq, bsz, d).transpose(1, 0, 2).astype(x.dtype)
```

```python
import functools

import numpy as np
import jax
import jax.numpy as jnp
from jax import lax
from jax.experimental import pallas as pl
from jax.experimental.pallas import tpu as pltpu

F32 = jnp.float32
BF16 = jnp.bfloat16
I32 = jnp.int32

EPS = 1e-6
N_ADA = 6
TOP_K = 4
S5_RE_MAX = -1e-4
SWIGLU_LIMIT = 7.0
SWIGLU_ALPHA = 1.702
GELU_C0 = 0.7978845608028654
GELU_C1 = 0.044715

LANES = 128
SUBLANES = 8
HG_CHUNK = 64
HG_GROUP = 4
S5_CHUNK = 128
S5_SUB = 4
NORM_ROWS = 32
MOE_ROWS = 512
HI16 = -65536
VMEM_LIMIT = 56 * 1024 * 1024


def _params(*sem):
    return pltpu.CompilerParams(dimension_semantics=sem, vmem_limit_bytes=VMEM_LIMIT)


def _tile(n, pref, align):
    if n <= pref:
        return n
    t = (pref // align) * align
    while t > align and n % t:
        t -= align
    assert n % t == 0, (n, pref, align)
    return t


def _dot(a, b):
    return jnp.dot(a, b, preferred_element_type=F32)


def _dot_nt(a, b):
    return lax.dot_general(a, b, (((1,), (1,)), ((), ())), preferred_element_type=F32)


def _dot_tn(a, b):
    return lax.dot_general(a, b, (((0,), (0,)), ((), ())), preferred_element_type=F32)


def _split(a):
    hi = a.astype(BF16)
    lo = (a - hi.astype(F32)).astype(BF16)
    return hi, lo


def _dot3(a, w):
    ah, al = _split(a)
    wh, wl = _split(w)
    return _dot(ah, wh) + _dot(al, wh) + _dot(ah, wl)


def _norm_mod(x, g, sc, sh):
    tm, d = x.shape
    nb = sc.shape[0]
    ms = jnp.mean(x * x, axis=-1, keepdims=True)
    y = (x * lax.rsqrt(ms + EPS)) * g
    y3 = y.reshape(tm // nb, nb, d)
    return (y3 * (1.0 + sc)[None] + sh[None]).reshape(tm, d)


def _norm_mod_rows(x_ref, g_ref, sc_ref, sh_ref, emit):
    rows = x_ref.shape[0]
    ch = min(NORM_ROWS, rows)

    def body(i, carry):
        r0 = pl.multiple_of(i * ch, ch)
        emit(r0, _norm_mod(x_ref[pl.ds(r0, ch), :], g_ref[...], sc_ref[...], sh_ref[...]))
        return carry

    lax.fori_loop(0, rows // ch, body, 0, unroll=4 if (rows // ch) % 4 == 0 else 1)


def _gate_rows(p, g):
    tm, tn = p.shape
    nb = g.shape[0]
    return (p.reshape(tm // nb, nb, tn) * g[None]).reshape(tm, tn)


def _pack_pair(lo, hi):
    lo_b = lax.bitcast_convert_type(lo.astype(BF16).astype(F32), I32)
    hi_b = lax.bitcast_convert_type(hi.astype(BF16).astype(F32), I32)
    return lax.shift_right_logical(lo_b, 16) | (hi_b & HI16)


def _unpack_pair(p):
    lo = lax.bitcast_convert_type(lax.shift_left(p, 16), F32)
    hi = lax.bitcast_convert_type(p & HI16, F32)
    return lo, hi


def _store_tiles(ref, row0, val):
    rows, width = val.shape
    pcs = width // LANES
    for jj in range(pcs):
        ref[pl.ds(row0 * pcs + jj, rows, stride=pcs), :] = val[:, jj * LANES:(jj + 1) * LANES]


def _load_tiles(ref, row0, rows, pcs):
    return jnp.concatenate([ref[pl.ds(row0 * pcs + jj, rows, stride=pcs), :] for jj in range(pcs)], axis=1)


def _ada_kernel(c_ref, w_ref, b_ref, o_ref):
    c = c_ref[...]
    o_ref[0] = _dot3(c * jax.nn.sigmoid(c), w_ref[0]) + b_ref[0]


def _ada_mod(c, ada_w, ada_b):
    depth, d, nd = ada_w.shape
    nb = c.shape[0]
    tn = _tile(nd, 768, LANES)
    return pl.pallas_call(
        _ada_kernel,
        grid=(depth, nd // tn),
        in_specs=[pl.BlockSpec((nb, d), lambda i, j: (0, 0)),
                  pl.BlockSpec((1, d, tn), lambda i, j: (i, 0, j)),
                  pl.BlockSpec((1, 1, tn), lambda i, j: (i, 0, j))],
        out_specs=pl.BlockSpec((1, nb, tn), lambda i, j: (i, 0, j)),
        out_shape=jax.ShapeDtypeStruct((depth, nb, nd), F32),
        compiler_params=_params("parallel", "parallel"),
        name="ada_mod",
    )(c, ada_w, ada_b.reshape(depth, 1, nd))


def _s5_kernel(x_ref, g_ref, sc_ref, sh_ref, bmat_ref, cmat_ref, lre_ref, lim_ref, d_ref,
               z_ref, h_s, st_s, bu_s, sa_s, *, tc, nb, hw):
    c = pl.program_id(0)
    j = pl.program_id(1)
    n_j = h_s.shape[0]

    @pl.when(j == 0)
    def _():
        def emit(r0, h):
            for jj in range(n_j):
                h_s[jj, pl.ds(r0, h.shape[0]), :] = h[:, jj * LANES:(jj + 1) * LANES]

        _norm_mod_rows(x_ref, g_ref, sc_ref, sh_ref, emit)

    @pl.when(c == 0)
    def _():
        st_s[j] = jnp.zeros(st_s.shape[1:], F32)

    n_sub = bu_s.shape[0]
    ts = tc // n_sub
    rs = ts * nb
    bmat = bmat_ref[0]
    cmat = cmat_ref[0]
    for k in range(n_sub):
        bu_s[k] = _dot(h_s[j, k * rs:(k + 1) * rs, :].astype(BF16), bmat)
    lre = jnp.broadcast_to(lre_ref[0], (nb, hw))
    lim = jnp.broadcast_to(lim_ref[0], (nb, hw))
    st = st_s[j]
    s_re, s_im = st[:, :hw], st[:, hw:]
    for k in range(n_sub):
        for t in range(ts):
            b = bu_s[k, t * nb:(t + 1) * nb, :]
            s_re, s_im = (lre * s_re - lim * s_im + b[:, :hw],
                          lre * s_im + lim * s_re + b[:, hw:])
            sa_s[k, t * nb:(t + 1) * nb, :hw] = s_re
            sa_s[k, t * nb:(t + 1) * nb, hw:] = s_im
        y = _dot(sa_s[k].astype(BF16), cmat) + d_ref[...] * h_s[j, k * rs:(k + 1) * rs, :]
        z = 0.5 * y * (1.0 + jnp.tanh(GELU_C0 * (y + GELU_C1 * (y * y * y))))
        z_ref[k * rs:(k + 1) * rs, :] = z.astype(BF16)
    st_s[j, :, :hw] = s_re
    st_s[j, :, hw:] = s_im


def _s5_prep(lam_re, lam_im, log_dt, b_re, b_im, c_re, c_im):
    n_g, n_p = lam_re.shape
    n_h = b_re.shape[-1]
    gpt = LANES // n_h
    n_j = n_g // gpt
    lam = lax.complex(jnp.minimum(lam_re.astype(F32), S5_RE_MAX), lam_im.astype(F32))
    dt = jnp.exp(log_dt.astype(F32))[:, None]
    lam_bar = jnp.exp(lam * dt)
    b_bar = ((lam_bar - 1.0) / lam)[:, :, None] * lax.complex(b_re.astype(F32), b_im.astype(F32))
    eye = jnp.eye(gpt, dtype=F32)
    bb = b_bar.reshape(n_j, gpt, n_p, n_h)
    bre = jnp.einsum('jkph,gk->jghkp', bb.real, eye)
    bim = jnp.einsum('jkph,gk->jghkp', bb.imag, eye)
    bmat = jnp.stack([bre, bim], axis=3).reshape(n_j, LANES, 2 * gpt * n_p)
    cre = jnp.einsum('jghp,kg->jkpgh', c_re.astype(F32).reshape(n_j, gpt, n_h, n_p), eye)
    cim = jnp.einsum('jghp,kg->jkpgh', c_im.astype(F32).reshape(n_j, gpt, n_h, n_p), eye)
    cmat = jnp.stack([cre, -cim], axis=1).reshape(n_j, 2 * gpt * n_p, LANES)
    lre = lam_bar.real.reshape(n_j, 1, gpt * n_p)
    lim = lam_bar.imag.reshape(n_j, 1, gpt * n_p)
    return bmat.astype(BF16), cmat.astype(BF16), lre, lim


def _s5_scan(x, g, sc, sh, bmat, cmat, lre, lim, d_skip, nb):
    n, d = x.shape
    n_j = d // LANES
    hw = lre.shape[-1]
    seq = n // nb
    tc = _tile(seq, S5_CHUNK, 8)
    rows = tc * nb
    n_sub = S5_SUB if tc % S5_SUB == 0 else 1
    kern = functools.partial(_s5_kernel, tc=tc, nb=nb, hw=hw)
    return pl.pallas_call(
        kern,
        grid=(seq // tc, n_j),
        in_specs=[pl.BlockSpec((rows, d), lambda c, j: (c, 0)),
                  pl.BlockSpec((1, d), lambda c, j: (0, 0)),
                  pl.BlockSpec((nb, d), lambda c, j: (0, 0)),
                  pl.BlockSpec((nb, d), lambda c, j: (0, 0)),
                  pl.BlockSpec((1, LANES, 2 * hw), lambda c, j: (j, 0, 0)),
                  pl.BlockSpec((1, 2 * hw, LANES), lambda c, j: (j, 0, 0)),
                  pl.BlockSpec((1, 1, hw), lambda c, j: (j, 0, 0)),
                  pl.BlockSpec((1, 1, hw), lambda c, j: (j, 0, 0)),
                  pl.BlockSpec((1, LANES), lambda c, j: (0, j))],
        out_specs=pl.BlockSpec((rows, LANES), lambda c, j: (c, j)),
        out_shape=jax.ShapeDtypeStruct((n, d), BF16),
        scratch_shapes=[pltpu.VMEM((n_j, rows, LANES), F32),
                        pltpu.VMEM((n_j, nb, 2 * hw), F32),
                        pltpu.VMEM((n_sub, rows // n_sub, 2 * hw), F32),
                        pltpu.VMEM((n_sub, rows // n_sub, 2 * hw), F32)],
        compiler_params=_params("arbitrary", "arbitrary"),
        name="s5_scan",
    )(x, g, sc, sh, bmat, cmat, lre, lim, d_skip)


def _mm_res_kernel(a_ref, w_ref, x_ref, g_ref, o_ref, *, glu, tn):
    a = a_ref[...]
    d = o_ref.shape[1]
    for c0 in range(0, d, tn):
        p = _dot(a, w_ref[:, c0:c0 + tn])
        if glu:
            p = p * jax.nn.sigmoid(_dot(a, w_ref[:, d + c0:d + c0 + tn]))
        o_ref[:, c0:c0 + tn] = x_ref[:, c0:c0 + tn] + _gate_rows(p, g_ref[:, c0:c0 + tn])


def _mm_res(a, w, x, gate, glu):
    n, kdim = a.shape
    d = x.shape[1]
    nb = gate.shape[0]
    tm = _tile(n, 512, 8 * nb)
    tn = _tile(d, 512, LANES)
    return pl.pallas_call(
        functools.partial(_mm_res_kernel, glu=glu, tn=tn),
        grid=(n // tm,),
        in_specs=[pl.BlockSpec((tm, kdim), lambda i: (i, 0)),
                  pl.BlockSpec(w.shape, lambda i: (0, 0), pipeline_mode=pl.Buffered(1)),
                  pl.BlockSpec((tm, d), lambda i: (i, 0)),
                  pl.BlockSpec((nb, d), lambda i: (0, 0))],
        out_specs=pl.BlockSpec((tm, d), lambda i: (i, 0)),
        out_shape=jax.ShapeDtypeStruct((n, d), F32),
        compiler_params=_params("parallel"),
        name="glu_res" if glu else "proj_res",
    )(a, w, x, gate)


def _hg_proj_kernel(x_ref, g_ref, sc_ref, sh_ref, w_ref, la_ref, l1_ref, o_ref, h_s, *, f_lo, f_hi):
    j = pl.program_id(1)

    @pl.when(j == 0)
    def _():
        def emit(r0, h):
            h_s[pl.ds(r0, h.shape[0]), :] = h.astype(BF16)

        _norm_mod_rows(x_ref, g_ref, sc_ref, sh_ref, emit)

    p = _dot(h_s[...], w_ref[...])
    is_f = jnp.logical_and(j >= f_lo, j < f_hi)

    @pl.when(is_f)
    def _():
        ls = jnp.minimum(p, 0.0) - jnp.log(1.0 + jnp.exp(-jnp.abs(p)))
        bt = l1_ref[...] + ls
        la = la_ref[...]
        o_ref[...] = jnp.maximum(la, bt) + jnp.log(1.0 + jnp.exp(-jnp.abs(la - bt)))

    @pl.when(jnp.logical_not(is_f))
    def _():
        o_ref[...] = p


def _hg_proj(x, g, sc, sh, w_in, log_lb, log1m_lb, fdim):
    n, d = x.shape
    nb = sc.shape[0]
    cols = w_in.shape[1]
    tm = _tile(n, 1024, 8 * nb)
    tn = _tile(fdim, 512, LANES)
    f_lo = fdim // tn
    f_hi = 2 * fdim // tn

    def lb_map(i, j):
        return (0, jnp.clip(j - f_lo, 0, f_lo - 1))

    return pl.pallas_call(
        functools.partial(_hg_proj_kernel, f_lo=f_lo, f_hi=f_hi),
        grid=(n // tm, cols // tn),
        in_specs=[pl.BlockSpec((tm, d), lambda i, j: (i, 0)),
                  pl.BlockSpec((1, d), lambda i, j: (0, 0)),
                  pl.BlockSpec((nb, d), lambda i, j: (0, 0)),
                  pl.BlockSpec((nb, d), lambda i, j: (0, 0)),
                  pl.BlockSpec((d, tn), lambda i, j: (0, j)),
                  pl.BlockSpec((1, tn), lb_map),
                  pl.BlockSpec((1, tn), lb_map)],
        out_specs=pl.BlockSpec((tm, tn), lambda i, j: (i, j)),
        out_shape=jax.ShapeDtypeStruct((n, cols), F32),
        scratch_shapes=[pltpu.VMEM((tm, d), BF16)],
        compiler_params=_params("parallel", "arbitrary"),
        name="hg_proj",
    )(x, g, sc, sh, w_in, log_lb, log1m_lb)


def _hg_levels(ch):
    levels = []
    m = ch // 2
    while m >= 1:
        levels.append(m)
        m //= 2
    return levels


def _hg_masks(ch):
    levels = _hg_levels(ch)
    mk = np.zeros((len(levels) + 1, ch, ch), np.float32)
    for li, m in enumerate(levels):
        for t in range(ch):
            mid = (t // (2 * m)) * 2 * m + m - 1
            if t > mid:
                mk[li, t, mid - m + 1:mid + 1] = 1.0
    mk[-1] = np.eye(ch, dtype=np.float32)
    return mk


def _cumsum_rows(x, t_idx):
    sh = 1
    while sh < x.shape[0]:
        x = x + jnp.where(t_idx >= sh, pltpu.roll(x, sh, 0), 0.0)
        sh *= 2
    return x


def _level_ref(bc, m, sub3):
    ch, dk = bc.shape
    if 2 * m >= SUBLANES:
        bp = bc.reshape(ch // (2 * m), 2 * m, dk)
        return jnp.broadcast_to(bp[:, m - 1:m, :], bp.shape).reshape(ch, dk)
    b3 = bc.reshape(ch // SUBLANES, SUBLANES, dk)
    r = None
    for start in range(0, SUBLANES, 2 * m):
        cand = jnp.broadcast_to(b3[:, start + m - 1:start + m, :], b3.shape)
        r = cand if r is None else jnp.where(sub3 >= start, cand, r)
    return r.reshape(ch, dk)


def _hg_rec_kernel(q_ref, lf_ref, v_ref, gt_ref, mask_ref, gn_ref, o_ref, st_s, ob_s, *, nb, ch):
    c = pl.program_id(1)

    @pl.when(c == 0)
    def _():
        st_s[...] = jnp.zeros(st_s.shape, F32)

    dk = q_ref.shape[-1]
    levels = _hg_levels(ch)
    n_lv = len(levels)
    gn = gn_ref[...]
    t_idx = lax.broadcasted_iota(I32, (ch, dk), 0)
    sub3 = lax.broadcasted_iota(I32, (ch // SUBLANES, SUBLANES, dk), 1)
    masks = [mask_ref[li] > 0.5 for li in range(n_lv + 1)]

    for b0 in range(0, nb, HG_GROUP):
        grp = range(b0, min(b0 + HG_GROUP, nb))
        rows = {b: pl.ds(b, ch, stride=nb) for b in grp}
        qk, qms, kms, q_in, k_out, decay = {}, {}, {}, {}, {}, {}
        for b in grp:
            q = q_ref[rows[b], :]
            lf = lf_ref[rows[b], :]
            bc = _cumsum_rows(lf, t_idx)
            qb = q.astype(BF16)
            kb = (1.0 - jnp.exp(lf)).astype(BF16)
            qk[b] = (qb, kb)
            qms[b], kms[b] = [], []
            for m in levels:
                e = jnp.exp(-jnp.abs(bc - _level_ref(bc, m, sub3))).astype(BF16)
                qms[b].append(qb * e)
                kms[b].append(kb * e)
            b_end = bc[ch - 1:ch, :]
            q_in[b] = qb * jnp.exp(bc).astype(BF16)
            k_out[b] = kb * jnp.exp(b_end - bc).astype(BF16)
            decay[b] = jnp.exp(b_end)
        scores = {}
        for b in grp:
            s = jnp.where(masks[n_lv], _dot_nt(*qk[b]), 0.0)
            for li in range(n_lv):
                s = jnp.where(masks[li], _dot_nt(qms[b][li], kms[b][li]), s)
            scores[b] = s.astype(BF16)
        outs = {}
        for b in grp:
            v = v_ref[rows[b], :].astype(BF16)
            st = st_s[b]
            outs[b] = _dot(scores[b], v) + _dot_nt(q_in[b], st.astype(BF16))
            st_s[b] = decay[b] * st + _dot_tn(v, k_out[b])
        for b in grp:
            o = outs[b]
            gate = gt_ref[rows[b], :]
            o = o * lax.rsqrt(jnp.mean(o * o, axis=-1, keepdims=True) + EPS) * gn
            ob_s[rows[b], :] = o * (gate * jax.nn.sigmoid(gate))
    o_ref[...] = ob_s[...].astype(BF16)


def _hg_rec(proj, g_norm, nb, heads, dk):
    n = proj.shape[0]
    seq = n // nb
    ch = _tile(seq, HG_CHUNK, 8)
    rows = ch * nb
    mk = _hg_masks(ch)
    blk = lambda off: pl.BlockSpec((rows, dk), lambda h, c: (c, off + h))
    return pl.pallas_call(
        functools.partial(_hg_rec_kernel, nb=nb, ch=ch),
        grid=(heads, seq // ch),
        in_specs=[blk(0), blk(heads), blk(2 * heads), blk(3 * heads),
                  pl.BlockSpec(mk.shape, lambda h, c: (0, 0, 0)),
                  pl.BlockSpec((1, dk), lambda h, c: (0, 0))],
        out_specs=pl.BlockSpec((rows, dk), lambda h, c: (c, h)),
        out_shape=jax.ShapeDtypeStruct((n, heads * dk), BF16),
        scratch_shapes=[pltpu.VMEM((nb, dk, dk), F32), pltpu.VMEM((rows, dk), F32)],
        compiler_params=_params("parallel", "arbitrary"),
        name="hg_rec",
    )(proj, proj, proj, proj, jnp.asarray(mk, F32), g_norm)


def _router_kernel(x_ref, g_ref, sc_ref, sh_ref, wh_ref, wl_ref, br_ref, tri_ref,
                   hp_ref, idx_ref, gate_ref, rank_ref, cnt_ref):
    h = _norm_mod(x_ref[...], g_ref[...], sc_ref[...], sh_ref[...])
    half = h.shape[1] // 2
    _store_tiles(hp_ref, 0, _pack_pair(h[:, :half], h[:, half:]))
    hh, hl = _split(h)
    wh = wh_ref[...]
    vals = _dot(hh, wh) + _dot(hl, wh) + _dot(hh, wl_ref[...]) + br_ref[...]
    tm, n_exp = vals.shape
    lane = lax.broadcasted_iota(I32, (tm, n_exp), 1)
    tops, idxs, hots = [], [], []
    for _ in range(TOP_K):
        m = jnp.max(vals, axis=-1, keepdims=True)
        i = jnp.min(jnp.where(vals == m, lane, n_exp), axis=-1, keepdims=True)
        tops.append(m)
        idxs.append(i)
        hots.append(lane == i)
        vals = jnp.where(hots[-1], -jnp.inf, vals)
    es = [jnp.exp(t - tops[0]) for t in tops]
    den = es[0] + es[1] + es[2] + es[3]

    sel = [jnp.where(hm, 1.0, 0.0) for hm in hots]
    multi = sel[0] + sel[1] + sel[2] + sel[3]
    base = _dot(tri_ref[...], multi.astype(BF16))
    cnt_ref[0] = jnp.sum(multi, axis=0, keepdims=True)

    lane_o = lax.broadcasted_iota(I32, idx_ref.shape, 1)
    io = jnp.zeros(idx_ref.shape, I32)
    ro = jnp.zeros(idx_ref.shape, I32)
    go = jnp.zeros(gate_ref.shape, F32)
    for kk in range(TOP_K):
        rk = jnp.sum(sel[kk] * base, axis=-1, keepdims=True).astype(I32)
        io = jnp.where(lane_o == kk, idxs[kk], io)
        ro = jnp.where(lane_o == kk, rk, ro)
        go = jnp.where(lane_o == kk, es[kk] / den, go)
    idx_ref[...] = io
    rank_ref[...] = ro
    gate_ref[...] = go


def _router(x, g, sc, sh, w_r, b_r):
    n, d = x.shape
    nb = sc.shape[0]
    n_exp = w_r.shape[1]
    tm = _tile(n, 512, 8 * nb)
    pcs = d // 2 // LANES
    wh, wl = _split(w_r.astype(F32))
    tri = jnp.asarray(np.tril(np.ones((tm, tm), np.float32), -1), BF16)
    row = lambda w: pl.BlockSpec((tm, w), lambda i: (i, 0))
    full = lambda a: pl.BlockSpec(a.shape, lambda i: (0, 0))
    b2 = b_r.astype(F32).reshape(1, n_exp)
    return pl.pallas_call(
        _router_kernel,
        grid=(n // tm,),
        in_specs=[row(d), full(g), full(sc), full(sh), full(wh), full(wl), full(b2), full(tri)],
        out_specs=[pl.BlockSpec((tm * pcs, LANES), lambda i: (i, 0)), row(LANES), row(LANES), row(LANES),
                   pl.BlockSpec((1, 1, n_exp), lambda i: (i, 0, 0))],
        out_shape=[jax.ShapeDtypeStruct((n * pcs, LANES), I32),
                   jax.ShapeDtypeStruct((n, LANES), I32),
                   jax.ShapeDtypeStruct((n, LANES), F32),
                   jax.ShapeDtypeStruct((n, LANES), I32),
                   jax.ShapeDtypeStruct((n // tm, 1, n_exp), F32)],
        compiler_params=_params("parallel"),
        name="router",
    )(x, g, sc, sh, wh, wl, b2, tri)


def _moe_plan(top_idx, lrank, tile_counts, tmb):
    n = top_idx.shape[0]
    n_tiles, _, n_exp = tile_counts.shape
    tc = tile_counts.reshape(n_tiles, n_exp).astype(I32)
    counts = jnp.sum(tc, axis=0)
    nblk = (counts + tmb - 1) // tmb
    blk_end = jnp.cumsum(nblk)
    blk_start = blk_end - nblk
    n_used = blk_end[-1]
    n_blocks = n * TOP_K // tmb + n_exp
    blk = jnp.arange(n_blocks, dtype=I32)
    be = jnp.minimum(jnp.sum((blk[:, None] >= blk_end[None, :]).astype(I32), axis=1), n_exp - 1)
    last = jnp.sum(jnp.where(blk == n_used - 1, be, 0))
    be = jnp.where(blk < n_used, be, last).astype(I32)
    base = blk_start * tmb
    seg_dst = base[None, :] + jnp.cumsum(tc, axis=0) - tc
    seg_off = jnp.cumsum(tc, axis=1) - tc
    tm = n // n_tiles
    hot = top_idx.reshape(n_tiles, tm, TOP_K, 1) == jnp.arange(n_exp, dtype=I32)
    lpos = jnp.sum(jnp.where(hot, seg_off[:, None, None, :], 0), axis=-1) + lrank.reshape(n_tiles, tm, TOP_K)
    pad_start = base + counts
    pad_len = blk_end * tmb - pad_start
    flat = lambda a: a.reshape(-1).astype(I32)
    return dict(be=be, n_used=n_used.reshape(1).astype(I32),
                lpos=lpos.reshape(n_tiles, 1, tm * TOP_K).astype(I32),
                seg_cnt=flat(tc), seg_off=flat(seg_off), seg_dst=flat(seg_dst),
                pad_start=pad_start.astype(I32), pad_len=pad_len.astype(I32),
                n_blocks=n_blocks, tmb=tmb, n_tiles=n_tiles, n_exp=n_exp)


def _tile_at(ref, row, pcs):
    return ref.at[pl.ds(pl.multiple_of(row * pcs, pcs), pcs)]


def _segments(i, cnt_ref, off_ref, dst_ref, n_exp, pcs, copy):
    def seg(e, carry):
        cnt = cnt_ref[i * n_exp + e]

        @pl.when(cnt > 0)
        def _():
            size = pl.multiple_of(cnt * pcs, pcs)
            copy(pl.ds(pl.multiple_of(off_ref[i * n_exp + e] * pcs, pcs), size),
                 pl.ds(pl.multiple_of(dst_ref[i * n_exp + e] * pcs, pcs), size))

        return carry

    lax.fori_loop(0, n_exp, seg, 0)


def _dispatch_kernel(sc_ref, so_ref, sd_ref, ps_ref, pn_ref, nu_ref, lpos_ref, hp_ref, xs_hbm,
                     sbuf, zero_s, sems, sem, *, tmb, pcs):
    i = pl.program_id(0)
    last = pl.num_programs(0) - 1
    tm = hp_ref.shape[0] // pcs
    slot = i % 2

    def drain(s):
        pltpu.make_async_copy(sbuf.at[s], xs_hbm.at[pl.ds(0, sbuf.shape[1])], sems.at[s]).wait()

    @pl.when(i >= 2)
    def _():
        drain(slot)

    def row(r, carry):
        tile = hp_ref[pl.ds(pl.multiple_of(r * pcs, pcs), pcs), :]
        for kk in range(TOP_K):
            p = lpos_ref[0, 0, TOP_K * r + kk]
            sbuf[slot, pl.ds(pl.multiple_of(p * pcs, pcs), pcs), :] = tile
        return carry

    lax.fori_loop(0, tm, row, 0, unroll=8)

    def copy(src, dst):
        pltpu.make_async_copy(sbuf.at[slot, src], xs_hbm.at[dst], sems.at[slot]).start()

    _segments(i, sc_ref, so_ref, sd_ref, ps_ref.shape[0], pcs, copy)

    @pl.when(i == last)
    def _():
        drain(slot)

        @pl.when(i >= 1)
        def _():
            drain(1 - slot)

        zero_s[...] = jnp.zeros(zero_s.shape, I32)
        ztile = zero_s.at[pl.ds(0, pcs)]

        def expert(e, carry):
            p0 = ps_ref[e]
            cnt = pn_ref[e]

            def fill(r, c2):
                pltpu.make_async_copy(ztile, _tile_at(xs_hbm, p0 + r, pcs), sem).start()
                return c2

            def fill_done(r, c2):
                pltpu.make_async_copy(ztile, _tile_at(xs_hbm, p0, pcs), sem).wait()
                return c2

            lax.fori_loop(0, cnt, fill, 0)
            lax.fori_loop(0, cnt, fill_done, 0)
            return carry

        lax.fori_loop(0, ps_ref.shape[0], expert, 0)

        def block(bk, carry):
            cp = pltpu.make_async_copy(zero_s, _tile_at(xs_hbm, bk, tmb * pcs), sem)
            cp.start()
            cp.wait()
            return carry

        lax.fori_loop(nu_ref[0], xs_hbm.shape[0] // (tmb * pcs), block, 0)


def _dispatch(hp, plan, n):
    pcs = hp.shape[0] // n
    tmb = plan['tmb']
    n_tiles = plan['n_tiles']
    tm = n // n_tiles
    grid_spec = pltpu.PrefetchScalarGridSpec(
        num_scalar_prefetch=6,
        grid=(n_tiles,),
        in_specs=[pl.BlockSpec((1, 1, TOP_K * tm), lambda i, *_: (i, 0, 0), memory_space=pltpu.SMEM),
                  pl.BlockSpec((tm * pcs, LANES), lambda i, *_: (i, 0))],
        out_specs=pl.BlockSpec(memory_space=pl.ANY),
        scratch_shapes=[pltpu.VMEM((2, TOP_K * tm * pcs, LANES), I32),
                        pltpu.VMEM((tmb * pcs, LANES), I32),
                        pltpu.SemaphoreType.DMA((2,)), pltpu.SemaphoreType.DMA(())],
    )
    return pl.pallas_call(
        functools.partial(_dispatch_kernel, tmb=tmb, pcs=pcs),
        grid_spec=grid_spec,
        out_shape=jax.ShapeDtypeStruct((plan['n_blocks'] * tmb * pcs, LANES), I32),
        compiler_params=_params("arbitrary"),
        name="moe_dispatch",
    )(plan['seg_cnt'], plan['seg_off'], plan['seg_dst'], plan['pad_start'], plan['pad_len'], plan['n_used'],
      plan['lpos'], hp)


def _moe_kernel(be_ref, nu_ref, x_ref, wgu_ref, bgu_ref, wdn_ref, bdn_ref, y_ref, wgu_s, wdn_s, *, ff):
    i = pl.program_id(0)
    used = i < nu_ref[0]

    @pl.when(used)
    def _():
        e_new = jnp.logical_or(i == 0, be_ref[i] != be_ref[jnp.maximum(i - 1, 0)])

        @pl.when(e_new)
        def _():
            wgu_s[...] = wgu_ref[0, 0].astype(BF16)
            wdn_s[...] = wdn_ref[0, 0].astype(BF16)

        half = wgu_s.shape[0] // 2
        pcs = half // LANES
        tmb = x_ref.shape[0] // pcs
        x_lo, x_hi = _unpack_pair(_load_tiles(x_ref, 0, tmb, pcs))
        gu = (_dot(x_lo.astype(BF16), wgu_s[:half, :]) + _dot(x_hi.astype(BF16), wgu_s[half:, :])
              + bgu_ref[0, 0])
        gt = jnp.minimum(gu[:, :ff], SWIGLU_LIMIT)
        up = jnp.clip(gu[:, ff:], -SWIGLU_LIMIT, SWIGLU_LIMIT)
        act = ((up + 1.0) * gt * jax.nn.sigmoid(SWIGLU_ALPHA * gt)).astype(BF16)
        bdn = bdn_ref[0, 0]
        y_lo = _dot(act, wdn_s[:, :half]) + bdn[:, :half]
        y_hi = _dot(act, wdn_s[:, half:]) + bdn[:, half:]
        _store_tiles(y_ref, 0, _pack_pair(y_lo, y_hi))

    @pl.when(jnp.logical_not(used))
    def _():
        y_ref[...] = jnp.zeros(y_ref.shape, I32)


def _moe_experts(xs, plan, w_gu, b_gu, w_dn, b_dn, layer):
    depth, n_exp, d, ff2 = w_gu.shape
    ff = ff2 // 2
    tmb = plan['tmb']
    n_blocks = plan['n_blocks']
    rows = xs.shape[0] // n_blocks
    wmap = lambda i, be, nu: (layer, be[i], 0, 0)
    grid_spec = pltpu.PrefetchScalarGridSpec(
        num_scalar_prefetch=2,
        grid=(n_blocks,),
        in_specs=[pl.BlockSpec((rows, LANES), lambda i, be, nu: (i, 0)),
                  pl.BlockSpec((1, 1, d, ff2), wmap),
                  pl.BlockSpec((1, 1, 1, ff2), wmap),
                  pl.BlockSpec((1, 1, ff, d), wmap),
                  pl.BlockSpec((1, 1, 1, d), wmap)],
        out_specs=pl.BlockSpec((rows, LANES), lambda i, be, nu: (i, 0)),
        scratch_shapes=[pltpu.VMEM((d, ff2), BF16), pltpu.VMEM((ff, d), BF16)],
    )
    return pl.pallas_call(
        functools.partial(_moe_kernel, ff=ff),
        grid_spec=grid_spec,
        out_shape=jax.ShapeDtypeStruct(xs.shape, I32),
        compiler_params=_params("arbitrary"),
        name="moe_experts",
    )(plan['be'], plan['n_used'], xs, w_gu, b_gu.reshape(depth, n_exp, 1, ff2),
      w_dn, b_dn.reshape(depth, n_exp, 1, d))


def _combine_kernel(sc_ref, so_ref, sd_ref, lpos_ref, x_ref, gate_ref, g_ref, ys_hbm, o_ref,
                    ybuf, gbuf, sems, *, n_exp):
    i = pl.program_id(0)
    tm, d = x_ref.shape
    half = d // 2
    pcs = half // LANES
    slot = i % 2

    def fetch(t, s):
        def copy(loc, blk):
            pltpu.make_async_copy(ys_hbm.at[blk], ybuf.at[s, loc], sems.at[s]).start()

        _segments(t, sc_ref, so_ref, sd_ref, n_exp, pcs, copy)

    @pl.when(i == 0)
    def _():
        fetch(i, slot)

    @pl.when(i + 1 < pl.num_programs(0))
    def _():
        fetch(i + 1, 1 - slot)

    pltpu.make_async_copy(ys_hbm.at[pl.ds(0, ybuf.shape[1])], ybuf.at[slot], sems.at[slot]).wait()

    def row(r, carry):
        for kk in range(TOP_K):
            p = lpos_ref[0, 0, TOP_K * r + kk]
            gbuf[kk, pl.ds(pl.multiple_of(r * pcs, pcs), pcs), :] = (
                ybuf[slot, pl.ds(pl.multiple_of(p * pcs, pcs), pcs), :])
        return carry

    lax.fori_loop(0, tm, row, 0, unroll=8)

    gt = gate_ref[...]
    lo = hi = None
    for kk in range(TOP_K):
        w = gt[:, kk:kk + 1]
        y_lo, y_hi = _unpack_pair(_load_tiles(gbuf.at[kk], 0, tm, pcs))
        lo = w * y_lo if lo is None else lo + w * y_lo
        hi = w * y_hi if hi is None else hi + w * y_hi
    g = g_ref[...]
    o_ref[:, :half] = x_ref[:, :half] + _gate_rows(lo, g[:, :half])
    o_ref[:, half:] = x_ref[:, half:] + _gate_rows(hi, g[:, half:])


def _combine(x, ys, gates, g2, plan):
    n, d = x.shape
    nb = g2.shape[0]
    pcs = d // 2 // LANES
    nt = plan['n_tiles']
    tm = n // nt
    grid_spec = pltpu.PrefetchScalarGridSpec(
        num_scalar_prefetch=3,
        grid=(nt,),
        in_specs=[pl.BlockSpec((1, 1, TOP_K * tm), lambda i, *_: (i, 0, 0), memory_space=pltpu.SMEM),
                  pl.BlockSpec((tm, d), lambda i, *_: (i, 0)),
                  pl.BlockSpec((tm, LANES), lambda i, *_: (i, 0)),
                  pl.BlockSpec((nb, d), lambda i, *_: (0, 0)),
                  pl.BlockSpec(memory_space=pl.ANY)],
        out_specs=pl.BlockSpec((tm, d), lambda i, *_: (i, 0)),
        scratch_shapes=[pltpu.VMEM((2, TOP_K * tm * pcs, LANES), I32),
                        pltpu.VMEM((TOP_K, tm * pcs, LANES), I32),
                        pltpu.SemaphoreType.DMA((2,))],
    )
    return pl.pallas_call(
        functools.partial(_combine_kernel, n_exp=plan['n_exp']),
        grid_spec=grid_spec,
        out_shape=jax.ShapeDtypeStruct((n, d), F32),
        compiler_params=_params("arbitrary"),
        name="moe_combine",
    )(plan['seg_cnt'], plan['seg_off'], plan['seg_dst'], plan['lpos'], x, gates, g2, ys)


def _final_kernel(x_ref, g_ref, o_ref):
    x = x_ref[...]
    ms = jnp.mean(x * x, axis=-1, keepdims=True)
    o_ref[...] = (x * lax.rsqrt(ms + EPS)) * g_ref[...]


def _final_norm(x, g):
    n, d = x.shape
    tm = _tile(n, 512, 8)
    return pl.pallas_call(
        _final_kernel,
        grid=(n // tm,),
        in_specs=[pl.BlockSpec((tm, d), lambda i: (i, 0)), pl.BlockSpec((1, d), lambda i: (0, 0))],
        out_specs=pl.BlockSpec((tm, d), lambda i: (i, 0)),
        out_shape=jax.ShapeDtypeStruct((n, d), F32),
        compiler_params=_params("parallel"),
        name="final_norm",
    )(x, g)


def kernel(x, c, ada_w, ada_b, norm_mix, norm_ffn, norm_final, s5_lambda_re, s5_lambda_im, s5_log_dt,
           s5_b_re, s5_b_im, s5_c_re, s5_c_im, s5_d, s5_w_glu, hg_w_in, hg_lb_raw, hg_norm, hg_w_out,
           router_w, router_b, moe_w_gate_up, moe_b_gate_up, moe_w_down, moe_b_down):
    bsz, seq, d = x.shape
    depth = ada_w.shape[0]
    n = bsz * seq
    dv = hg_norm.shape[-1]
    heads = d // dv
    fdim = (hg_w_in.shape[-1] - 2 * d) // 2
    dk = fdim // heads
    assert dk == LANES and dv == LANES, "one HGRN2 head per 128-lane tile"
    tmb = _tile(n * TOP_K, MOE_ROWS, SUBLANES)

    lb_p = jax.nn.softmax(hg_lb_raw.astype(F32), axis=0)
    lb_all = jnp.cumsum(lb_p, axis=0) - lb_p[0]
    mod = _ada_mod(c.astype(F32), ada_w, ada_b)

    xs = x.astype(F32).transpose(1, 0, 2).reshape(n, d)
    row = lambda v: v.astype(F32).reshape(1, -1)
    for i in range(depth):
        sh1, sc1, g1, sh2, sc2, g2 = [mod[i, :, k * d:(k + 1) * d] for k in range(N_ADA)]
        j = i // 2
        if i % 2 == 0:
            bmat, cmat, lre, lim = _s5_prep(s5_lambda_re[j], s5_lambda_im[j], s5_log_dt[j],
                                            s5_b_re[j], s5_b_im[j], s5_c_re[j], s5_c_im[j])
            z = _s5_scan(xs, row(norm_mix[i]), sc1, sh1, bmat, cmat, lre, lim, row(s5_d[j]), bsz)
            xs = _mm_res(z, s5_w_glu[j].astype(BF16), xs, g1, glu=True)
        else:
            lb = lb_all[i]
            proj = _hg_proj(xs, row(norm_mix[i]), sc1, sh1, hg_w_in[j].astype(BF16),
                            row(jnp.log(lb)), row(jnp.log1p(-lb)), fdim)
            o = _hg_rec(proj, row(hg_norm[j]), bsz, heads, dk)
            xs = _mm_res(o, hg_w_out[j].astype(BF16), xs, g1, glu=False)
        hp, top_idx, gates, rank, counts = _router(xs, row(norm_ffn[i]), sc2, sh2, router_w[i], router_b[i])
        plan = _moe_plan(top_idx[:, :TOP_K], rank[:, :TOP_K], counts, tmb)
        xd = _dispatch(hp, plan, n)
        ys = _moe_experts(xd, plan, moe_w_gate_up, moe_b_gate_up.astype(F32),
                          moe_w_down, moe_b_down.astype(F32), i)
        xs = _combine(xs, ys, gates, g2, plan)
    out = _final_norm(xs, row(norm_final))
    return out.reshape(seq, bsz, d).transpose(1, 0, 2).astype(x.dtype)
```

```python
import functools

import numpy as np
import jax
import jax.numpy as jnp
from jax import lax
from jax.experimental import pallas as pl
from jax.experimental.pallas import tpu as pltpu

F32 = jnp.float32
BF16 = jnp.bfloat16
I32 = jnp.int32

EPS = 1e-6
N_ADA = 6
TOP_K = 4
S5_RE_MAX = -1e-4
SWIGLU_LIMIT = 7.0
SWIGLU_ALPHA = 1.702
GELU_C0 = 0.7978845608028654
GELU_C1 = 0.044715

LANES = 128
SUBLANES = 8
HG_CHUNK = 64
HG_GROUP = 4
S5_CHUNK = 128
S5_SUB = 4
NORM_ROWS = 32
MOE_ROWS = 512
HI16 = -65536
VMEM_LIMIT = 56 * 1024 * 1024


def _params(*sem):
    return pltpu.CompilerParams(dimension_semantics=sem, vmem_limit_bytes=VMEM_LIMIT)


def _tile(n, pref, align):
    if n <= pref:
        return n
    t = (pref // align) * align
    while t > align and n % t:
        t -= align
    assert n % t == 0, (n, pref, align)
    return t


def _dot(a, b):
    return jnp.dot(a, b, preferred_element_type=F32)


def _dot_nt(a, b):
    return lax.dot_general(a, b, (((1,), (1,)), ((), ())), preferred_element_type=F32)


def _dot_tn(a, b):
    return lax.dot_general(a, b, (((0,), (0,)), ((), ())), preferred_element_type=F32)


def _split(a):
    hi = a.astype(BF16)
    lo = (a - hi.astype(F32)).astype(BF16)
    return hi, lo


def _dot3(a, w):
    ah, al = _split(a)
    wh, wl = _split(w)
    return _dot(ah, wh) + _dot(al, wh) + _dot(ah, wl)


def _norm_mod(x, g, sc, sh):
    ms = jnp.mean(x * x, axis=-1, keepdims=True)
    return ((x * lax.rsqrt(ms + EPS)) * g) * (1.0 + sc) + sh


def _norm_mod_rows(x_ref, mods, emit):
    rows = x_ref.shape[0]
    ch = min(NORM_ROWS, rows)

    def body(i, carry):
        r0 = pl.multiple_of(i * ch, ch)
        emit(r0, _norm_mod(x_ref[pl.ds(r0, ch), :], *mods()))
        return carry

    lax.fori_loop(0, rows // ch, body, 0, unroll=4 if (rows // ch) % 4 == 0 else 1)


def _batch_spec(width, tiles_per_batch):
    return pl.BlockSpec((1, 1, width), lambda i, *_: (i // tiles_per_batch, 0, 0))


def _pack_pair(lo, hi):
    lo_b = lax.bitcast_convert_type(lo.astype(BF16).astype(F32), I32)
    hi_b = lax.bitcast_convert_type(hi.astype(BF16).astype(F32), I32)
    return lax.shift_right_logical(lo_b, 16) | (hi_b & HI16)


def _unpack_pair(p):
    lo = lax.bitcast_convert_type(lax.shift_left(p, 16), F32)
    hi = lax.bitcast_convert_type(p & HI16, F32)
    return lo, hi


def _store_tiles(ref, row0, val):
    rows, width = val.shape
    pcs = width // LANES
    for jj in range(pcs):
        ref[pl.ds(row0 * pcs + jj, rows, stride=pcs), :] = val[:, jj * LANES:(jj + 1) * LANES]


def _load_tiles(ref, row0, rows, pcs):
    return jnp.concatenate([ref[pl.ds(row0 * pcs + jj, rows, stride=pcs), :] for jj in range(pcs)], axis=1)


def _ada_kernel(c_ref, w_ref, b_ref, o_ref):
    c = c_ref[...]
    o_ref[0] = _dot3(c * jax.nn.sigmoid(c), w_ref[0]) + b_ref[0]


def _ada_mod(c, ada_w, ada_b):
    depth, d, nd = ada_w.shape
    nb = c.shape[0]
    tn = _tile(nd, 768, LANES)
    return pl.pallas_call(
        _ada_kernel,
        grid=(depth, nd // tn),
        in_specs=[pl.BlockSpec((nb, d), lambda i, j: (0, 0)),
                  pl.BlockSpec((1, d, tn), lambda i, j: (i, 0, j)),
                  pl.BlockSpec((1, 1, tn), lambda i, j: (i, 0, j))],
        out_specs=pl.BlockSpec((1, nb, tn), lambda i, j: (i, 0, j)),
        out_shape=jax.ShapeDtypeStruct((depth, nb, nd), F32),
        compiler_params=_params("parallel", "parallel"),
        name="ada_mod",
    )(c, ada_w, ada_b.reshape(depth, 1, nd))


def _s5_kernel(x_ref, g_ref, sc_ref, sh_ref, bmat_ref, cmat_ref, lre_ref, lim_ref, d_ref,
               z_ref, h_s, st_s, bu_s, sa_s, z_s, *, tc, nb, hw):
    c = pl.program_id(0)
    j = pl.program_id(1)
    n_j = h_s.shape[0]

    @pl.when(j == 0)
    def _():
        for b in range(nb):
            def emit(r0, h, b=b):
                for jj in range(n_j):
                    h_s[jj, pl.ds(r0 * nb + b, h.shape[0], stride=nb), :] = h[:, jj * LANES:(jj + 1) * LANES]

            _norm_mod_rows(x_ref.at[b], lambda b=b: (g_ref[...], sc_ref[b], sh_ref[b]), emit)

    @pl.when(c == 0)
    def _():
        st_s[j] = jnp.zeros(st_s.shape[1:], F32)

    n_sub = bu_s.shape[0]
    ts = tc // n_sub
    rs = ts * nb
    bmat = bmat_ref[0]
    cmat = cmat_ref[0]
    for k in range(n_sub):
        bu_s[k] = _dot(h_s[j, k * rs:(k + 1) * rs, :].astype(BF16), bmat)
    lre = jnp.broadcast_to(lre_ref[0], (nb, hw))
    lim = jnp.broadcast_to(lim_ref[0], (nb, hw))
    st = st_s[j]
    s_re, s_im = st[:, :hw], st[:, hw:]
    for k in range(n_sub):
        for t in range(ts):
            b = bu_s[k, t * nb:(t + 1) * nb, :]
            s_re, s_im = (lre * s_re - lim * s_im + b[:, :hw],
                          lre * s_im + lim * s_re + b[:, hw:])
            sa_s[k, t * nb:(t + 1) * nb, :hw] = s_re
            sa_s[k, t * nb:(t + 1) * nb, hw:] = s_im
        y = _dot(sa_s[k].astype(BF16), cmat) + d_ref[...] * h_s[j, k * rs:(k + 1) * rs, :]
        z = 0.5 * y * (1.0 + jnp.tanh(GELU_C0 * (y + GELU_C1 * (y * y * y))))
        z_s[k * rs:(k + 1) * rs, :] = z
    st_s[j, :, :hw] = s_re
    st_s[j, :, hw:] = s_im
    for b in range(nb):
        z_ref[b] = z_s[pl.ds(b, tc, stride=nb), :].astype(BF16)


def _s5_prep(lam_re, lam_im, log_dt, b_re, b_im, c_re, c_im):
    n_g, n_p = lam_re.shape
    n_h = b_re.shape[-1]
    gpt = LANES // n_h
    n_j = n_g // gpt
    lam = lax.complex(jnp.minimum(lam_re.astype(F32), S5_RE_MAX), lam_im.astype(F32))
    dt = jnp.exp(log_dt.astype(F32))[:, None]
    lam_bar = jnp.exp(lam * dt)
    b_bar = ((lam_bar - 1.0) / lam)[:, :, None] * lax.complex(b_re.astype(F32), b_im.astype(F32))
    eye = jnp.eye(gpt, dtype=F32)
    bb = b_bar.reshape(n_j, gpt, n_p, n_h)
    bre = jnp.einsum('jkph,gk->jghkp', bb.real, eye)
    bim = jnp.einsum('jkph,gk->jghkp', bb.imag, eye)
    bmat = jnp.stack([bre, bim], axis=3).reshape(n_j, LANES, 2 * gpt * n_p)
    cre = jnp.einsum('jghp,kg->jkpgh', c_re.astype(F32).reshape(n_j, gpt, n_h, n_p), eye)
    cim = jnp.einsum('jghp,kg->jkpgh', c_im.astype(F32).reshape(n_j, gpt, n_h, n_p), eye)
    cmat = jnp.stack([cre, -cim], axis=1).reshape(n_j, 2 * gpt * n_p, LANES)
    lre = lam_bar.real.reshape(n_j, 1, gpt * n_p)
    lim = lam_bar.imag.reshape(n_j, 1, gpt * n_p)
    return bmat.astype(BF16), cmat.astype(BF16), lre, lim


def _s5_scan(x, g, sc, sh, bmat, cmat, lre, lim, d_skip):
    nb, seq, d = x.shape
    n_j = d // LANES
    hw = lre.shape[-1]
    tc = _tile(seq, S5_CHUNK, 8)
    rows = tc * nb
    n_sub = S5_SUB if tc % S5_SUB == 0 else 1
    kern = functools.partial(_s5_kernel, tc=tc, nb=nb, hw=hw)
    return pl.pallas_call(
        kern,
        grid=(seq // tc, n_j),
        in_specs=[pl.BlockSpec((nb, tc, d), lambda c, j: (0, c, 0)),
                  pl.BlockSpec((1, d), lambda c, j: (0, 0)),
                  pl.BlockSpec((nb, 1, d), lambda c, j: (0, 0, 0)),
                  pl.BlockSpec((nb, 1, d), lambda c, j: (0, 0, 0)),
                  pl.BlockSpec((1, LANES, 2 * hw), lambda c, j: (j, 0, 0)),
                  pl.BlockSpec((1, 2 * hw, LANES), lambda c, j: (j, 0, 0)),
                  pl.BlockSpec((1, 1, hw), lambda c, j: (j, 0, 0)),
                  pl.BlockSpec((1, 1, hw), lambda c, j: (j, 0, 0)),
                  pl.BlockSpec((1, LANES), lambda c, j: (0, j))],
        out_specs=pl.BlockSpec((nb, tc, LANES), lambda c, j: (0, c, j)),
        out_shape=jax.ShapeDtypeStruct((nb, seq, d), BF16),
        scratch_shapes=[pltpu.VMEM((n_j, rows, LANES), F32),
                        pltpu.VMEM((n_j, nb, 2 * hw), F32),
                        pltpu.VMEM((n_sub, rows // n_sub, 2 * hw), F32),
                        pltpu.VMEM((n_sub, rows // n_sub, 2 * hw), F32),
                        pltpu.VMEM((rows, LANES), F32)],
        compiler_params=_params("arbitrary", "arbitrary"),
        name="s5_scan",
    )(x, g, sc, sh, bmat, cmat, lre, lim, d_skip)


def _mm_res_kernel(a_ref, w_ref, x_ref, g_ref, o_ref, *, glu, tn):
    a = a_ref[...]
    d = o_ref.shape[1]
    for c0 in range(0, d, tn):
        p = _dot(a, w_ref[:, c0:c0 + tn])
        if glu:
            p = p * jax.nn.sigmoid(_dot(a, w_ref[:, d + c0:d + c0 + tn]))
        o_ref[:, c0:c0 + tn] = x_ref[:, c0:c0 + tn] + p * g_ref[0, :, c0:c0 + tn]


def _mm_res(a, w, x, gate, glu, seq):
    n, kdim = a.shape
    d = x.shape[1]
    tm = _tile(seq, 512, 16)
    tn = _tile(d, 512, LANES)
    return pl.pallas_call(
        functools.partial(_mm_res_kernel, glu=glu, tn=tn),
        grid=(n // tm,),
        in_specs=[pl.BlockSpec((tm, kdim), lambda i: (i, 0)),
                  pl.BlockSpec(w.shape, lambda i: (0, 0), pipeline_mode=pl.Buffered(1)),
                  pl.BlockSpec((tm, d), lambda i: (i, 0)),
                  _batch_spec(d, seq // tm)],
        out_specs=pl.BlockSpec((tm, d), lambda i: (i, 0)),
        out_shape=jax.ShapeDtypeStruct((n, d), F32),
        compiler_params=_params("parallel"),
        name="glu_res" if glu else "proj_res",
    )(a, w, x, gate)


def _hg_proj_kernel(x_ref, g_ref, sc_ref, sh_ref, w_ref, la_ref, l1_ref, o_ref, h_s, *, f_lo, f_hi):
    j = pl.program_id(1)

    @pl.when(j == 0)
    def _():
        def emit(r0, h):
            h_s[pl.ds(r0, h.shape[0]), :] = h.astype(BF16)

        _norm_mod_rows(x_ref, lambda: (g_ref[...], sc_ref[0], sh_ref[0]), emit)

    p = _dot(h_s[...], w_ref[...])
    is_f = jnp.logical_and(j >= f_lo, j < f_hi)

    @pl.when(is_f)
    def _():
        ls = jnp.minimum(p, 0.0) - jnp.log(1.0 + jnp.exp(-jnp.abs(p)))
        bt = l1_ref[...] + ls
        la = la_ref[...]
        o_ref[...] = jnp.maximum(la, bt) + jnp.log(1.0 + jnp.exp(-jnp.abs(la - bt)))

    @pl.when(jnp.logical_not(is_f))
    def _():
        o_ref[...] = p


def _hg_proj(x, g, sc, sh, w_in, log_lb, log1m_lb, fdim, seq):
    n, d = x.shape
    cols = w_in.shape[1]
    tm = _tile(seq, 1024, 16)
    tn = _tile(fdim, 512, LANES)
    f_lo = fdim // tn
    f_hi = 2 * fdim // tn
    tpb = seq // tm

    def lb_map(i, j):
        return (0, jnp.clip(j - f_lo, 0, f_lo - 1))

    return pl.pallas_call(
        functools.partial(_hg_proj_kernel, f_lo=f_lo, f_hi=f_hi),
        grid=(n // tm, cols // tn),
        in_specs=[pl.BlockSpec((tm, d), lambda i, j: (i, 0)),
                  pl.BlockSpec((1, d), lambda i, j: (0, 0)),
                  pl.BlockSpec((1, 1, d), lambda i, j: (i // tpb, 0, 0)),
                  pl.BlockSpec((1, 1, d), lambda i, j: (i // tpb, 0, 0)),
                  pl.BlockSpec((d, tn), lambda i, j: (0, j)),
                  pl.BlockSpec((1, tn), lb_map),
                  pl.BlockSpec((1, tn), lb_map)],
        out_specs=pl.BlockSpec((tm, tn), lambda i, j: (i, j)),
        out_shape=jax.ShapeDtypeStruct((n, cols), F32),
        scratch_shapes=[pltpu.VMEM((tm, d), BF16)],
        compiler_params=_params("parallel", "arbitrary"),
        name="hg_proj",
    )(x, g, sc, sh, w_in, log_lb, log1m_lb)


def _hg_levels(ch):
    levels = []
    m = ch // 2
    while m >= 1:
        levels.append(m)
        m //= 2
    return levels


def _hg_masks(ch):
    levels = _hg_levels(ch)
    mk = np.zeros((len(levels) + 1, ch, ch), np.float32)
    for li, m in enumerate(levels):
        for t in range(ch):
            mid = (t // (2 * m)) * 2 * m + m - 1
            if t > mid:
                mk[li, t, mid - m + 1:mid + 1] = 1.0
    mk[-1] = np.eye(ch, dtype=np.float32)
    return mk


def _cumsum_rows(x, t_idx):
    sh = 1
    while sh < x.shape[0]:
        x = x + jnp.where(t_idx >= sh, pltpu.roll(x, sh, 0), 0.0)
        sh *= 2
    return x


def _level_ref(bc, m, sub3):
    ch, dk = bc.shape
    if 2 * m >= SUBLANES:
        bp = bc.reshape(ch // (2 * m), 2 * m, dk)
        return jnp.broadcast_to(bp[:, m - 1:m, :], bp.shape).reshape(ch, dk)
    b3 = bc.reshape(ch // SUBLANES, SUBLANES, dk)
    r = None
    for start in range(0, SUBLANES, 2 * m):
        cand = jnp.broadcast_to(b3[:, start + m - 1:start + m, :], b3.shape)
        r = cand if r is None else jnp.where(sub3 >= start, cand, r)
    return r.reshape(ch, dk)


def _hg_rec_kernel(q_ref, lf_ref, v_ref, gt_ref, mask_ref, gn_ref, o_ref, st_s, *, nb, ch):
    c = pl.program_id(1)

    @pl.when(c == 0)
    def _():
        st_s[...] = jnp.zeros(st_s.shape, F32)

    dk = q_ref.shape[-1]
    levels = _hg_levels(ch)
    n_lv = len(levels)
    gn = gn_ref[...]
    t_idx = lax.broadcasted_iota(I32, (ch, dk), 0)
    sub3 = lax.broadcasted_iota(I32, (ch // SUBLANES, SUBLANES, dk), 1)
    masks = [mask_ref[li] > 0.5 for li in range(n_lv + 1)]

    for b0 in range(0, nb, HG_GROUP):
        grp = range(b0, min(b0 + HG_GROUP, nb))
        qk, qms, kms, q_in, k_out, decay = {}, {}, {}, {}, {}, {}
        for b in grp:
            q = q_ref[b]
            lf = lf_ref[b]
            bc = _cumsum_rows(lf, t_idx)
            qb = q.astype(BF16)
            kb = (1.0 - jnp.exp(lf)).astype(BF16)
            qk[b] = (qb, kb)
            qms[b], kms[b] = [], []
            for m in levels:
                e = jnp.exp(-jnp.abs(bc - _level_ref(bc, m, sub3))).astype(BF16)
                qms[b].append(qb * e)
                kms[b].append(kb * e)
            b_end = bc[ch - 1:ch, :]
            q_in[b] = qb * jnp.exp(bc).astype(BF16)
            k_out[b] = kb * jnp.exp(b_end - bc).astype(BF16)
            decay[b] = jnp.exp(b_end)
        scores = {}
        for b in grp:
            s = jnp.where(masks[n_lv], _dot_nt(*qk[b]), 0.0)
            for li in range(n_lv):
                s = jnp.where(masks[li], _dot_nt(qms[b][li], kms[b][li]), s)
            scores[b] = s.astype(BF16)
        outs = {}
        for b in grp:
            v = v_ref[b].astype(BF16)
            st = st_s[b]
            outs[b] = _dot(scores[b], v) + _dot_nt(q_in[b], st.astype(BF16))
            st_s[b] = decay[b] * st + _dot_tn(v, k_out[b])
        for b in grp:
            o = outs[b]
            gate = gt_ref[b]
            o = o * lax.rsqrt(jnp.mean(o * o, axis=-1, keepdims=True) + EPS) * gn
            o_ref[b] = (o * (gate * jax.nn.sigmoid(gate))).astype(BF16)


def _hg_rec(proj, g_norm, heads, dk):
    nb, seq, _ = proj.shape
    ch = _tile(seq, HG_CHUNK, 16)
    mk = _hg_masks(ch)
    blk = lambda off: pl.BlockSpec((nb, ch, dk), lambda h, c: (0, c, off + h))
    return pl.pallas_call(
        functools.partial(_hg_rec_kernel, nb=nb, ch=ch),
        grid=(heads, seq // ch),
        in_specs=[blk(0), blk(heads), blk(2 * heads), blk(3 * heads),
                  pl.BlockSpec(mk.shape, lambda h, c: (0, 0, 0)),
                  pl.BlockSpec((1, dk), lambda h, c: (0, 0))],
        out_specs=pl.BlockSpec((nb, ch, dk), lambda h, c: (0, c, h)),
        out_shape=jax.ShapeDtypeStruct((nb, seq, heads * dk), BF16),
        scratch_shapes=[pltpu.VMEM((nb, dk, dk), F32)],
        compiler_params=_params("parallel", "arbitrary"),
        name="hg_rec",
    )(proj, proj, proj, proj, jnp.asarray(mk, F32), g_norm)


def _router_kernel(x_ref, g_ref, sc_ref, sh_ref, wh_ref, wl_ref, br_ref, tri_ref,
                   hp_ref, idx_ref, gate_ref, rank_ref, cnt_ref):
    h = _norm_mod(x_ref[...], g_ref[...], sc_ref[0], sh_ref[0])
    half = h.shape[1] // 2
    _store_tiles(hp_ref, 0, _pack_pair(h[:, :half], h[:, half:]))
    hh, hl = _split(h)
    wh = wh_ref[...]
    vals = _dot(hh, wh) + _dot(hl, wh) + _dot(hh, wl_ref[...]) + br_ref[...]
    tm, n_exp = vals.shape
    lane = lax.broadcasted_iota(I32, (tm, n_exp), 1)
    tops, idxs, hots = [], [], []
    for _ in range(TOP_K):
        m = jnp.max(vals, axis=-1, keepdims=True)
        i = jnp.min(jnp.where(vals == m, lane, n_exp), axis=-1, keepdims=True)
        tops.append(m)
        idxs.append(i)
        hots.append(lane == i)
        vals = jnp.where(hots[-1], -jnp.inf, vals)
    es = [jnp.exp(t - tops[0]) for t in tops]
    den = es[0] + es[1] + es[2] + es[3]

    sel = [jnp.where(hm, 1.0, 0.0) for hm in hots]
    multi = sel[0] + sel[1] + sel[2] + sel[3]
    base = _dot(tri_ref[...], multi.astype(BF16))
    cnt_ref[0] = jnp.sum(multi, axis=0, keepdims=True)

    lane_o = lax.broadcasted_iota(I32, idx_ref.shape, 1)
    io = jnp.zeros(idx_ref.shape, I32)
    ro = jnp.zeros(idx_ref.shape, I32)
    go = jnp.zeros(gate_ref.shape, F32)
    for kk in range(TOP_K):
        rk = jnp.sum(sel[kk] * base, axis=-1, keepdims=True).astype(I32)
        io = jnp.where(lane_o == kk, idxs[kk], io)
        ro = jnp.where(lane_o == kk, rk, ro)
        go = jnp.where(lane_o == kk, es[kk] / den, go)
    idx_ref[...] = io
    rank_ref[...] = ro
    gate_ref[...] = go


def _router(x, g, sc, sh, w_r, b_r, seq):
    n, d = x.shape
    n_exp = w_r.shape[1]
    tm = _tile(seq, 512, 16)
    bspec = _batch_spec(d, seq // tm)
    pcs = d // 2 // LANES
    wh, wl = _split(w_r.astype(F32))
    tri = jnp.asarray(np.tril(np.ones((tm, tm), np.float32), -1), BF16)
    row = lambda w: pl.BlockSpec((tm, w), lambda i: (i, 0))
    full = lambda a: pl.BlockSpec(a.shape, lambda i: (0, 0))
    b2 = b_r.astype(F32).reshape(1, n_exp)
    return pl.pallas_call(
        _router_kernel,
        grid=(n // tm,),
        in_specs=[row(d), full(g), bspec, bspec, full(wh), full(wl), full(b2), full(tri)],
        out_specs=[pl.BlockSpec((tm * pcs, LANES), lambda i: (i, 0)), row(LANES), row(LANES), row(LANES),
                   pl.BlockSpec((1, 1, n_exp), lambda i: (i, 0, 0))],
        out_shape=[jax.ShapeDtypeStruct((n * pcs, LANES), I32),
                   jax.ShapeDtypeStruct((n, LANES), I32),
                   jax.ShapeDtypeStruct((n, LANES), F32),
                   jax.ShapeDtypeStruct((n, LANES), I32),
                   jax.ShapeDtypeStruct((n // tm, 1, n_exp), F32)],
        compiler_params=_params("parallel"),
        name="router",
    )(x, g, sc, sh, wh, wl, b2, tri)


def _moe_plan(top_idx, lrank, tile_counts, tmb):
    n = top_idx.shape[0]
    n_tiles, _, n_exp = tile_counts.shape
    tc = tile_counts.reshape(n_tiles, n_exp).astype(I32)
    counts = jnp.sum(tc, axis=0)
    nblk = (counts + tmb - 1) // tmb
    blk_end = jnp.cumsum(nblk)
    blk_start = blk_end - nblk
    n_used = blk_end[-1]
    n_blocks = n * TOP_K // tmb + n_exp
    blk = jnp.arange(n_blocks, dtype=I32)
    be = jnp.minimum(jnp.sum((blk[:, None] >= blk_end[None, :]).astype(I32), axis=1), n_exp - 1)
    last = jnp.sum(jnp.where(blk == n_used - 1, be, 0))
    be = jnp.where(blk < n_used, be, last).astype(I32)
    base = blk_start * tmb
    seg_dst = base[None, :] + jnp.cumsum(tc, axis=0) - tc
    seg_off = jnp.cumsum(tc, axis=1) - tc
    tm = n // n_tiles
    hot = top_idx.reshape(n_tiles, tm, TOP_K, 1) == jnp.arange(n_exp, dtype=I32)
    lpos = jnp.sum(jnp.where(hot, seg_off[:, None, None, :], 0), axis=-1) + lrank.reshape(n_tiles, tm, TOP_K)
    pad_start = base + counts
    pad_len = blk_end * tmb - pad_start
    flat = lambda a: a.reshape(-1).astype(I32)
    return dict(be=be, n_used=n_used.reshape(1).astype(I32),
                lpos=lpos.reshape(n_tiles, 1, tm * TOP_K).astype(I32),
                seg_cnt=flat(tc), seg_off=flat(seg_off), seg_dst=flat(seg_dst),
                pad_start=pad_start.astype(I32), pad_len=pad_len.astype(I32),
                n_blocks=n_blocks, tmb=tmb, n_tiles=n_tiles, n_exp=n_exp)


def _tile_at(ref, row, pcs):
    return ref.at[pl.ds(pl.multiple_of(row * pcs, pcs), pcs)]


def _segments(i, cnt_ref, off_ref, dst_ref, n_exp, pcs, copy):
    def seg(e, carry):
        cnt = cnt_ref[i * n_exp + e]

        @pl.when(cnt > 0)
        def _():
            size = pl.multiple_of(cnt * pcs, pcs)
            copy(pl.ds(pl.multiple_of(off_ref[i * n_exp + e] * pcs, pcs), size),
                 pl.ds(pl.multiple_of(dst_ref[i * n_exp + e] * pcs, pcs), size))

        return carry

    lax.fori_loop(0, n_exp, seg, 0)


def _dispatch_kernel(sc_ref, so_ref, sd_ref, ps_ref, pn_ref, nu_ref, lpos_ref, hp_ref, xs_hbm,
                     sbuf, zero_s, sems, sem, *, tmb, pcs):
    i = pl.program_id(0)
    last = pl.num_programs(0) - 1
    tm = hp_ref.shape[0] // pcs
    slot = i % 2

    def drain(s):
        pltpu.make_async_copy(sbuf.at[s], xs_hbm.at[pl.ds(0, sbuf.shape[1])], sems.at[s]).wait()

    @pl.when(i >= 2)
    def _():
        drain(slot)

    def row(r, carry):
        tile = hp_ref[pl.ds(pl.multiple_of(r * pcs, pcs), pcs), :]
        for kk in range(TOP_K):
            p = lpos_ref[0, 0, TOP_K * r + kk]
            sbuf[slot, pl.ds(pl.multiple_of(p * pcs, pcs), pcs), :] = tile
        return carry

    lax.fori_loop(0, tm, row, 0, unroll=8)

    def copy(src, dst):
        pltpu.make_async_copy(sbuf.at[slot, src], xs_hbm.at[dst], sems.at[slot]).start()

    _segments(i, sc_ref, so_ref, sd_ref, ps_ref.shape[0], pcs, copy)

    @pl.when(i == last)
    def _():
        drain(slot)

        @pl.when(i >= 1)
        def _():
            drain(1 - slot)

        zero_s[...] = jnp.zeros(zero_s.shape, I32)

        size = tmb // 2
        while size >= 1:
            def chunk(e, size=size):
                cnt = pn_ref[e]
                before = cnt - cnt % (2 * size)
                dst = pl.ds(pl.multiple_of((ps_ref[e] + before) * pcs, pcs), size * pcs)
                return (cnt // size) % 2 == 1, pltpu.make_async_copy(
                    zero_s.at[pl.ds(0, size * pcs)], xs_hbm.at[dst], sem)

            def fill(e, carry):
                take, cp = chunk(e)

                @pl.when(take)
                def _():
                    cp.start()

                return carry

            def fill_done(e, carry):
                take, cp = chunk(e)

                @pl.when(take)
                def _():
                    cp.wait()

                return carry

            lax.fori_loop(0, ps_ref.shape[0], fill, 0)
            lax.fori_loop(0, ps_ref.shape[0], fill_done, 0)
            size //= 2

        def block(bk, carry):
            cp = pltpu.make_async_copy(zero_s, _tile_at(xs_hbm, bk, tmb * pcs), sem)
            cp.start()
            cp.wait()
            return carry

        lax.fori_loop(nu_ref[0], xs_hbm.shape[0] // (tmb * pcs), block, 0)


def _dispatch(hp, plan, n):
    pcs = hp.shape[0] // n
    tmb = plan['tmb']
    n_tiles = plan['n_tiles']
    tm = n // n_tiles
    grid_spec = pltpu.PrefetchScalarGridSpec(
        num_scalar_prefetch=6,
        grid=(n_tiles,),
        in_specs=[pl.BlockSpec((1, 1, TOP_K * tm), lambda i, *_: (i, 0, 0), memory_space=pltpu.SMEM),
                  pl.BlockSpec((tm * pcs, LANES), lambda i, *_: (i, 0))],
        out_specs=pl.BlockSpec(memory_space=pl.ANY),
        scratch_shapes=[pltpu.VMEM((2, TOP_K * tm * pcs, LANES), I32),
                        pltpu.VMEM((tmb * pcs, LANES), I32),
                        pltpu.SemaphoreType.DMA((2,)), pltpu.SemaphoreType.DMA(())],
    )
    return pl.pallas_call(
        functools.partial(_dispatch_kernel, tmb=tmb, pcs=pcs),
        grid_spec=grid_spec,
        out_shape=jax.ShapeDtypeStruct((plan['n_blocks'] * tmb * pcs, LANES), I32),
        compiler_params=_params("arbitrary"),
        name="moe_dispatch",
    )(plan['seg_cnt'], plan['seg_off'], plan['seg_dst'], plan['pad_start'], plan['pad_len'], plan['n_used'],
      plan['lpos'], hp)


def _moe_kernel(be_ref, nu_ref, x_ref, wgu_ref, bgu_ref, wdn_ref, bdn_ref, y_ref, wgu_s, wdn_s, *, ff):
    i = pl.program_id(0)
    used = i < nu_ref[0]

    @pl.when(used)
    def _():
        e_new = jnp.logical_or(i == 0, be_ref[i] != be_ref[jnp.maximum(i - 1, 0)])

        @pl.when(e_new)
        def _():
            wgu_s[...] = wgu_ref[0, 0].astype(BF16)
            wdn_s[...] = wdn_ref[0, 0].astype(BF16)

        half = wgu_s.shape[0] // 2
        pcs = half // LANES
        tmb = x_ref.shape[0] // pcs
        x_lo, x_hi = _unpack_pair(_load_tiles(x_ref, 0, tmb, pcs))
        gu = (_dot(x_lo.astype(BF16), wgu_s[:half, :]) + _dot(x_hi.astype(BF16), wgu_s[half:, :])
              + bgu_ref[0, 0])
        gt = jnp.minimum(gu[:, :ff], SWIGLU_LIMIT)
        up = jnp.clip(gu[:, ff:], -SWIGLU_LIMIT, SWIGLU_LIMIT)
        act = ((up + 1.0) * gt * jax.nn.sigmoid(SWIGLU_ALPHA * gt)).astype(BF16)
        bdn = bdn_ref[0, 0]
        y_lo = _dot(act, wdn_s[:, :half]) + bdn[:, :half]
        y_hi = _dot(act, wdn_s[:, half:]) + bdn[:, half:]
        _store_tiles(y_ref, 0, _pack_pair(y_lo, y_hi))

    @pl.when(jnp.logical_not(used))
    def _():
        y_ref[...] = jnp.zeros(y_ref.shape, I32)


def _moe_experts(xs, plan, w_gu, b_gu, w_dn, b_dn, layer):
    depth, n_exp, d, ff2 = w_gu.shape
    ff = ff2 // 2
    tmb = plan['tmb']
    n_blocks = plan['n_blocks']
    rows = xs.shape[0] // n_blocks
    wmap = lambda i, be, nu: (layer, be[i], 0, 0)
    grid_spec = pltpu.PrefetchScalarGridSpec(
        num_scalar_prefetch=2,
        grid=(n_blocks,),
        in_specs=[pl.BlockSpec((rows, LANES), lambda i, be, nu: (i, 0)),
                  pl.BlockSpec((1, 1, d, ff2), wmap),
                  pl.BlockSpec((1, 1, 1, ff2), wmap),
                  pl.BlockSpec((1, 1, ff, d), wmap),
                  pl.BlockSpec((1, 1, 1, d), wmap)],
        out_specs=pl.BlockSpec((rows, LANES), lambda i, be, nu: (i, 0)),
        scratch_shapes=[pltpu.VMEM((d, ff2), BF16), pltpu.VMEM((ff, d), BF16)],
    )
    return pl.pallas_call(
        functools.partial(_moe_kernel, ff=ff),
        grid_spec=grid_spec,
        out_shape=jax.ShapeDtypeStruct(xs.shape, I32),
        compiler_params=_params("arbitrary"),
        name="moe_experts",
    )(plan['be'], plan['n_used'], xs, w_gu, b_gu.reshape(depth, n_exp, 1, ff2),
      w_dn, b_dn.reshape(depth, n_exp, 1, d))


def _combine_kernel(sc_ref, so_ref, sd_ref, lpos_ref, x_ref, gate_ref, g_ref, ys_hbm, o_ref,
                    ybuf, gbuf, sems, *, n_exp):
    i = pl.program_id(0)
    tm, d = x_ref.shape
    half = d // 2
    pcs = half // LANES
    slot = i % 2

    def fetch(t, s):
        def copy(loc, blk):
            pltpu.make_async_copy(ys_hbm.at[blk], ybuf.at[s, loc], sems.at[s]).start()

        _segments(t, sc_ref, so_ref, sd_ref, n_exp, pcs, copy)

    @pl.when(i == 0)
    def _():
        fetch(i, slot)

    @pl.when(i + 1 < pl.num_programs(0))
    def _():
        fetch(i + 1, 1 - slot)

    pltpu.make_async_copy(ys_hbm.at[pl.ds(0, ybuf.shape[1])], ybuf.at[slot], sems.at[slot]).wait()

    def row(r, carry):
        for kk in range(TOP_K):
            p = lpos_ref[0, 0, TOP_K * r + kk]
            gbuf[kk, pl.ds(pl.multiple_of(r * pcs, pcs), pcs), :] = (
                ybuf[slot, pl.ds(pl.multiple_of(p * pcs, pcs), pcs), :])
        return carry

    lax.fori_loop(0, tm, row, 0, unroll=8)

    gt = gate_ref[...]
    lo = hi = None
    for kk in range(TOP_K):
        w = gt[:, kk:kk + 1]
        y_lo, y_hi = _unpack_pair(_load_tiles(gbuf.at[kk], 0, tm, pcs))
        lo = w * y_lo if lo is None else lo + w * y_lo
        hi = w * y_hi if hi is None else hi + w * y_hi
    g = g_ref[0]
    o_ref[:, :half] = x_ref[:, :half] + lo * g[:, :half]
    o_ref[:, half:] = x_ref[:, half:] + hi * g[:, half:]


def _combine(x, ys, gates, g2, plan, seq):
    n, d = x.shape
    pcs = d // 2 // LANES
    nt = plan['n_tiles']
    tm = n // nt
    grid_spec = pltpu.PrefetchScalarGridSpec(
        num_scalar_prefetch=3,
        grid=(nt,),
        in_specs=[pl.BlockSpec((1, 1, TOP_K * tm), lambda i, *_: (i, 0, 0), memory_space=pltpu.SMEM),
                  pl.BlockSpec((tm, d), lambda i, *_: (i, 0)),
                  pl.BlockSpec((tm, LANES), lambda i, *_: (i, 0)),
                  _batch_spec(d, seq // tm),
                  pl.BlockSpec(memory_space=pl.ANY)],
        out_specs=pl.BlockSpec((tm, d), lambda i, *_: (i, 0)),
        scratch_shapes=[pltpu.VMEM((2, TOP_K * tm * pcs, LANES), I32),
                        pltpu.VMEM((TOP_K, tm * pcs, LANES), I32),
                        pltpu.SemaphoreType.DMA((2,))],
    )
    return pl.pallas_call(
        functools.partial(_combine_kernel, n_exp=plan['n_exp']),
        grid_spec=grid_spec,
        out_shape=jax.ShapeDtypeStruct((n, d), F32),
        compiler_params=_params("arbitrary"),
        name="moe_combine",
    )(plan['seg_cnt'], plan['seg_off'], plan['seg_dst'], plan['lpos'], x, gates, g2, ys)


def _final_kernel(x_ref, g_ref, o_ref):
    x = x_ref[...]
    ms = jnp.mean(x * x, axis=-1, keepdims=True)
    o_ref[...] = (x * lax.rsqrt(ms + EPS)) * g_ref[...]


def _final_norm(x, g):
    n, d = x.shape
    tm = _tile(n, 512, 8)
    return pl.pallas_call(
        _final_kernel,
        grid=(n // tm,),
        in_specs=[pl.BlockSpec((tm, d), lambda i: (i, 0)), pl.BlockSpec((1, d), lambda i: (0, 0))],
        out_specs=pl.BlockSpec((tm, d), lambda i: (i, 0)),
        out_shape=jax.ShapeDtypeStruct((n, d), F32),
        compiler_params=_params("parallel"),
        name="final_norm",
    )(x, g)


def kernel(x, c, ada_w, ada_b, norm_mix, norm_ffn, norm_final, s5_lambda_re, s5_lambda_im, s5_log_dt,
           s5_b_re, s5_b_im, s5_c_re, s5_c_im, s5_d, s5_w_glu, hg_w_in, hg_lb_raw, hg_norm, hg_w_out,
           router_w, router_b, moe_w_gate_up, moe_b_gate_up, moe_w_down, moe_b_down):
    bsz, seq, d = x.shape
    depth = ada_w.shape[0]
    n = bsz * seq
    dv = hg_norm.shape[-1]
    heads = d // dv
    fdim = (hg_w_in.shape[-1] - 2 * d) // 2
    dk = fdim // heads
    assert dk == LANES and dv == LANES, "one HGRN2 head per 128-lane tile"
    tmb = _tile(n * TOP_K, MOE_ROWS, SUBLANES)

    lb_p = jax.nn.softmax(hg_lb_raw.astype(F32), axis=0)
    lb_all = jnp.cumsum(lb_p, axis=0) - lb_p[0]
    mod = _ada_mod(c.astype(F32), ada_w, ada_b)

    xs = x.astype(F32).reshape(n, d)
    row = lambda v: v.astype(F32).reshape(1, -1)
    for i in range(depth):
        sh1, sc1, g1, sh2, sc2, g2 = [mod[i, :, k * d:(k + 1) * d].reshape(bsz, 1, d) for k in range(N_ADA)]
        j = i // 2
        if i % 2 == 0:
            bmat, cmat, lre, lim = _s5_prep(s5_lambda_re[j], s5_lambda_im[j], s5_log_dt[j],
                                            s5_b_re[j], s5_b_im[j], s5_c_re[j], s5_c_im[j])
            z = _s5_scan(xs.reshape(bsz, seq, d), row(norm_mix[i]), sc1, sh1, bmat, cmat, lre, lim, row(s5_d[j]))
            xs = _mm_res(z.reshape(n, d), s5_w_glu[j].astype(BF16), xs, g1, True, seq)
        else:
            lb = lb_all[i]
            proj = _hg_proj(xs, row(norm_mix[i]), sc1, sh1, hg_w_in[j].astype(BF16),
                            row(jnp.log(lb)), row(jnp.log1p(-lb)), fdim, seq)
            o = _hg_rec(proj.reshape(bsz, seq, -1), row(hg_norm[j]), heads, dk)
            xs = _mm_res(o.reshape(n, d), hg_w_out[j].astype(BF16), xs, g1, False, seq)
        hp, top_idx, gates, rank, counts = _router(xs, row(norm_ffn[i]), sc2, sh2, router_w[i], router_b[i], seq)
        plan = _moe_plan(top_idx[:, :TOP_K], rank[:, :TOP_K], counts, tmb)
        xd = _dispatch(hp, plan, n)
        ys = _moe_experts(xd, plan, moe_w_gate_up, moe_b_gate_up.astype(F32),
                          moe_w_down, moe_b_down.astype(F32), i)
        xs = _combine(xs, ys, gates, g2, plan, seq)
    out = _final_norm(xs, row(norm_final))
    return out.reshape(bsz, seq, d).astype(x.dtype)
```

```python
import functools

import numpy as np
import jax
import jax.numpy as jnp
from jax import lax
from jax.experimental import pallas as pl
from jax.experimental.pallas import tpu as pltpu

F32 = jnp.float32
BF16 = jnp.bfloat16
I32 = jnp.int32

EPS = 1e-6
N_ADA = 6
TOP_K = 4
S5_RE_MAX = -1e-4
SWIGLU_LIMIT = 7.0
SWIGLU_ALPHA = 1.702
GELU_C0 = 0.7978845608028654
GELU_C1 = 0.044715

LANES = 128
SUBLANES = 8
HG_CHUNK = 64
HG_GROUP = 4
S5_CHUNK = 128
S5_SUB = 4
NORM_ROWS = 32
MOE_ROWS = 512
HI16 = -65536
VMEM_LIMIT = 56 * 1024 * 1024


def _params(*sem):
    return pltpu.CompilerParams(dimension_semantics=sem, vmem_limit_bytes=VMEM_LIMIT)


def _tile(n, pref, align):
    if n <= pref:
        return n
    t = (pref // align) * align
    while t > align and n % t:
        t -= align
    assert n % t == 0, (n, pref, align)
    return t


def _dot(a, b):
    return jnp.dot(a, b, preferred_element_type=F32)


def _dot_nt(a, b):
    return lax.dot_general(a, b, (((1,), (1,)), ((), ())), preferred_element_type=F32)


def _dot_tn(a, b):
    return lax.dot_general(a, b, (((0,), (0,)), ((), ())), preferred_element_type=F32)


def _split(a):
    hi = a.astype(BF16)
    lo = (a - hi.astype(F32)).astype(BF16)
    return hi, lo


def _dot3(a, w):
    ah, al = _split(a)
    wh, wl = _split(w)
    return _dot(ah, wh) + _dot(al, wh) + _dot(ah, wl)


def _norm_mod(x, g, sc, sh):
    ms = jnp.mean(x * x, axis=-1, keepdims=True)
    return ((x * lax.rsqrt(ms + EPS)) * g) * (1.0 + sc) + sh


def _norm_mod_rows(x_ref, mods, emit):
    rows = x_ref.shape[0]
    ch = min(NORM_ROWS, rows)

    def body(i, carry):
        r0 = pl.multiple_of(i * ch, ch)
        emit(r0, _norm_mod(x_ref[pl.ds(r0, ch), :], *mods()))
        return carry

    lax.fori_loop(0, rows // ch, body, 0, unroll=4 if (rows // ch) % 4 == 0 else 1)


def _batch_spec(width, tiles_per_batch):
    return pl.BlockSpec((1, 1, width), lambda i, *_: (i // tiles_per_batch, 0, 0))


def _pack_pair(lo, hi):
    lo_b = lax.bitcast_convert_type(lo.astype(BF16).astype(F32), I32)
    hi_b = lax.bitcast_convert_type(hi.astype(BF16).astype(F32), I32)
    return lax.shift_right_logical(lo_b, 16) | (hi_b & HI16)


def _unpack_pair(p):
    lo = lax.bitcast_convert_type(lax.shift_left(p, 16), F32)
    hi = lax.bitcast_convert_type(p & HI16, F32)
    return lo, hi


def _store_tiles(ref, row0, val):
    rows, width = val.shape
    pcs = width // LANES
    for jj in range(pcs):
        ref[pl.ds(row0 * pcs + jj, rows, stride=pcs), :] = val[:, jj * LANES:(jj + 1) * LANES]


def _load_tiles(ref, row0, rows, pcs):
    return jnp.concatenate([ref[pl.ds(row0 * pcs + jj, rows, stride=pcs), :] for jj in range(pcs)], axis=1)


def _ada_kernel(c_ref, w_ref, b_ref, o_ref):
    c = c_ref[...]
    o_ref[0] = _dot3(c * jax.nn.sigmoid(c), w_ref[0]) + b_ref[0]


def _ada_mod(c, ada_w, ada_b):
    depth, d, nd = ada_w.shape
    nb = c.shape[0]
    tn = _tile(nd, 768, LANES)
    return pl.pallas_call(
        _ada_kernel,
        grid=(depth, nd // tn),
        in_specs=[pl.BlockSpec((nb, d), lambda i, j: (0, 0)),
                  pl.BlockSpec((1, d, tn), lambda i, j: (i, 0, j)),
                  pl.BlockSpec((1, 1, tn), lambda i, j: (i, 0, j))],
        out_specs=pl.BlockSpec((1, nb, tn), lambda i, j: (i, 0, j)),
        out_shape=jax.ShapeDtypeStruct((depth, nb, nd), F32),
        compiler_params=_params("parallel", "parallel"),
        name="ada_mod",
    )(c, ada_w, ada_b.reshape(depth, 1, nd))


def _s5_kernel(x_ref, g_ref, sc_ref, sh_ref, bmat_ref, cmat_ref, lre_ref, lim_ref, d_ref,
               z_ref, h_s, st_s, bu_s, sa_s, z_s, *, tc, nb, hw):
    c = pl.program_id(0)
    j = pl.program_id(1)
    n_j = h_s.shape[0]

    @pl.when(j == 0)
    def _():
        for b in range(nb):
            def emit(r0, h, b=b):
                for jj in range(n_j):
                    h_s[jj, pl.ds(r0 * nb + b, h.shape[0], stride=nb), :] = h[:, jj * LANES:(jj + 1) * LANES]

            _norm_mod_rows(x_ref.at[b], lambda b=b: (g_ref[...], sc_ref[b], sh_ref[b]), emit)

    @pl.when(c == 0)
    def _():
        st_s[j] = jnp.zeros(st_s.shape[1:], F32)

    n_sub = bu_s.shape[0]
    ts = tc // n_sub
    rs = ts * nb
    bmat = bmat_ref[0]
    cmat = cmat_ref[0]
    for k in range(n_sub):
        bu_s[k] = _dot(h_s[j, k * rs:(k + 1) * rs, :].astype(BF16), bmat)
    lre = jnp.broadcast_to(lre_ref[0], (nb, hw))
    lim = jnp.broadcast_to(lim_ref[0], (nb, hw))
    st = st_s[j]
    s_re, s_im = st[:, :hw], st[:, hw:]
    for k in range(n_sub):
        for t in range(ts):
            b = bu_s[k, t * nb:(t + 1) * nb, :]
            s_re, s_im = (lre * s_re - lim * s_im + b[:, :hw],
                          lre * s_im + lim * s_re + b[:, hw:])
            sa_s[k, t * nb:(t + 1) * nb, :hw] = s_re
            sa_s[k, t * nb:(t + 1) * nb, hw:] = s_im
        y = _dot(sa_s[k].astype(BF16), cmat) + d_ref[...] * h_s[j, k * rs:(k + 1) * rs, :]
        z = 0.5 * y * (1.0 + jnp.tanh(GELU_C0 * (y + GELU_C1 * (y * y * y))))
        z_s[k * rs:(k + 1) * rs, :] = z
    st_s[j, :, :hw] = s_re
    st_s[j, :, hw:] = s_im
    for b in range(nb):
        z_ref[b] = z_s[pl.ds(b, tc, stride=nb), :].astype(BF16)


def _s5_prep(lam_re, lam_im, log_dt, b_re, b_im, c_re, c_im):
    n_g, n_p = lam_re.shape
    n_h = b_re.shape[-1]
    gpt = LANES // n_h
    n_j = n_g // gpt
    lam = lax.complex(jnp.minimum(lam_re.astype(F32), S5_RE_MAX), lam_im.astype(F32))
    dt = jnp.exp(log_dt.astype(F32))[:, None]
    lam_bar = jnp.exp(lam * dt)
    b_bar = ((lam_bar - 1.0) / lam)[:, :, None] * lax.complex(b_re.astype(F32), b_im.astype(F32))
    eye = jnp.eye(gpt, dtype=F32)
    bb = b_bar.reshape(n_j, gpt, n_p, n_h)
    bre = jnp.einsum('jkph,gk->jghkp', bb.real, eye)
    bim = jnp.einsum('jkph,gk->jghkp', bb.imag, eye)
    bmat = jnp.stack([bre, bim], axis=3).reshape(n_j, LANES, 2 * gpt * n_p)
    cre = jnp.einsum('jghp,kg->jkpgh', c_re.astype(F32).reshape(n_j, gpt, n_h, n_p), eye)
    cim = jnp.einsum('jghp,kg->jkpgh', c_im.astype(F32).reshape(n_j, gpt, n_h, n_p), eye)
    cmat = jnp.stack([cre, -cim], axis=1).reshape(n_j, 2 * gpt * n_p, LANES)
    lre = lam_bar.real.reshape(n_j, 1, gpt * n_p)
    lim = lam_bar.imag.reshape(n_j, 1, gpt * n_p)
    return bmat.astype(BF16), cmat.astype(BF16), lre, lim


def _s5_scan(x, g, sc, sh, bmat, cmat, lre, lim, d_skip):
    nb, seq, d = x.shape
    n_j = d // LANES
    hw = lre.shape[-1]
    tc = _tile(seq, S5_CHUNK, 8)
    rows = tc * nb
    n_sub = S5_SUB if tc % S5_SUB == 0 else 1
    kern = functools.partial(_s5_kernel, tc=tc, nb=nb, hw=hw)
    return pl.pallas_call(
        kern,
        grid=(seq // tc, n_j),
        in_specs=[pl.BlockSpec((nb, tc, d), lambda c, j: (0, c, 0)),
                  pl.BlockSpec((1, d), lambda c, j: (0, 0)),
                  pl.BlockSpec((nb, 1, d), lambda c, j: (0, 0, 0)),
                  pl.BlockSpec((nb, 1, d), lambda c, j: (0, 0, 0)),
                  pl.BlockSpec((1, LANES, 2 * hw), lambda c, j: (j, 0, 0)),
                  pl.BlockSpec((1, 2 * hw, LANES), lambda c, j: (j, 0, 0)),
                  pl.BlockSpec((1, 1, hw), lambda c, j: (j, 0, 0)),
                  pl.BlockSpec((1, 1, hw), lambda c, j: (j, 0, 0)),
                  pl.BlockSpec((1, LANES), lambda c, j: (0, j))],
        out_specs=pl.BlockSpec((nb, tc, LANES), lambda c, j: (0, c, j)),
        out_shape=jax.ShapeDtypeStruct((nb, seq, d), BF16),
        scratch_shapes=[pltpu.VMEM((n_j, rows, LANES), F32),
                        pltpu.VMEM((n_j, nb, 2 * hw), F32),
                        pltpu.VMEM((n_sub, rows // n_sub, 2 * hw), F32),
                        pltpu.VMEM((n_sub, rows // n_sub, 2 * hw), F32),
                        pltpu.VMEM((rows, LANES), F32)],
        compiler_params=_params("arbitrary", "arbitrary"),
        name="s5_scan",
    )(x, g, sc, sh, bmat, cmat, lre, lim, d_skip)


def _mm_res_kernel(a_ref, w_ref, x_ref, g_ref, o_ref, *, glu, tn):
    a = a_ref[...]
    d = o_ref.shape[1]
    for c0 in range(0, d, tn):
        p = _dot(a, w_ref[:, c0:c0 + tn])
        if glu:
            p = p * jax.nn.sigmoid(_dot(a, w_ref[:, d + c0:d + c0 + tn]))
        o_ref[:, c0:c0 + tn] = x_ref[:, c0:c0 + tn] + p * g_ref[0, :, c0:c0 + tn]


def _mm_res(a, w, x, gate, glu, seq):
    n, kdim = a.shape
    d = x.shape[1]
    tm = _tile(seq, 512, 16)
    tn = _tile(d, 512, LANES)
    return pl.pallas_call(
        functools.partial(_mm_res_kernel, glu=glu, tn=tn),
        grid=(n // tm,),
        in_specs=[pl.BlockSpec((tm, kdim), lambda i: (i, 0)),
                  pl.BlockSpec(w.shape, lambda i: (0, 0), pipeline_mode=pl.Buffered(1)),
                  pl.BlockSpec((tm, d), lambda i: (i, 0)),
                  _batch_spec(d, seq // tm)],
        out_specs=pl.BlockSpec((tm, d), lambda i: (i, 0)),
        out_shape=jax.ShapeDtypeStruct((n, d), F32),
        compiler_params=_params("parallel"),
        name="glu_res" if glu else "proj_res",
    )(a, w, x, gate)


def _hg_norm_kernel(x_ref, g_ref, sc_ref, sh_ref, o_ref):
    def emit(r0, h):
        o_ref[pl.ds(r0, h.shape[0]), :] = h.astype(BF16)

    _norm_mod_rows(x_ref, lambda: (g_ref[...], sc_ref[0], sh_ref[0]), emit)


def _hg_norm(x, g, sc, sh, seq):
    n, d = x.shape
    tm = _tile(seq, 1024, 16)
    bspec = _batch_spec(d, seq // tm)
    return pl.pallas_call(
        _hg_norm_kernel,
        grid=(n // tm,),
        in_specs=[pl.BlockSpec((tm, d), lambda i: (i, 0)), pl.BlockSpec((1, d), lambda i: (0, 0)), bspec, bspec],
        out_specs=pl.BlockSpec((tm, d), lambda i: (i, 0)),
        out_shape=jax.ShapeDtypeStruct((n, d), BF16),
        compiler_params=_params("parallel"),
        name="hg_norm",
    )(x, g, sc, sh)


def _hg_levels(ch):
    levels = []
    m = ch // 2
    while m >= 1:
        levels.append(m)
        m //= 2
    return levels


def _hg_masks(ch):
    levels = _hg_levels(ch)
    mk = np.zeros((len(levels) + 1, ch, ch), np.float32)
    for li, m in enumerate(levels):
        for t in range(ch):
            mid = (t // (2 * m)) * 2 * m + m - 1
            if t > mid:
                mk[li, t, mid - m + 1:mid + 1] = 1.0
    mk[-1] = np.eye(ch, dtype=np.float32)
    return mk


def _cumsum_rows(x, t_idx):
    sh = 1
    while sh < x.shape[0]:
        x = x + jnp.where(t_idx >= sh, pltpu.roll(x, sh, 0), 0.0)
        sh *= 2
    return x


def _level_ref(bc, m, sub3):
    ch, dk = bc.shape
    if 2 * m >= SUBLANES:
        bp = bc.reshape(ch // (2 * m), 2 * m, dk)
        return jnp.broadcast_to(bp[:, m - 1:m, :], bp.shape).reshape(ch, dk)
    b3 = bc.reshape(ch // SUBLANES, SUBLANES, dk)
    r = None
    for start in range(0, SUBLANES, 2 * m):
        cand = jnp.broadcast_to(b3[:, start + m - 1:start + m, :], b3.shape)
        r = cand if r is None else jnp.where(sub3 >= start, cand, r)
    return r.reshape(ch, dk)


def _hg_mix_kernel(h0_ref, hn_ref, w_ref, la_ref, l1_ref, mask_ref, gn_ref, o_ref, st_s, pa_s, pb_s, *, nb, ch):
    c = pl.program_id(1)
    d = h0_ref.shape[-1]

    def project(h_ref, c0, width):
        return _dot(h_ref[...].reshape(nb * ch, d), w_ref[0, :, c0:c0 + width])

    @pl.when(c == 0)
    def _():
        st_s[...] = jnp.zeros(st_s.shape, F32)
        pa_s[...] = project(h0_ref, 0, pa_s.shape[-1])

    @pl.when(c % 2 == 0)
    def _():
        _hg_chunk(hn_ref, w_ref, la_ref, l1_ref, mask_ref, gn_ref, o_ref, st_s, pa_s, pb_s, project, nb, ch)

    @pl.when(c % 2 == 1)
    def _():
        _hg_chunk(hn_ref, w_ref, la_ref, l1_ref, mask_ref, gn_ref, o_ref, st_s, pb_s, pa_s, project, nb, ch)


def _hg_chunk(hn_ref, w_ref, la_ref, l1_ref, mask_ref, gn_ref, o_ref, st_s, p_s, pn_s, project, nb, ch):
    dk = o_ref.shape[-1]
    cols = pn_s.shape[-1]
    assert nb % HG_GROUP == 0 and cols % (nb // HG_GROUP * LANES) == 0
    cpg = cols // (nb // HG_GROUP)

    la = la_ref[0]
    l1 = l1_ref[0]
    levels = _hg_levels(ch)
    n_lv = len(levels)
    gn = gn_ref[...]
    t_idx = lax.broadcasted_iota(I32, (ch, dk), 0)
    sub3 = lax.broadcasted_iota(I32, (ch // SUBLANES, SUBLANES, dk), 1)
    masks = [mask_ref[li] > 0.5 for li in range(n_lv + 1)]

    for b0 in range(0, nb, HG_GROUP):
        grp = range(b0, min(b0 + HG_GROUP, nb))
        c0 = (b0 // HG_GROUP) * cpg
        pn_s[:, c0:c0 + cpg] = project(hn_ref, c0, cpg)
        qk, qms, kms, q_in, k_out, decay = {}, {}, {}, {}, {}, {}
        for b in grp:
            q = p_s[b * ch:(b + 1) * ch, 0:dk]
            z = p_s[b * ch:(b + 1) * ch, dk:2 * dk]
            bt = l1 + jnp.minimum(z, 0.0) - jnp.log(1.0 + jnp.exp(-jnp.abs(z)))
            lf = jnp.maximum(la, bt) + jnp.log(1.0 + jnp.exp(-jnp.abs(la - bt)))
            bc = _cumsum_rows(lf, t_idx)
            qb = q.astype(BF16)
            kb = (1.0 - jnp.exp(lf)).astype(BF16)
            qk[b] = (qb, kb)
            qms[b], kms[b] = [], []
            for m in levels:
                e = jnp.exp(-jnp.abs(bc - _level_ref(bc, m, sub3))).astype(BF16)
                qms[b].append(qb * e)
                kms[b].append(kb * e)
            b_end = bc[ch - 1:ch, :]
            q_in[b] = qb * jnp.exp(bc).astype(BF16)
            k_out[b] = kb * jnp.exp(b_end - bc).astype(BF16)
            decay[b] = jnp.exp(b_end)
        scores = {}
        for b in grp:
            s = jnp.where(masks[n_lv], _dot_nt(*qk[b]), 0.0)
            for li in range(n_lv):
                s = jnp.where(masks[li], _dot_nt(qms[b][li], kms[b][li]), s)
            scores[b] = s.astype(BF16)
        outs = {}
        for b in grp:
            v = p_s[b * ch:(b + 1) * ch, 2 * dk:3 * dk].astype(BF16)
            st = st_s[b]
            outs[b] = _dot(scores[b], v) + _dot_nt(q_in[b], st.astype(BF16))
            st_s[b] = decay[b] * st + _dot_tn(v, k_out[b])
        for b in grp:
            o = outs[b]
            gate = p_s[b * ch:(b + 1) * ch, 3 * dk:4 * dk]
            o = o * lax.rsqrt(jnp.mean(o * o, axis=-1, keepdims=True) + EPS) * gn
            o_ref[b] = (o * (gate * jax.nn.sigmoid(gate))).astype(BF16)


def _hg_mix(h, w_heads, log_lb, log1m_lb, g_norm):
    nb, seq, d = h.shape
    heads, _, cols = w_heads.shape
    dk = cols // 4
    ch = _tile(seq, HG_CHUNK, 16)
    nc = seq // ch
    mk = _hg_masks(ch)
    per_head = lambda w: pl.BlockSpec((1,) + w.shape[1:], lambda hd, c: (hd, 0, 0))
    return pl.pallas_call(
        functools.partial(_hg_mix_kernel, nb=nb, ch=ch),
        grid=(heads, nc),
        in_specs=[pl.BlockSpec((nb, ch, d), lambda hd, c: (0, 0, 0)),
                  pl.BlockSpec((nb, ch, d), lambda hd, c: (0, jnp.minimum(c + 1, nc - 1), 0)),
                  per_head(w_heads), per_head(log_lb), per_head(log1m_lb),
                  pl.BlockSpec(mk.shape, lambda hd, c: (0, 0, 0)),
                  pl.BlockSpec((1, dk), lambda hd, c: (0, 0))],
        out_specs=pl.BlockSpec((nb, ch, dk), lambda hd, c: (0, c, hd)),
        out_shape=jax.ShapeDtypeStruct((nb, seq, heads * dk), BF16),
        scratch_shapes=[pltpu.VMEM((nb, dk, dk), F32), pltpu.VMEM((nb * ch, cols), F32),
                        pltpu.VMEM((nb * ch, cols), F32)],
        compiler_params=_params("parallel", "arbitrary"),
        name="hg_mix",
    )(h, h, w_heads, log_lb, log1m_lb, jnp.asarray(mk, F32), g_norm)


def _router_kernel(x_ref, g_ref, sc_ref, sh_ref, wh_ref, wl_ref, br_ref, tri_ref,
                   hp_ref, idx_ref, gate_ref, rank_ref, cnt_ref):
    h = _norm_mod(x_ref[...], g_ref[...], sc_ref[0], sh_ref[0])
    half = h.shape[1] // 2
    _store_tiles(hp_ref, 0, _pack_pair(h[:, :half], h[:, half:]))
    hh, hl = _split(h)
    wh = wh_ref[...]
    vals = _dot(hh, wh) + _dot(hl, wh) + _dot(hh, wl_ref[...]) + br_ref[...]
    tm, n_exp = vals.shape
    lane = lax.broadcasted_iota(I32, (tm, n_exp), 1)
    tops, idxs, hots = [], [], []
    for _ in range(TOP_K):
        m = jnp.max(vals, axis=-1, keepdims=True)
        i = jnp.min(jnp.where(vals == m, lane, n_exp), axis=-1, keepdims=True)
        tops.append(m)
        idxs.append(i)
        hots.append(lane == i)
        vals = jnp.where(hots[-1], -jnp.inf, vals)
    es = [jnp.exp(t - tops[0]) for t in tops]
    den = es[0] + es[1] + es[2] + es[3]

    sel = [jnp.where(hm, 1.0, 0.0) for hm in hots]
    multi = sel[0] + sel[1] + sel[2] + sel[3]
    base = _dot(tri_ref[...], multi.astype(BF16))
    cnt_ref[0] = jnp.sum(multi, axis=0, keepdims=True)

    lane_o = lax.broadcasted_iota(I32, idx_ref.shape, 1)
    io = jnp.zeros(idx_ref.shape, I32)
    ro = jnp.zeros(idx_ref.shape, I32)
    go = jnp.zeros(gate_ref.shape, F32)
    for kk in range(TOP_K):
        rk = jnp.sum(sel[kk] * base, axis=-1, keepdims=True).astype(I32)
        io = jnp.where(lane_o == kk, idxs[kk], io)
        ro = jnp.where(lane_o == kk, rk, ro)
        go = jnp.where(lane_o == kk, es[kk] / den, go)
    idx_ref[...] = io
    rank_ref[...] = ro
    gate_ref[...] = go


def _router(x, g, sc, sh, w_r, b_r, seq):
    n, d = x.shape
    n_exp = w_r.shape[1]
    tm = _tile(seq, 512, 16)
    bspec = _batch_spec(d, seq // tm)
    pcs = d // 2 // LANES
    wh, wl = _split(w_r.astype(F32))
    tri = jnp.asarray(np.tril(np.ones((tm, tm), np.float32), -1), BF16)
    row = lambda w: pl.BlockSpec((tm, w), lambda i: (i, 0))
    full = lambda a: pl.BlockSpec(a.shape, lambda i: (0, 0))
    b2 = b_r.astype(F32).reshape(1, n_exp)
    return pl.pallas_call(
        _router_kernel,
        grid=(n // tm,),
        in_specs=[row(d), full(g), bspec, bspec, full(wh), full(wl), full(b2), full(tri)],
        out_specs=[pl.BlockSpec((tm * pcs, LANES), lambda i: (i, 0)), row(LANES), row(LANES), row(LANES),
                   pl.BlockSpec((1, 1, n_exp), lambda i: (i, 0, 0))],
        out_shape=[jax.ShapeDtypeStruct((n * pcs, LANES), I32),
                   jax.ShapeDtypeStruct((n, LANES), I32),
                   jax.ShapeDtypeStruct((n, LANES), F32),
                   jax.ShapeDtypeStruct((n, LANES), I32),
                   jax.ShapeDtypeStruct((n // tm, 1, n_exp), F32)],
        compiler_params=_params("parallel"),
        name="router",
    )(x, g, sc, sh, wh, wl, b2, tri)


def _moe_plan(top_idx, lrank, tile_counts, tmb):
    n = top_idx.shape[0]
    n_tiles, _, n_exp = tile_counts.shape
    tc = tile_counts.reshape(n_tiles, n_exp).astype(I32)
    counts = jnp.sum(tc, axis=0)
    nblk = (counts + tmb - 1) // tmb
    blk_end = jnp.cumsum(nblk)
    blk_start = blk_end - nblk
    n_used = blk_end[-1]
    n_blocks = n * TOP_K // tmb + n_exp
    blk = jnp.arange(n_blocks, dtype=I32)
    be = jnp.minimum(jnp.sum((blk[:, None] >= blk_end[None, :]).astype(I32), axis=1), n_exp - 1)
    last = jnp.sum(jnp.where(blk == n_used - 1, be, 0))
    be = jnp.where(blk < n_used, be, last).astype(I32)
    base = blk_start * tmb
    seg_dst = base[None, :] + jnp.cumsum(tc, axis=0) - tc
    seg_off = jnp.cumsum(tc, axis=1) - tc
    tm = n // n_tiles
    hot = top_idx.reshape(n_tiles, tm, TOP_K, 1) == jnp.arange(n_exp, dtype=I32)
    lpos = jnp.sum(jnp.where(hot, seg_off[:, None, None, :], 0), axis=-1) + lrank.reshape(n_tiles, tm, TOP_K)
    pad_start = base + counts
    pad_len = blk_end * tmb - pad_start
    flat = lambda a: a.reshape(-1).astype(I32)
    return dict(be=be, n_used=n_used.reshape(1).astype(I32),
                lpos=lpos.reshape(n_tiles, 1, tm * TOP_K).astype(I32),
                seg_cnt=flat(tc), seg_off=flat(seg_off), seg_dst=flat(seg_dst),
                pad_start=pad_start.astype(I32), pad_len=pad_len.astype(I32),
                n_blocks=n_blocks, tmb=tmb, n_tiles=n_tiles, n_exp=n_exp)


def _tile_at(ref, row, pcs):
    return ref.at[pl.ds(pl.multiple_of(row * pcs, pcs), pcs)]


def _segments(i, cnt_ref, off_ref, dst_ref, n_exp, pcs, copy):
    def seg(e, carry):
        cnt = cnt_ref[i * n_exp + e]

        @pl.when(cnt > 0)
        def _():
            size = pl.multiple_of(cnt * pcs, pcs)
            copy(pl.ds(pl.multiple_of(off_ref[i * n_exp + e] * pcs, pcs), size),
                 pl.ds(pl.multiple_of(dst_ref[i * n_exp + e] * pcs, pcs), size))

        return carry

    lax.fori_loop(0, n_exp, seg, 0)


def _dispatch_kernel(sc_ref, so_ref, sd_ref, ps_ref, pn_ref, nu_ref, lpos_ref, hp_ref, xs_hbm,
                     sbuf, zero_s, sems, sem, *, tmb, pcs):
    i = pl.program_id(0)
    last = pl.num_programs(0) - 1
    tm = hp_ref.shape[0] // pcs
    slot = i % 2

    def drain(s):
        pltpu.make_async_copy(sbuf.at[s], xs_hbm.at[pl.ds(0, sbuf.shape[1])], sems.at[s]).wait()

    @pl.when(i >= 2)
    def _():
        drain(slot)

    def row(r, carry):
        tile = hp_ref[pl.ds(pl.multiple_of(r * pcs, pcs), pcs), :]
        for kk in range(TOP_K):
            p = lpos_ref[0, 0, TOP_K * r + kk]
            sbuf[slot, pl.ds(pl.multiple_of(p * pcs, pcs), pcs), :] = tile
        return carry

    lax.fori_loop(0, tm, row, 0, unroll=8)

    def copy(src, dst):
        pltpu.make_async_copy(sbuf.at[slot, src], xs_hbm.at[dst], sems.at[slot]).start()

    _segments(i, sc_ref, so_ref, sd_ref, ps_ref.shape[0], pcs, copy)

    @pl.when(i == last)
    def _():
        drain(slot)

        @pl.when(i >= 1)
        def _():
            drain(1 - slot)

        zero_s[...] = jnp.zeros(zero_s.shape, I32)

        size = tmb // 2
        while size >= 1:
            def chunk(e, size=size):
                cnt = pn_ref[e]
                before = cnt - cnt % (2 * size)
                dst = pl.ds(pl.multiple_of((ps_ref[e] + before) * pcs, pcs), size * pcs)
                return (cnt // size) % 2 == 1, pltpu.make_async_copy(
                    zero_s.at[pl.ds(0, size * pcs)], xs_hbm.at[dst], sem)

            def fill(e, carry):
                take, cp = chunk(e)

                @pl.when(take)
                def _():
                    cp.start()

                return carry

            def fill_done(e, carry):
                take, cp = chunk(e)

                @pl.when(take)
                def _():
                    cp.wait()

                return carry

            lax.fori_loop(0, ps_ref.shape[0], fill, 0)
            lax.fori_loop(0, ps_ref.shape[0], fill_done, 0)
            size //= 2

        def block(bk, carry):
            cp = pltpu.make_async_copy(zero_s, _tile_at(xs_hbm, bk, tmb * pcs), sem)
            cp.start()
            cp.wait()
            return carry

        lax.fori_loop(nu_ref[0], xs_hbm.shape[0] // (tmb * pcs), block, 0)


def _dispatch(hp, plan, n):
    pcs = hp.shape[0] // n
    tmb = plan['tmb']
    n_tiles = plan['n_tiles']
    tm = n // n_tiles
    grid_spec = pltpu.PrefetchScalarGridSpec(
        num_scalar_prefetch=6,
        grid=(n_tiles,),
        in_specs=[pl.BlockSpec((1, 1, TOP_K * tm), lambda i, *_: (i, 0, 0), memory_space=pltpu.SMEM),
                  pl.BlockSpec((tm * pcs, LANES), lambda i, *_: (i, 0))],
        out_specs=pl.BlockSpec(memory_space=pl.ANY),
        scratch_shapes=[pltpu.VMEM((2, TOP_K * tm * pcs, LANES), I32),
                        pltpu.VMEM((tmb * pcs, LANES), I32),
                        pltpu.SemaphoreType.DMA((2,)), pltpu.SemaphoreType.DMA(())],
    )
    return pl.pallas_call(
        functools.partial(_dispatch_kernel, tmb=tmb, pcs=pcs),
        grid_spec=grid_spec,
        out_shape=jax.ShapeDtypeStruct((plan['n_blocks'] * tmb * pcs, LANES), I32),
        compiler_params=_params("arbitrary"),
        name="moe_dispatch",
    )(plan['seg_cnt'], plan['seg_off'], plan['seg_dst'], plan['pad_start'], plan['pad_len'], plan['n_used'],
      plan['lpos'], hp)


def _moe_kernel(be_ref, nu_ref, x_ref, wgu_ref, bgu_ref, wdn_ref, bdn_ref, y_ref, wgu_s, wdn_s, *, ff):
    i = pl.program_id(0)
    used = i < nu_ref[0]

    @pl.when(used)
    def _():
        e_new = jnp.logical_or(i == 0, be_ref[i] != be_ref[jnp.maximum(i - 1, 0)])

        @pl.when(e_new)
        def _():
            wgu_s[...] = wgu_ref[0, 0].astype(BF16)
            wdn_s[...] = wdn_ref[0, 0].astype(BF16)

        half = wgu_s.shape[0] // 2
        pcs = half // LANES
        tmb = x_ref.shape[0] // pcs
        x_lo, x_hi = _unpack_pair(_load_tiles(x_ref, 0, tmb, pcs))
        gu = (_dot(x_lo.astype(BF16), wgu_s[:half, :]) + _dot(x_hi.astype(BF16), wgu_s[half:, :])
              + bgu_ref[0, 0])
        gt = jnp.minimum(gu[:, :ff], SWIGLU_LIMIT)
        up = jnp.clip(gu[:, ff:], -SWIGLU_LIMIT, SWIGLU_LIMIT)
        act = ((up + 1.0) * gt * jax.nn.sigmoid(SWIGLU_ALPHA * gt)).astype(BF16)
        bdn = bdn_ref[0, 0]
        y_lo = _dot(act, wdn_s[:, :half]) + bdn[:, :half]
        y_hi = _dot(act, wdn_s[:, half:]) + bdn[:, half:]
        _store_tiles(y_ref, 0, _pack_pair(y_lo, y_hi))

    @pl.when(jnp.logical_not(used))
    def _():
        y_ref[...] = jnp.zeros(y_ref.shape, I32)


def _moe_experts(xs, plan, w_gu, b_gu, w_dn, b_dn, layer):
    depth, n_exp, d, ff2 = w_gu.shape
    ff = ff2 // 2
    tmb = plan['tmb']
    n_blocks = plan['n_blocks']
    rows = xs.shape[0] // n_blocks
    wmap = lambda i, be, nu: (layer, be[i], 0, 0)
    grid_spec = pltpu.PrefetchScalarGridSpec(
        num_scalar_prefetch=2,
        grid=(n_blocks,),
        in_specs=[pl.BlockSpec((rows, LANES), lambda i, be, nu: (i, 0)),
                  pl.BlockSpec((1, 1, d, ff2), wmap),
                  pl.BlockSpec((1, 1, 1, ff2), wmap),
                  pl.BlockSpec((1, 1, ff, d), wmap),
                  pl.BlockSpec((1, 1, 1, d), wmap)],
        out_specs=pl.BlockSpec((rows, LANES), lambda i, be, nu: (i, 0)),
        scratch_shapes=[pltpu.VMEM((d, ff2), BF16), pltpu.VMEM((ff, d), BF16)],
    )
    return pl.pallas_call(
        functools.partial(_moe_kernel, ff=ff),
        grid_spec=grid_spec,
        out_shape=jax.ShapeDtypeStruct(xs.shape, I32),
        compiler_params=_params("arbitrary"),
        name="moe_experts",
    )(plan['be'], plan['n_used'], xs, w_gu, b_gu.reshape(depth, n_exp, 1, ff2),
      w_dn, b_dn.reshape(depth, n_exp, 1, d))


def _combine_kernel(sc_ref, so_ref, sd_ref, lpos_ref, x_ref, gate_ref, g_ref, ys_hbm, o_ref,
                    ybuf, gbuf, sems, *, n_exp):
    i = pl.program_id(0)
    tm, d = x_ref.shape
    half = d // 2
    pcs = half // LANES
    slot = i % 2

    def fetch(t, s):
        def copy(loc, blk):
            pltpu.make_async_copy(ys_hbm.at[blk], ybuf.at[s, loc], sems.at[s]).start()

        _segments(t, sc_ref, so_ref, sd_ref, n_exp, pcs, copy)

    @pl.when(i == 0)
    def _():
        fetch(i, slot)

    @pl.when(i + 1 < pl.num_programs(0))
    def _():
        fetch(i + 1, 1 - slot)

    pltpu.make_async_copy(ys_hbm.at[pl.ds(0, ybuf.shape[1])], ybuf.at[slot], sems.at[slot]).wait()

    def row(r, carry):
        for kk in range(TOP_K):
            p = lpos_ref[0, 0, TOP_K * r + kk]
            gbuf[kk, pl.ds(pl.multiple_of(r * pcs, pcs), pcs), :] = (
                ybuf[slot, pl.ds(pl.multiple_of(p * pcs, pcs), pcs), :])
        return carry

    lax.fori_loop(0, tm, row, 0, unroll=8)

    gt = gate_ref[...]
    lo = hi = None
    for kk in range(TOP_K):
        w = gt[:, kk:kk + 1]
        y_lo, y_hi = _unpack_pair(_load_tiles(gbuf.at[kk], 0, tm, pcs))
        lo = w * y_lo if lo is None else lo + w * y_lo
        hi = w * y_hi if hi is None else hi + w * y_hi
    g = g_ref[0]
    o_ref[:, :half] = x_ref[:, :half] + lo * g[:, :half]
    o_ref[:, half:] = x_ref[:, half:] + hi * g[:, half:]


def _combine(x, ys, gates, g2, plan, seq):
    n, d = x.shape
    pcs = d // 2 // LANES
    nt = plan['n_tiles']
    tm = n // nt
    grid_spec = pltpu.PrefetchScalarGridSpec(
        num_scalar_prefetch=3,
        grid=(nt,),
        in_specs=[pl.BlockSpec((1, 1, TOP_K * tm), lambda i, *_: (i, 0, 0), memory_space=pltpu.SMEM),
                  pl.BlockSpec((tm, d), lambda i, *_: (i, 0)),
                  pl.BlockSpec((tm, LANES), lambda i, *_: (i, 0)),
                  _batch_spec(d, seq // tm),
                  pl.BlockSpec(memory_space=pl.ANY)],
        out_specs=pl.BlockSpec((tm, d), lambda i, *_: (i, 0)),
        scratch_shapes=[pltpu.VMEM((2, TOP_K * tm * pcs, LANES), I32),
                        pltpu.VMEM((TOP_K, tm * pcs, LANES), I32),
                        pltpu.SemaphoreType.DMA((2,))],
    )
    return pl.pallas_call(
        functools.partial(_combine_kernel, n_exp=plan['n_exp']),
        grid_spec=grid_spec,
        out_shape=jax.ShapeDtypeStruct((n, d), F32),
        compiler_params=_params("arbitrary"),
        name="moe_combine",
    )(plan['seg_cnt'], plan['seg_off'], plan['seg_dst'], plan['lpos'], x, gates, g2, ys)


def _final_kernel(x_ref, g_ref, o_ref):
    x = x_ref[...]
    ms = jnp.mean(x * x, axis=-1, keepdims=True)
    o_ref[...] = (x * lax.rsqrt(ms + EPS)) * g_ref[...]


def _final_norm(x, g):
    n, d = x.shape
    tm = _tile(n, 512, 8)
    return pl.pallas_call(
        _final_kernel,
        grid=(n // tm,),
        in_specs=[pl.BlockSpec((tm, d), lambda i: (i, 0)), pl.BlockSpec((1, d), lambda i: (0, 0))],
        out_specs=pl.BlockSpec((tm, d), lambda i: (i, 0)),
        out_shape=jax.ShapeDtypeStruct((n, d), F32),
        compiler_params=_params("parallel"),
        name="final_norm",
    )(x, g)


def kernel(x, c, ada_w, ada_b, norm_mix, norm_ffn, norm_final, s5_lambda_re, s5_lambda_im, s5_log_dt,
           s5_b_re, s5_b_im, s5_c_re, s5_c_im, s5_d, s5_w_glu, hg_w_in, hg_lb_raw, hg_norm, hg_w_out,
           router_w, router_b, moe_w_gate_up, moe_b_gate_up, moe_w_down, moe_b_down):
    bsz, seq, d = x.shape
    depth = ada_w.shape[0]
    n = bsz * seq
    dv = hg_norm.shape[-1]
    heads = d // dv
    fdim = (hg_w_in.shape[-1] - 2 * d) // 2
    dk = fdim // heads
    assert dk == LANES and dv == LANES, "one HGRN2 head per 128-lane tile"
    tmb = _tile(n * TOP_K, MOE_ROWS, SUBLANES)

    lb_p = jax.nn.softmax(hg_lb_raw.astype(F32), axis=0)
    lb_all = jnp.cumsum(lb_p, axis=0) - lb_p[0]
    mod = _ada_mod(c.astype(F32), ada_w, ada_b)

    xs = x.astype(F32).reshape(n, d)
    row = lambda v: v.astype(F32).reshape(1, -1)
    for i in range(depth):
        sh1, sc1, g1, sh2, sc2, g2 = [mod[i, :, k * d:(k + 1) * d].reshape(bsz, 1, d) for k in range(N_ADA)]
        j = i // 2
        if i % 2 == 0:
            bmat, cmat, lre, lim = _s5_prep(s5_lambda_re[j], s5_lambda_im[j], s5_log_dt[j],
                                            s5_b_re[j], s5_b_im[j], s5_c_re[j], s5_c_im[j])
            z = _s5_scan(xs.reshape(bsz, seq, d), row(norm_mix[i]), sc1, sh1, bmat, cmat, lre, lim, row(s5_d[j]))
            xs = _mm_res(z.reshape(n, d), s5_w_glu[j].astype(BF16), xs, g1, True, seq)
        else:
            lb = lb_all[i].reshape(heads, 1, dk)
            w_heads = hg_w_in[j].astype(BF16).reshape(d, 4, heads, dk).transpose(2, 0, 1, 3).reshape(heads, d, 4 * dk)
            h = _hg_norm(xs, row(norm_mix[i]), sc1, sh1, seq)
            o = _hg_mix(h.reshape(bsz, seq, d), w_heads, jnp.log(lb), jnp.log1p(-lb), row(hg_norm[j]))
            xs = _mm_res(o.reshape(n, d), hg_w_out[j].astype(BF16), xs, g1, False, seq)
        hp, top_idx, gates, rank, counts = _router(xs, row(norm_ffn[i]), sc2, sh2, router_w[i], router_b[i], seq)
        plan = _moe_plan(top_idx[:, :TOP_K], rank[:, :TOP_K], counts, tmb)
        xd = _dispatch(hp, plan, n)
        ys = _moe_experts(xd, plan, moe_w_gate_up, moe_b_gate_up.astype(F32),
                          moe_w_down, moe_b_down.astype(F32), i)
        xs = _combine(xs, ys, gates, g2, plan, seq)
    out = _final_norm(xs, row(norm_final))
    return out.reshape(bsz, seq, d).astype(x.dtype)
```

```python
import functools

import numpy as np
import jax
import jax.numpy as jnp
from jax import lax
from jax.experimental import pallas as pl
from jax.experimental.pallas import tpu as pltpu

F32 = jnp.float32
BF16 = jnp.bfloat16
I32 = jnp.int32

EPS = 1e-6
N_ADA = 6
TOP_K = 4
S5_RE_MAX = -1e-4
SWIGLU_LIMIT = 7.0
SWIGLU_ALPHA = 1.702
GELU_C0 = 0.7978845608028654
GELU_C1 = 0.044715

LANES = 128
SUBLANES = 8
HG_CHUNK = 64
HG_GROUP = 4
S5_CHUNK = 128
S5_SUB = 4
NORM_ROWS = 32
MOE_ROWS = 512
HI16 = -65536
VMEM_LIMIT = 56 * 1024 * 1024


def _params(*sem):
    return pltpu.CompilerParams(dimension_semantics=sem, vmem_limit_bytes=VMEM_LIMIT)


def _tile(n, pref, align):
    if n <= pref:
        return n
    t = (pref // align) * align
    while t > align and n % t:
        t -= align
    assert n % t == 0, (n, pref, align)
    return t


def _dot(a, b):
    return jnp.dot(a, b, preferred_element_type=F32)


def _dot_nt(a, b):
    return lax.dot_general(a, b, (((1,), (1,)), ((), ())), preferred_element_type=F32)


def _dot_tn(a, b):
    return lax.dot_general(a, b, (((0,), (0,)), ((), ())), preferred_element_type=F32)


def _split(a):
    hi = a.astype(BF16)
    lo = (a - hi.astype(F32)).astype(BF16)
    return hi, lo


def _dot3(a, w):
    ah, al = _split(a)
    wh, wl = _split(w)
    return _dot(ah, wh) + _dot(al, wh) + _dot(ah, wl)


def _norm_mod(x, g, sc, sh):
    ms = jnp.mean(x * x, axis=-1, keepdims=True)
    return ((x * lax.rsqrt(ms + EPS)) * g) * (1.0 + sc) + sh


def _norm_mod_rows(x_ref, mods, emit):
    rows = x_ref.shape[0]
    ch = min(NORM_ROWS, rows)

    def body(i, carry):
        r0 = pl.multiple_of(i * ch, ch)
        emit(r0, _norm_mod(x_ref[pl.ds(r0, ch), :], *mods()))
        return carry

    lax.fori_loop(0, rows // ch, body, 0, unroll=4 if (rows // ch) % 4 == 0 else 1)


def _batch_spec(width, tiles_per_batch):
    return pl.BlockSpec((1, 1, width), lambda i, *_: (i // tiles_per_batch, 0, 0))


def _pack_pair(lo, hi):
    lo_b = lax.bitcast_convert_type(lo.astype(BF16).astype(F32), I32)
    hi_b = lax.bitcast_convert_type(hi.astype(BF16).astype(F32), I32)
    return lax.shift_right_logical(lo_b, 16) | (hi_b & HI16)


def _unpack_pair(p):
    lo = lax.bitcast_convert_type(lax.shift_left(p, 16), F32)
    hi = lax.bitcast_convert_type(p & HI16, F32)
    return lo, hi


def _store_tiles(ref, row0, val):
    rows, width = val.shape
    pcs = width // LANES
    for jj in range(pcs):
        ref[pl.ds(row0 * pcs + jj, rows, stride=pcs), :] = val[:, jj * LANES:(jj + 1) * LANES]


def _load_tiles(ref, row0, rows, pcs):
    return jnp.concatenate([ref[pl.ds(row0 * pcs + jj, rows, stride=pcs), :] for jj in range(pcs)], axis=1)


def _ada_kernel(c_ref, w_ref, b_ref, o_ref):
    c = c_ref[...]
    o_ref[0] = _dot3(c * jax.nn.sigmoid(c), w_ref[0]) + b_ref[0]


def _ada_mod(c, ada_w, ada_b):
    depth, d, nd = ada_w.shape
    nb = c.shape[0]
    tn = _tile(nd, 768, LANES)
    return pl.pallas_call(
        _ada_kernel,
        grid=(depth, nd // tn),
        in_specs=[pl.BlockSpec((nb, d), lambda i, j: (0, 0)),
                  pl.BlockSpec((1, d, tn), lambda i, j: (i, 0, j)),
                  pl.BlockSpec((1, 1, tn), lambda i, j: (i, 0, j))],
        out_specs=pl.BlockSpec((1, nb, tn), lambda i, j: (i, 0, j)),
        out_shape=jax.ShapeDtypeStruct((depth, nb, nd), F32),
        compiler_params=_params("parallel", "parallel"),
        name="ada_mod",
    )(c, ada_w, ada_b.reshape(depth, 1, nd))


def _s5_kernel(x_ref, g_ref, sc_ref, sh_ref, bmat_ref, cmat_ref, lre_ref, lim_ref, d_ref,
               z_ref, h_s, st_s, bu_s, sa_s, z_s, *, tc, nb, hw):
    c = pl.program_id(0)
    j = pl.program_id(1)
    n_j = h_s.shape[0]

    @pl.when(j == 0)
    def _():
        for b in range(nb):
            def emit(r0, h, b=b):
                for jj in range(n_j):
                    h_s[jj, pl.ds(r0 * nb + b, h.shape[0], stride=nb), :] = h[:, jj * LANES:(jj + 1) * LANES]

            _norm_mod_rows(x_ref.at[b], lambda b=b: (g_ref[...], sc_ref[b], sh_ref[b]), emit)

    @pl.when(c == 0)
    def _():
        st_s[j] = jnp.zeros(st_s.shape[1:], F32)

    n_sub = bu_s.shape[0]
    ts = tc // n_sub
    rs = ts * nb
    bmat = bmat_ref[0]
    cmat = cmat_ref[0]
    for k in range(n_sub):
        bu_s[k] = _dot(h_s[j, k * rs:(k + 1) * rs, :].astype(BF16), bmat)
    lre = jnp.broadcast_to(lre_ref[0], (nb, hw))
    lim = jnp.broadcast_to(lim_ref[0], (nb, hw))
    st = st_s[j]
    s_re, s_im = st[:, :hw], st[:, hw:]
    for k in range(n_sub):
        for t in range(ts):
            b = bu_s[k, t * nb:(t + 1) * nb, :]
            s_re, s_im = (lre * s_re - lim * s_im + b[:, :hw],
                          lre * s_im + lim * s_re + b[:, hw:])
            sa_s[k, t * nb:(t + 1) * nb, :hw] = s_re
            sa_s[k, t * nb:(t + 1) * nb, hw:] = s_im
        y = _dot(sa_s[k].astype(BF16), cmat) + d_ref[...] * h_s[j, k * rs:(k + 1) * rs, :]
        z = 0.5 * y * (1.0 + jnp.tanh(GELU_C0 * (y + GELU_C1 * (y * y * y))))
        z_s[k * rs:(k + 1) * rs, :] = z
    st_s[j, :, :hw] = s_re
    st_s[j, :, hw:] = s_im
    for b in range(nb):
        z_ref[b] = z_s[pl.ds(b, tc, stride=nb), :].astype(BF16)


def _s5_prep(lam_re, lam_im, log_dt, b_re, b_im, c_re, c_im):
    n_g, n_p = lam_re.shape
    n_h = b_re.shape[-1]
    gpt = LANES // n_h
    n_j = n_g // gpt
    lam = lax.complex(jnp.minimum(lam_re.astype(F32), S5_RE_MAX), lam_im.astype(F32))
    dt = jnp.exp(log_dt.astype(F32))[:, None]
    lam_bar = jnp.exp(lam * dt)
    b_bar = ((lam_bar - 1.0) / lam)[:, :, None] * lax.complex(b_re.astype(F32), b_im.astype(F32))
    eye = jnp.eye(gpt, dtype=F32)
    bb = b_bar.reshape(n_j, gpt, n_p, n_h)
    bre = jnp.einsum('jkph,gk->jghkp', bb.real, eye)
    bim = jnp.einsum('jkph,gk->jghkp', bb.imag, eye)
    bmat = jnp.stack([bre, bim], axis=3).reshape(n_j, LANES, 2 * gpt * n_p)
    cre = jnp.einsum('jghp,kg->jkpgh', c_re.astype(F32).reshape(n_j, gpt, n_h, n_p), eye)
    cim = jnp.einsum('jghp,kg->jkpgh', c_im.astype(F32).reshape(n_j, gpt, n_h, n_p), eye)
    cmat = jnp.stack([cre, -cim], axis=1).reshape(n_j, 2 * gpt * n_p, LANES)
    lre = lam_bar.real.reshape(n_j, 1, gpt * n_p)
    lim = lam_bar.imag.reshape(n_j, 1, gpt * n_p)
    return bmat.astype(BF16), cmat.astype(BF16), lre, lim


def _s5_scan(x, g, sc, sh, bmat, cmat, lre, lim, d_skip):
    nb, seq, d = x.shape
    n_j = d // LANES
    hw = lre.shape[-1]
    tc = _tile(seq, S5_CHUNK, 8)
    rows = tc * nb
    n_sub = S5_SUB if tc % S5_SUB == 0 else 1
    kern = functools.partial(_s5_kernel, tc=tc, nb=nb, hw=hw)
    return pl.pallas_call(
        kern,
        grid=(seq // tc, n_j),
        in_specs=[pl.BlockSpec((nb, tc, d), lambda c, j: (0, c, 0)),
                  pl.BlockSpec((1, d), lambda c, j: (0, 0)),
                  pl.BlockSpec((nb, 1, d), lambda c, j: (0, 0, 0)),
                  pl.BlockSpec((nb, 1, d), lambda c, j: (0, 0, 0)),
                  pl.BlockSpec((1, LANES, 2 * hw), lambda c, j: (j, 0, 0)),
                  pl.BlockSpec((1, 2 * hw, LANES), lambda c, j: (j, 0, 0)),
                  pl.BlockSpec((1, 1, hw), lambda c, j: (j, 0, 0)),
                  pl.BlockSpec((1, 1, hw), lambda c, j: (j, 0, 0)),
                  pl.BlockSpec((1, LANES), lambda c, j: (0, j))],
        out_specs=pl.BlockSpec((nb, tc, LANES), lambda c, j: (0, c, j)),
        out_shape=jax.ShapeDtypeStruct((nb, seq, d), BF16),
        scratch_shapes=[pltpu.VMEM((n_j, rows, LANES), F32),
                        pltpu.VMEM((n_j, nb, 2 * hw), F32),
                        pltpu.VMEM((n_sub, rows // n_sub, 2 * hw), F32),
                        pltpu.VMEM((n_sub, rows // n_sub, 2 * hw), F32),
                        pltpu.VMEM((rows, LANES), F32)],
        compiler_params=_params("arbitrary", "arbitrary"),
        name="s5_scan",
    )(x, g, sc, sh, bmat, cmat, lre, lim, d_skip)


def _mm_res_kernel(a_ref, w_ref, x_ref, g_ref, o_ref, *, glu, tn):
    a = a_ref[...]
    d = o_ref.shape[1]
    for c0 in range(0, d, tn):
        p = _dot(a, w_ref[:, c0:c0 + tn])
        if glu:
            p = p * jax.nn.sigmoid(_dot(a, w_ref[:, d + c0:d + c0 + tn]))
        o_ref[:, c0:c0 + tn] = x_ref[:, c0:c0 + tn] + p * g_ref[0, :, c0:c0 + tn]


def _mm_res(a, w, x, gate, glu, seq):
    n, kdim = a.shape
    d = x.shape[1]
    tm = _tile(seq, 512, 16)
    tn = _tile(d, 512, LANES)
    return pl.pallas_call(
        functools.partial(_mm_res_kernel, glu=glu, tn=tn),
        grid=(n // tm,),
        in_specs=[pl.BlockSpec((tm, kdim), lambda i: (i, 0)),
                  pl.BlockSpec(w.shape, lambda i: (0, 0), pipeline_mode=pl.Buffered(1)),
                  pl.BlockSpec((tm, d), lambda i: (i, 0)),
                  _batch_spec(d, seq // tm)],
        out_specs=pl.BlockSpec((tm, d), lambda i: (i, 0)),
        out_shape=jax.ShapeDtypeStruct((n, d), F32),
        compiler_params=_params("parallel"),
        name="glu_res" if glu else "proj_res",
    )(a, w, x, gate)


def _hg_norm_kernel(x_ref, g_ref, sc_ref, sh_ref, o_ref):
    def emit(r0, h):
        o_ref[pl.ds(r0, h.shape[0]), :] = h.astype(BF16)

    _norm_mod_rows(x_ref, lambda: (g_ref[...], sc_ref[0], sh_ref[0]), emit)


def _hg_norm(x, g, sc, sh, seq):
    n, d = x.shape
    tm = _tile(seq, 1024, 16)
    bspec = _batch_spec(d, seq // tm)
    return pl.pallas_call(
        _hg_norm_kernel,
        grid=(n // tm,),
        in_specs=[pl.BlockSpec((tm, d), lambda i: (i, 0)), pl.BlockSpec((1, d), lambda i: (0, 0)), bspec, bspec],
        out_specs=pl.BlockSpec((tm, d), lambda i: (i, 0)),
        out_shape=jax.ShapeDtypeStruct((n, d), BF16),
        compiler_params=_params("parallel"),
        name="hg_norm",
    )(x, g, sc, sh)


def _hg_levels(ch):
    levels = []
    m = ch // 2
    while m >= 1:
        levels.append(m)
        m //= 2
    return levels


def _hg_masks(ch):
    levels = _hg_levels(ch)
    mk = np.zeros((len(levels) + 1, ch, ch), np.float32)
    for li, m in enumerate(levels):
        for t in range(ch):
            mid = (t // (2 * m)) * 2 * m + m - 1
            if t > mid:
                mk[li, t, mid - m + 1:mid + 1] = 1.0
    mk[-1] = np.eye(ch, dtype=np.float32)
    return mk


def _cumsum_rows(x, t_idx):
    sh = 1
    while sh < x.shape[0]:
        x = x + jnp.where(t_idx >= sh, pltpu.roll(x, sh, 0), 0.0)
        sh *= 2
    return x


def _level_ref(bc, m, sub3):
    ch, dk = bc.shape
    if 2 * m >= SUBLANES:
        bp = bc.reshape(ch // (2 * m), 2 * m, dk)
        return jnp.broadcast_to(bp[:, m - 1:m, :], bp.shape).reshape(ch, dk)
    b3 = bc.reshape(ch // SUBLANES, SUBLANES, dk)
    r = None
    for start in range(0, SUBLANES, 2 * m):
        cand = jnp.broadcast_to(b3[:, start + m - 1:start + m, :], b3.shape)
        r = cand if r is None else jnp.where(sub3 >= start, cand, r)
    return r.reshape(ch, dk)


def _hg_mix_kernel(h0_ref, hb_ref, hc_ref, w_ref, la_ref, l1_ref, mask_ref, gn_ref, o_ref, st_s, pa_s, pb_s,
                   *, nb, ch):
    d = h0_ref.shape[-1]

    def project(h_ref, c0, width):
        return _dot(h_ref[...].reshape(nb * ch, d), w_ref[0, :, c0:c0 + width])

    @pl.when(pl.program_id(1) == 0)
    def _():
        st_s[...] = jnp.zeros(st_s.shape, F32)
        pa_s[...] = project(h0_ref, 0, pa_s.shape[-1])

    args = (w_ref, la_ref, l1_ref, mask_ref, gn_ref, o_ref, st_s)
    _hg_chunk(hb_ref, *args, pa_s, pb_s, project, nb, ch, 0)
    _hg_chunk(hc_ref, *args, pb_s, pa_s, project, nb, ch, ch)


def _hg_chunk(hn_ref, w_ref, la_ref, l1_ref, mask_ref, gn_ref, o_ref, st_s, p_s, pn_s, project, nb, ch, r0):
    dk = o_ref.shape[-1]
    cols = pn_s.shape[-1]
    assert nb % HG_GROUP == 0 and cols % (nb // HG_GROUP * LANES) == 0
    cpg = cols // (nb // HG_GROUP)

    la = la_ref[0]
    l1 = l1_ref[0]
    levels = _hg_levels(ch)
    n_lv = len(levels)
    gn = gn_ref[...]
    t_idx = lax.broadcasted_iota(I32, (ch, dk), 0)
    sub3 = lax.broadcasted_iota(I32, (ch // SUBLANES, SUBLANES, dk), 1)
    masks = [mask_ref[li] > 0.5 for li in range(n_lv + 1)]

    for b0 in range(0, nb, HG_GROUP):
        grp = range(b0, min(b0 + HG_GROUP, nb))
        c0 = (b0 // HG_GROUP) * cpg
        pn_s[:, c0:c0 + cpg] = project(hn_ref, c0, cpg)
        qk, qms, kms, q_in, k_out, decay = {}, {}, {}, {}, {}, {}
        for b in grp:
            q = p_s[b * ch:(b + 1) * ch, 0:dk]
            z = p_s[b * ch:(b + 1) * ch, dk:2 * dk]
            bt = l1 + jnp.minimum(z, 0.0) - jnp.log(1.0 + jnp.exp(-jnp.abs(z)))
            lf = jnp.maximum(la, bt) + jnp.log(1.0 + jnp.exp(-jnp.abs(la - bt)))
            bc = _cumsum_rows(lf, t_idx)
            qb = q.astype(BF16)
            kb = (1.0 - jnp.exp(lf)).astype(BF16)
            qk[b] = (qb, kb)
            qms[b], kms[b] = [], []
            for m in levels:
                e = jnp.exp(-jnp.abs(bc - _level_ref(bc, m, sub3))).astype(BF16)
                qms[b].append(qb * e)
                kms[b].append(kb * e)
            b_end = bc[ch - 1:ch, :]
            q_in[b] = qb * jnp.exp(bc).astype(BF16)
            k_out[b] = kb * jnp.exp(b_end - bc).astype(BF16)
            decay[b] = jnp.exp(b_end)
        scores = {}
        for b in grp:
            s = jnp.where(masks[n_lv], _dot_nt(*qk[b]), 0.0)
            for li in range(n_lv):
                s = jnp.where(masks[li], _dot_nt(qms[b][li], kms[b][li]), s)
            scores[b] = s.astype(BF16)
        outs = {}
        for b in grp:
            v = p_s[b * ch:(b + 1) * ch, 2 * dk:3 * dk].astype(BF16)
            st = st_s[b]
            outs[b] = _dot(scores[b], v) + _dot_nt(q_in[b], st.astype(BF16))
            st_s[b] = decay[b] * st + _dot_tn(v, k_out[b])
        for b in grp:
            o = outs[b]
            gate = p_s[b * ch:(b + 1) * ch, 3 * dk:4 * dk]
            o = o * lax.rsqrt(jnp.mean(o * o, axis=-1, keepdims=True) + EPS) * gn
            o_ref[b, r0:r0 + ch, :] = (o * (gate * jax.nn.sigmoid(gate))).astype(BF16)


def _hg_mix(h, w_heads, log_lb, log1m_lb, g_norm):
    nb, seq, d = h.shape
    heads, _, cols = w_heads.shape
    dk = cols // 4
    ch = _tile(seq, HG_CHUNK, 16)
    nc = seq // ch
    assert nc % 2 == 0, "two chunks per grid step"
    mk = _hg_masks(ch)
    per_head = lambda w: pl.BlockSpec((1,) + w.shape[1:], lambda hd, s: (hd, 0, 0))
    return pl.pallas_call(
        functools.partial(_hg_mix_kernel, nb=nb, ch=ch),
        grid=(heads, nc // 2),
        in_specs=[pl.BlockSpec((nb, ch, d), lambda hd, s: (0, 0, 0)),
                  pl.BlockSpec((nb, ch, d), lambda hd, s: (0, 2 * s + 1, 0)),
                  pl.BlockSpec((nb, ch, d), lambda hd, s: (0, jnp.minimum(2 * s + 2, nc - 1), 0)),
                  per_head(w_heads), per_head(log_lb), per_head(log1m_lb),
                  pl.BlockSpec(mk.shape, lambda hd, s: (0, 0, 0)),
                  pl.BlockSpec((1, dk), lambda hd, s: (0, 0))],
        out_specs=pl.BlockSpec((nb, 2 * ch, dk), lambda hd, s: (0, s, hd)),
        out_shape=jax.ShapeDtypeStruct((nb, seq, heads * dk), BF16),
        scratch_shapes=[pltpu.VMEM((nb, dk, dk), F32), pltpu.VMEM((nb * ch, cols), F32),
                        pltpu.VMEM((nb * ch, cols), F32)],
        compiler_params=_params("parallel", "arbitrary"),
        name="hg_mix",
    )(h, h, h, w_heads, log_lb, log1m_lb, jnp.asarray(mk, F32), g_norm)


def _router_kernel(x_ref, g_ref, sc_ref, sh_ref, wh_ref, wl_ref, br_ref, tri_ref,
                   hp_ref, idx_ref, gate_ref, rank_ref, cnt_ref):
    h = _norm_mod(x_ref[...], g_ref[...], sc_ref[0], sh_ref[0])
    half = h.shape[1] // 2
    _store_tiles(hp_ref, 0, _pack_pair(h[:, :half], h[:, half:]))
    hh, hl = _split(h)
    wh = wh_ref[...]
    vals = _dot(hh, wh) + _dot(hl, wh) + _dot(hh, wl_ref[...]) + br_ref[...]
    tm, n_exp = vals.shape
    lane = lax.broadcasted_iota(I32, (tm, n_exp), 1)
    tops, idxs, hots = [], [], []
    for _ in range(TOP_K):
        m = jnp.max(vals, axis=-1, keepdims=True)
        i = jnp.min(jnp.where(vals == m, lane, n_exp), axis=-1, keepdims=True)
        tops.append(m)
        idxs.append(i)
        hots.append(lane == i)
        vals = jnp.where(hots[-1], -jnp.inf, vals)
    es = [jnp.exp(t - tops[0]) for t in tops]
    den = es[0] + es[1] + es[2] + es[3]

    sel = [jnp.where(hm, 1.0, 0.0) for hm in hots]
    multi = sel[0] + sel[1] + sel[2] + sel[3]
    base = _dot(tri_ref[...], multi.astype(BF16))
    cnt_ref[0] = jnp.sum(multi, axis=0, keepdims=True)

    lane_o = lax.broadcasted_iota(I32, idx_ref.shape, 1)
    io = jnp.zeros(idx_ref.shape, I32)
    ro = jnp.zeros(idx_ref.shape, I32)
    go = jnp.zeros(gate_ref.shape, F32)
    for kk in range(TOP_K):
        rk = jnp.sum(sel[kk] * base, axis=-1, keepdims=True).astype(I32)
        io = jnp.where(lane_o == kk, idxs[kk], io)
        ro = jnp.where(lane_o == kk, rk, ro)
        go = jnp.where(lane_o == kk, es[kk] / den, go)
    idx_ref[...] = io
    rank_ref[...] = ro
    gate_ref[...] = go


def _router(x, g, sc, sh, w_r, b_r, seq):
    n, d = x.shape
    n_exp = w_r.shape[1]
    tm = _tile(seq, 512, 16)
    bspec = _batch_spec(d, seq // tm)
    pcs = d // 2 // LANES
    wh, wl = _split(w_r.astype(F32))
    tri = jnp.asarray(np.tril(np.ones((tm, tm), np.float32), -1), BF16)
    row = lambda w: pl.BlockSpec((tm, w), lambda i: (i, 0))
    full = lambda a: pl.BlockSpec(a.shape, lambda i: (0, 0))
    b2 = b_r.astype(F32).reshape(1, n_exp)
    return pl.pallas_call(
        _router_kernel,
        grid=(n // tm,),
        in_specs=[row(d), full(g), bspec, bspec, full(wh), full(wl), full(b2), full(tri)],
        out_specs=[pl.BlockSpec((tm * pcs, LANES), lambda i: (i, 0)), row(LANES), row(LANES), row(LANES),
                   pl.BlockSpec((1, 1, n_exp), lambda i: (i, 0, 0))],
        out_shape=[jax.ShapeDtypeStruct((n * pcs, LANES), I32),
                   jax.ShapeDtypeStruct((n, LANES), I32),
                   jax.ShapeDtypeStruct((n, LANES), F32),
                   jax.ShapeDtypeStruct((n, LANES), I32),
                   jax.ShapeDtypeStruct((n // tm, 1, n_exp), F32)],
        compiler_params=_params("parallel"),
        name="router",
    )(x, g, sc, sh, wh, wl, b2, tri)


def _moe_plan(top_idx, lrank, tile_counts, tmb):
    n = top_idx.shape[0]
    n_tiles, _, n_exp = tile_counts.shape
    tc = tile_counts.reshape(n_tiles, n_exp).astype(I32)
    counts = jnp.sum(tc, axis=0)
    nblk = (counts + tmb - 1) // tmb
    blk_end = jnp.cumsum(nblk)
    blk_start = blk_end - nblk
    n_used = blk_end[-1]
    n_blocks = n * TOP_K // tmb + n_exp
    blk = jnp.arange(n_blocks, dtype=I32)
    be = jnp.minimum(jnp.sum((blk[:, None] >= blk_end[None, :]).astype(I32), axis=1), n_exp - 1)
    last = jnp.sum(jnp.where(blk == n_used - 1, be, 0))
    be = jnp.where(blk < n_used, be, last).astype(I32)
    base = blk_start * tmb
    seg_dst = base[None, :] + jnp.cumsum(tc, axis=0) - tc
    seg_off = jnp.cumsum(tc, axis=1) - tc
    tm = n // n_tiles
    hot = top_idx.reshape(n_tiles, tm, TOP_K, 1) == jnp.arange(n_exp, dtype=I32)
    lpos = jnp.sum(jnp.where(hot, seg_off[:, None, None, :], 0), axis=-1) + lrank.reshape(n_tiles, tm, TOP_K)
    pad_start = base + counts
    pad_len = blk_end * tmb - pad_start
    flat = lambda a: a.reshape(-1).astype(I32)
    return dict(be=be, n_used=n_used.reshape(1).astype(I32),
                lpos=lpos.reshape(n_tiles, 1, tm * TOP_K).astype(I32),
                seg_cnt=flat(tc), seg_off=flat(seg_off), seg_dst=flat(seg_dst),
                pad_start=pad_start.astype(I32), pad_len=pad_len.astype(I32),
                n_blocks=n_blocks, tmb=tmb, n_tiles=n_tiles, n_exp=n_exp)


def _tile_at(ref, row, pcs):
    return ref.at[pl.ds(pl.multiple_of(row * pcs, pcs), pcs)]


def _segments(i, cnt_ref, off_ref, dst_ref, n_exp, pcs, copy):
    def seg(e, carry):
        cnt = cnt_ref[i * n_exp + e]

        @pl.when(cnt > 0)
        def _():
            size = pl.multiple_of(cnt * pcs, pcs)
            copy(pl.ds(pl.multiple_of(off_ref[i * n_exp + e] * pcs, pcs), size),
                 pl.ds(pl.multiple_of(dst_ref[i * n_exp + e] * pcs, pcs), size))

        return carry

    lax.fori_loop(0, n_exp, seg, 0)


def _dispatch_kernel(sc_ref, so_ref, sd_ref, ps_ref, pn_ref, nu_ref, lpos_ref, hp_ref, xs_hbm,
                     sbuf, zero_s, sems, sem, *, tmb, pcs):
    i = pl.program_id(0)
    last = pl.num_programs(0) - 1
    tm = hp_ref.shape[0] // pcs
    slot = i % 2

    def drain(s):
        pltpu.make_async_copy(sbuf.at[s], xs_hbm.at[pl.ds(0, sbuf.shape[1])], sems.at[s]).wait()

    @pl.when(i >= 2)
    def _():
        drain(slot)

    def row(r, carry):
        tile = hp_ref[pl.ds(pl.multiple_of(r * pcs, pcs), pcs), :]
        for kk in range(TOP_K):
            p = lpos_ref[0, 0, TOP_K * r + kk]
            sbuf[slot, pl.ds(pl.multiple_of(p * pcs, pcs), pcs), :] = tile
        return carry

    lax.fori_loop(0, tm, row, 0, unroll=8)

    def copy(src, dst):
        pltpu.make_async_copy(sbuf.at[slot, src], xs_hbm.at[dst], sems.at[slot]).start()

    _segments(i, sc_ref, so_ref, sd_ref, ps_ref.shape[0], pcs, copy)

    @pl.when(i == last)
    def _():
        drain(slot)

        @pl.when(i >= 1)
        def _():
            drain(1 - slot)

        zero_s[...] = jnp.zeros(zero_s.shape, I32)

        size = tmb // 2
        while size >= 1:
            def chunk(e, size=size):
                cnt = pn_ref[e]
                before = cnt - cnt % (2 * size)
                dst = pl.ds(pl.multiple_of((ps_ref[e] + before) * pcs, pcs), size * pcs)
                return (cnt // size) % 2 == 1, pltpu.make_async_copy(
                    zero_s.at[pl.ds(0, size * pcs)], xs_hbm.at[dst], sem)

            def fill(e, carry):
                take, cp = chunk(e)

                @pl.when(take)
                def _():
                    cp.start()

                return carry

            def fill_done(e, carry):
                take, cp = chunk(e)

                @pl.when(take)
                def _():
                    cp.wait()

                return carry

            lax.fori_loop(0, ps_ref.shape[0], fill, 0)
            lax.fori_loop(0, ps_ref.shape[0], fill_done, 0)
            size //= 2

        def block(bk, carry):
            cp = pltpu.make_async_copy(zero_s, _tile_at(xs_hbm, bk, tmb * pcs), sem)
            cp.start()
            cp.wait()
            return carry

        lax.fori_loop(nu_ref[0], xs_hbm.shape[0] // (tmb * pcs), block, 0)


def _dispatch(hp, plan, n):
    pcs = hp.shape[0] // n
    tmb = plan['tmb']
    n_tiles = plan['n_tiles']
    tm = n // n_tiles
    grid_spec = pltpu.PrefetchScalarGridSpec(
        num_scalar_prefetch=6,
        grid=(n_tiles,),
        in_specs=[pl.BlockSpec((1, 1, TOP_K * tm), lambda i, *_: (i, 0, 0), memory_space=pltpu.SMEM),
                  pl.BlockSpec((tm * pcs, LANES), lambda i, *_: (i, 0))],
        out_specs=pl.BlockSpec(memory_space=pl.ANY),
        scratch_shapes=[pltpu.VMEM((2, TOP_K * tm * pcs, LANES), I32),
                        pltpu.VMEM((tmb * pcs, LANES), I32),
                        pltpu.SemaphoreType.DMA((2,)), pltpu.SemaphoreType.DMA(())],
    )
    return pl.pallas_call(
        functools.partial(_dispatch_kernel, tmb=tmb, pcs=pcs),
        grid_spec=grid_spec,
        out_shape=jax.ShapeDtypeStruct((plan['n_blocks'] * tmb * pcs, LANES), I32),
        compiler_params=_params("arbitrary"),
        name="moe_dispatch",
    )(plan['seg_cnt'], plan['seg_off'], plan['seg_dst'], plan['pad_start'], plan['pad_len'], plan['n_used'],
      plan['lpos'], hp)


def _moe_kernel(be_ref, nu_ref, x_ref, wgu_ref, bgu_ref, wdn_ref, bdn_ref, y_ref, wgu_s, wdn_s, *, ff):
    i = pl.program_id(0)
    used = i < nu_ref[0]

    @pl.when(used)
    def _():
        e_new = jnp.logical_or(i == 0, be_ref[i] != be_ref[jnp.maximum(i - 1, 0)])

        @pl.when(e_new)
        def _():
            wgu_s[...] = wgu_ref[0, 0].astype(BF16)
            wdn_s[...] = wdn_ref[0, 0].astype(BF16)

        half = wgu_s.shape[0] // 2
        pcs = half // LANES
        tmb = x_ref.shape[0] // pcs
        x_lo, x_hi = _unpack_pair(_load_tiles(x_ref, 0, tmb, pcs))
        gu = (_dot(x_lo.astype(BF16), wgu_s[:half, :]) + _dot(x_hi.astype(BF16), wgu_s[half:, :])
              + bgu_ref[0, 0])
        gt = jnp.minimum(gu[:, :ff], SWIGLU_LIMIT)
        up = jnp.clip(gu[:, ff:], -SWIGLU_LIMIT, SWIGLU_LIMIT)
        act = ((up + 1.0) * gt * jax.nn.sigmoid(SWIGLU_ALPHA * gt)).astype(BF16)
        bdn = bdn_ref[0, 0]
        y_lo = _dot(act, wdn_s[:, :half]) + bdn[:, :half]
        y_hi = _dot(act, wdn_s[:, half:]) + bdn[:, half:]
        _store_tiles(y_ref, 0, _pack_pair(y_lo, y_hi))

    @pl.when(jnp.logical_not(used))
    def _():
        y_ref[...] = jnp.zeros(y_ref.shape, I32)


def _moe_experts(xs, plan, w_gu, b_gu, w_dn, b_dn, layer):
    depth, n_exp, d, ff2 = w_gu.shape
    ff = ff2 // 2
    tmb = plan['tmb']
    n_blocks = plan['n_blocks']
    rows = xs.shape[0] // n_blocks
    wmap = lambda i, be, nu: (layer, be[i], 0, 0)
    grid_spec = pltpu.PrefetchScalarGridSpec(
        num_scalar_prefetch=2,
        grid=(n_blocks,),
        in_specs=[pl.BlockSpec((rows, LANES), lambda i, be, nu: (i, 0)),
                  pl.BlockSpec((1, 1, d, ff2), wmap),
                  pl.BlockSpec((1, 1, 1, ff2), wmap),
                  pl.BlockSpec((1, 1, ff, d), wmap),
                  pl.BlockSpec((1, 1, 1, d), wmap)],
        out_specs=pl.BlockSpec((rows, LANES), lambda i, be, nu: (i, 0)),
        scratch_shapes=[pltpu.VMEM((d, ff2), BF16), pltpu.VMEM((ff, d), BF16)],
    )
    return pl.pallas_call(
        functools.partial(_moe_kernel, ff=ff),
        grid_spec=grid_spec,
        out_shape=jax.ShapeDtypeStruct(xs.shape, I32),
        compiler_params=_params("arbitrary"),
        name="moe_experts",
    )(plan['be'], plan['n_used'], xs, w_gu, b_gu.reshape(depth, n_exp, 1, ff2),
      w_dn, b_dn.reshape(depth, n_exp, 1, d))


def _combine_kernel(sc_ref, so_ref, sd_ref, lpos_ref, x_ref, gate_ref, g_ref, nf_ref, ys_hbm, o_ref,
                    ybuf, gbuf, sems, *, n_exp, final):
    i = pl.program_id(0)
    tm, d = x_ref.shape
    half = d // 2
    pcs = half // LANES
    slot = i % 2

    def fetch(t, s):
        def copy(loc, blk):
            pltpu.make_async_copy(ys_hbm.at[blk], ybuf.at[s, loc], sems.at[s]).start()

        _segments(t, sc_ref, so_ref, sd_ref, n_exp, pcs, copy)

    @pl.when(i == 0)
    def _():
        fetch(i, slot)

    @pl.when(i + 1 < pl.num_programs(0))
    def _():
        fetch(i + 1, 1 - slot)

    pltpu.make_async_copy(ys_hbm.at[pl.ds(0, ybuf.shape[1])], ybuf.at[slot], sems.at[slot]).wait()

    def row(r, carry):
        for kk in range(TOP_K):
            p = lpos_ref[0, 0, TOP_K * r + kk]
            gbuf[kk, pl.ds(pl.multiple_of(r * pcs, pcs), pcs), :] = (
                ybuf[slot, pl.ds(pl.multiple_of(p * pcs, pcs), pcs), :])
        return carry

    lax.fori_loop(0, tm, row, 0, unroll=8)

    gt = gate_ref[...]
    lo = hi = None
    for kk in range(TOP_K):
        w = gt[:, kk:kk + 1]
        y_lo, y_hi = _unpack_pair(_load_tiles(gbuf.at[kk], 0, tm, pcs))
        lo = w * y_lo if lo is None else lo + w * y_lo
        hi = w * y_hi if hi is None else hi + w * y_hi
    g = g_ref[0]
    x_lo = x_ref[:, :half] + lo * g[:, :half]
    x_hi = x_ref[:, half:] + hi * g[:, half:]
    if final:
        ms = (jnp.sum(x_lo * x_lo, axis=-1, keepdims=True) + jnp.sum(x_hi * x_hi, axis=-1, keepdims=True)) / d
        r = lax.rsqrt(ms + EPS)
        x_lo = (x_lo * r) * nf_ref[:, :half]
        x_hi = (x_hi * r) * nf_ref[:, half:]
    o_ref[:, :half] = x_lo
    o_ref[:, half:] = x_hi


def _combine(x, ys, gates, g2, plan, seq, norm_final, final):
    n, d = x.shape
    pcs = d // 2 // LANES
    nt = plan['n_tiles']
    tm = n // nt
    grid_spec = pltpu.PrefetchScalarGridSpec(
        num_scalar_prefetch=3,
        grid=(nt,),
        in_specs=[pl.BlockSpec((1, 1, TOP_K * tm), lambda i, *_: (i, 0, 0), memory_space=pltpu.SMEM),
                  pl.BlockSpec((tm, d), lambda i, *_: (i, 0)),
                  pl.BlockSpec((tm, LANES), lambda i, *_: (i, 0)),
                  _batch_spec(d, seq // tm),
                  pl.BlockSpec((1, d), lambda i, *_: (0, 0)),
                  pl.BlockSpec(memory_space=pl.ANY)],
        out_specs=pl.BlockSpec((tm, d), lambda i, *_: (i, 0)),
        scratch_shapes=[pltpu.VMEM((2, TOP_K * tm * pcs, LANES), I32),
                        pltpu.VMEM((TOP_K, tm * pcs, LANES), I32),
                        pltpu.SemaphoreType.DMA((2,))],
    )
    return pl.pallas_call(
        functools.partial(_combine_kernel, n_exp=plan['n_exp'], final=final),
        grid_spec=grid_spec,
        out_shape=jax.ShapeDtypeStruct((n, d), F32),
        compiler_params=_params("arbitrary"),
        name="moe_combine",
    )(plan['seg_cnt'], plan['seg_off'], plan['seg_dst'], plan['lpos'], x, gates, g2, norm_final, ys)


def kernel(x, c, ada_w, ada_b, norm_mix, norm_ffn, norm_final, s5_lambda_re, s5_lambda_im, s5_log_dt,
           s5_b_re, s5_b_im, s5_c_re, s5_c_im, s5_d, s5_w_glu, hg_w_in, hg_lb_raw, hg_norm, hg_w_out,
           router_w, router_b, moe_w_gate_up, moe_b_gate_up, moe_w_down, moe_b_down):
    bsz, seq, d = x.shape
    depth = ada_w.shape[0]
    n = bsz * seq
    dv = hg_norm.shape[-1]
    heads = d // dv
    fdim = (hg_w_in.shape[-1] - 2 * d) // 2
    dk = fdim // heads
    assert dk == LANES and dv == LANES, "one HGRN2 head per 128-lane tile"
    tmb = _tile(n * TOP_K, MOE_ROWS, SUBLANES)

    lb_p = jax.nn.softmax(hg_lb_raw.astype(F32), axis=0)
    lb_all = jnp.cumsum(lb_p, axis=0) - lb_p[0]
    mod = _ada_mod(c.astype(F32), ada_w, ada_b)

    xs = x.astype(F32).reshape(n, d)
    row = lambda v: v.astype(F32).reshape(1, -1)
    for i in range(depth):
        sh1, sc1, g1, sh2, sc2, g2 = [mod[i, :, k * d:(k + 1) * d].reshape(bsz, 1, d) for k in range(N_ADA)]
        j = i // 2
        if i % 2 == 0:
            bmat, cmat, lre, lim = _s5_prep(s5_lambda_re[j], s5_lambda_im[j], s5_log_dt[j],
                                            s5_b_re[j], s5_b_im[j], s5_c_re[j], s5_c_im[j])
            z = _s5_scan(xs.reshape(bsz, seq, d), row(norm_mix[i]), sc1, sh1, bmat, cmat, lre, lim, row(s5_d[j]))
            xs = _mm_res(z.reshape(n, d), s5_w_glu[j].astype(BF16), xs, g1, True, seq)
        else:
            lb = lb_all[i].reshape(heads, 1, dk)
            w_heads = hg_w_in[j].astype(BF16).reshape(d, 4, heads, dk).transpose(2, 0, 1, 3).reshape(heads, d, 4 * dk)
            h = _hg_norm(xs, row(norm_mix[i]), sc1, sh1, seq)
            o = _hg_mix(h.reshape(bsz, seq, d), w_heads, jnp.log(lb), jnp.log1p(-lb), row(hg_norm[j]))
            xs = _mm_res(o.reshape(n, d), hg_w_out[j].astype(BF16), xs, g1, False, seq)
        hp, top_idx, gates, rank, counts = _router(xs, row(norm_ffn[i]), sc2, sh2, router_w[i], router_b[i], seq)
        plan = _moe_plan(top_idx[:, :TOP_K], rank[:, :TOP_K], counts, tmb)
        xd = _dispatch(hp, plan, n)
        ys = _moe_experts(xd, plan, moe_w_gate_up, moe_b_gate_up.astype(F32),
                          moe_w_down, moe_b_down.astype(F32), i)
        xs = _combine(xs, ys, gates, g2, plan, seq, row(norm_final), final=i == depth - 1)
    return xs.reshape(bsz, seq, d).astype(x.dtype)
```

```python
import functools

import numpy as np
import jax
import jax.numpy as jnp
from jax import lax
from jax.experimental import pallas as pl
from jax.experimental.pallas import tpu as pltpu

F32 = jnp.float32
BF16 = jnp.bfloat16
I32 = jnp.int32

EPS = 1e-6
N_ADA = 6
TOP_K = 4
S5_RE_MAX = -1e-4
SWIGLU_LIMIT = 7.0
SWIGLU_ALPHA = 1.702
GELU_C0 = 0.7978845608028654
GELU_C1 = 0.044715

LANES = 128
SUBLANES = 8
HG_CHUNK = 64
HG_GROUP = 4
S5_CHUNK = 128
S5_LANE_TILES = 2
S5_SUB = 4
NORM_ROWS = 32
MOE_ROWS = 512
HI16 = -65536
VMEM_LIMIT = 56 * 1024 * 1024


def _params(*sem):
    return pltpu.CompilerParams(dimension_semantics=sem, vmem_limit_bytes=VMEM_LIMIT)


def _tile(n, pref, align):
    if n <= pref:
        return n
    t = (pref // align) * align
    while t > align and n % t:
        t -= align
    assert n % t == 0, (n, pref, align)
    return t


def _dot(a, b):
    return jnp.dot(a, b, preferred_element_type=F32)


def _dot_nt(a, b):
    return lax.dot_general(a, b, (((1,), (1,)), ((), ())), preferred_element_type=F32)


def _dot_tn(a, b):
    return lax.dot_general(a, b, (((0,), (0,)), ((), ())), preferred_element_type=F32)


def _split(a):
    hi = a.astype(BF16)
    lo = (a - hi.astype(F32)).astype(BF16)
    return hi, lo


def _dot3(a, w):
    ah, al = _split(a)
    wh, wl = _split(w)
    return _dot(ah, wh) + _dot(al, wh) + _dot(ah, wl)


def _norm_mod(x, g, sc, sh):
    ms = jnp.mean(x * x, axis=-1, keepdims=True)
    return ((x * lax.rsqrt(ms + EPS)) * g) * (1.0 + sc) + sh


def _norm_mod_rows(x_ref, mods, emit):
    rows = x_ref.shape[0]
    ch = min(NORM_ROWS, rows)

    def body(i, carry):
        r0 = pl.multiple_of(i * ch, ch)
        emit(r0, _norm_mod(x_ref[pl.ds(r0, ch), :], *mods()))
        return carry

    lax.fori_loop(0, rows // ch, body, 0, unroll=4 if (rows // ch) % 4 == 0 else 1)


def _batch_spec(width, tiles_per_batch):
    return pl.BlockSpec((1, 1, width), lambda i, *_: (i // tiles_per_batch, 0, 0))


def _pack_pair(lo, hi):
    lo_b = lax.bitcast_convert_type(lo.astype(BF16).astype(F32), I32)
    hi_b = lax.bitcast_convert_type(hi.astype(BF16).astype(F32), I32)
    return lax.shift_right_logical(lo_b, 16) | (hi_b & HI16)


def _unpack_pair(p):
    lo = lax.bitcast_convert_type(lax.shift_left(p, 16), F32)
    hi = lax.bitcast_convert_type(p & HI16, F32)
    return lo, hi


def _store_tiles(ref, row0, val):
    rows, width = val.shape
    pcs = width // LANES
    for jj in range(pcs):
        ref[pl.ds(row0 * pcs + jj, rows, stride=pcs), :] = val[:, jj * LANES:(jj + 1) * LANES]


def _load_tiles(ref, row0, rows, pcs):
    return jnp.concatenate([ref[pl.ds(row0 * pcs + jj, rows, stride=pcs), :] for jj in range(pcs)], axis=1)


def _ada_kernel(c_ref, w_ref, b_ref, o_ref):
    c = c_ref[...]
    o_ref[0] = _dot3(c * jax.nn.sigmoid(c), w_ref[0]) + b_ref[0]


def _ada_mod(c, ada_w, ada_b):
    depth, d, nd = ada_w.shape
    nb = c.shape[0]
    tn = _tile(nd, 768, LANES)
    return pl.pallas_call(
        _ada_kernel,
        grid=(depth, nd // tn),
        in_specs=[pl.BlockSpec((nb, d), lambda i, j: (0, 0)),
                  pl.BlockSpec((1, d, tn), lambda i, j: (i, 0, j)),
                  pl.BlockSpec((1, 1, tn), lambda i, j: (i, 0, j))],
        out_specs=pl.BlockSpec((1, nb, tn), lambda i, j: (i, 0, j)),
        out_shape=jax.ShapeDtypeStruct((depth, nb, nd), F32),
        compiler_params=_params("parallel", "parallel"),
        name="ada_mod",
    )(c, ada_w, ada_b.reshape(depth, 1, nd))


def _s5_kernel(x_ref, g_ref, sc_ref, sh_ref, bmat_ref, cmat_ref, lre_ref, lim_ref, d_ref,
               z_ref, h_s, st_s, bu_s, sa_s, z_s, *, tc, nb, hw):
    c = pl.program_id(0)
    j = pl.program_id(1)
    n_j = h_s.shape[0]

    @pl.when(j == 0)
    def _():
        for b in range(nb):
            def emit(r0, h, b=b):
                for jj in range(n_j):
                    h_s[jj, pl.ds(r0 * nb + b, h.shape[0], stride=nb), :] = h[:, jj * LANES:(jj + 1) * LANES]

            _norm_mod_rows(x_ref.at[b], lambda b=b: (g_ref[...], sc_ref[b], sh_ref[b]), emit)

    n_lt, n_sub = bu_s.shape[:2]
    ts = tc // n_sub
    rs = ts * nb

    @pl.when(c == 0)
    def _():
        for l in range(n_lt):
            st_s[j * n_lt + l] = jnp.zeros(st_s.shape[1:], F32)

    for l in range(n_lt):
        bmat = bmat_ref[l]
        for k in range(n_sub):
            bu_s[l, k] = _dot(h_s[j * n_lt + l, k * rs:(k + 1) * rs, :].astype(BF16), bmat)
    for l in range(n_lt):
        jl = j * n_lt + l
        cmat = cmat_ref[l]
        lre = jnp.broadcast_to(lre_ref[l], (nb, hw))
        lim = jnp.broadcast_to(lim_ref[l], (nb, hw))
        st = st_s[jl]
        s_re, s_im = st[:, :hw], st[:, hw:]
        for k in range(n_sub):
            for t in range(ts):
                b = bu_s[l, k, t * nb:(t + 1) * nb, :]
                s_re, s_im = (lre * s_re - lim * s_im + b[:, :hw],
                              lre * s_im + lim * s_re + b[:, hw:])
                sa_s[l, k, t * nb:(t + 1) * nb, :hw] = s_re
                sa_s[l, k, t * nb:(t + 1) * nb, hw:] = s_im
            y = (_dot(sa_s[l, k].astype(BF16), cmat)
                 + d_ref[:, l * LANES:(l + 1) * LANES] * h_s[jl, k * rs:(k + 1) * rs, :])
            z = 0.5 * y * (1.0 + jnp.tanh(GELU_C0 * (y + GELU_C1 * (y * y * y))))
            z_s[l, k * rs:(k + 1) * rs, :] = z
        st_s[jl, :, :hw] = s_re
        st_s[jl, :, hw:] = s_im
    for b in range(nb):
        for l in range(n_lt):
            z_ref[b, :, l * LANES:(l + 1) * LANES] = z_s[l, pl.ds(b, tc, stride=nb), :].astype(BF16)


def _s5_prep(lam_re, lam_im, log_dt, b_re, b_im, c_re, c_im):
    n_g, n_p = lam_re.shape
    n_h = b_re.shape[-1]
    gpt = LANES // n_h
    n_j = n_g // gpt
    lam = lax.complex(jnp.minimum(lam_re.astype(F32), S5_RE_MAX), lam_im.astype(F32))
    dt = jnp.exp(log_dt.astype(F32))[:, None]
    lam_bar = jnp.exp(lam * dt)
    b_bar = ((lam_bar - 1.0) / lam)[:, :, None] * lax.complex(b_re.astype(F32), b_im.astype(F32))
    eye = jnp.eye(gpt, dtype=F32)
    bb = b_bar.reshape(n_j, gpt, n_p, n_h)
    bre = jnp.einsum('jkph,gk->jghkp', bb.real, eye)
    bim = jnp.einsum('jkph,gk->jghkp', bb.imag, eye)
    bmat = jnp.stack([bre, bim], axis=3).reshape(n_j, LANES, 2 * gpt * n_p)
    cre = jnp.einsum('jghp,kg->jkpgh', c_re.astype(F32).reshape(n_j, gpt, n_h, n_p), eye)
    cim = jnp.einsum('jghp,kg->jkpgh', c_im.astype(F32).reshape(n_j, gpt, n_h, n_p), eye)
    cmat = jnp.stack([cre, -cim], axis=1).reshape(n_j, 2 * gpt * n_p, LANES)
    lre = lam_bar.real.reshape(n_j, 1, gpt * n_p)
    lim = lam_bar.imag.reshape(n_j, 1, gpt * n_p)
    return bmat.astype(BF16), cmat.astype(BF16), lre, lim


def _s5_scan(x, g, sc, sh, bmat, cmat, lre, lim, d_skip):
    nb, seq, d = x.shape
    n_j = d // LANES
    hw = lre.shape[-1]
    tc = _tile(seq, S5_CHUNK, 8)
    rows = tc * nb
    n_sub = S5_SUB if tc % S5_SUB == 0 else 1
    lt = S5_LANE_TILES if n_j % S5_LANE_TILES == 0 else 1
    kern = functools.partial(_s5_kernel, tc=tc, nb=nb, hw=hw)
    return pl.pallas_call(
        kern,
        grid=(seq // tc, n_j // lt),
        in_specs=[pl.BlockSpec((nb, tc, d), lambda c, j: (0, c, 0)),
                  pl.BlockSpec((1, d), lambda c, j: (0, 0)),
                  pl.BlockSpec((nb, 1, d), lambda c, j: (0, 0, 0)),
                  pl.BlockSpec((nb, 1, d), lambda c, j: (0, 0, 0)),
                  pl.BlockSpec((lt, LANES, 2 * hw), lambda c, j: (j, 0, 0)),
                  pl.BlockSpec((lt, 2 * hw, LANES), lambda c, j: (j, 0, 0)),
                  pl.BlockSpec((lt, 1, hw), lambda c, j: (j, 0, 0)),
                  pl.BlockSpec((lt, 1, hw), lambda c, j: (j, 0, 0)),
                  pl.BlockSpec((1, lt * LANES), lambda c, j: (0, j))],
        out_specs=pl.BlockSpec((nb, tc, lt * LANES), lambda c, j: (0, c, j)),
        out_shape=jax.ShapeDtypeStruct((nb, seq, d), BF16),
        scratch_shapes=[pltpu.VMEM((n_j, rows, LANES), F32),
                        pltpu.VMEM((n_j, nb, 2 * hw), F32),
                        pltpu.VMEM((lt, n_sub, rows // n_sub, 2 * hw), F32),
                        pltpu.VMEM((lt, n_sub, rows // n_sub, 2 * hw), F32),
                        pltpu.VMEM((lt, rows, LANES), F32)],
        compiler_params=_params("arbitrary", "arbitrary"),
        name="s5_scan",
    )(x, g, sc, sh, bmat, cmat, lre, lim, d_skip)


def _mm_res_kernel(a_ref, w_ref, x_ref, g_ref, o_ref, *, glu, tn):
    a = a_ref[...]
    d = o_ref.shape[1]
    for c0 in range(0, d, tn):
        p = _dot(a, w_ref[:, c0:c0 + tn])
        if glu:
            p = p * jax.nn.sigmoid(_dot(a, w_ref[:, d + c0:d + c0 + tn]))
        o_ref[:, c0:c0 + tn] = x_ref[:, c0:c0 + tn] + p * g_ref[0, :, c0:c0 + tn]


def _mm_res(a, w, x, gate, glu, seq):
    n, kdim = a.shape
    d = x.shape[1]
    tm = _tile(seq, 512, 16)
    tn = _tile(d, 512, LANES)
    return pl.pallas_call(
        functools.partial(_mm_res_kernel, glu=glu, tn=tn),
        grid=(n // tm,),
        in_specs=[pl.BlockSpec((tm, kdim), lambda i: (i, 0)),
                  pl.BlockSpec(w.shape, lambda i: (0, 0), pipeline_mode=pl.Buffered(1)),
                  pl.BlockSpec((tm, d), lambda i: (i, 0)),
                  _batch_spec(d, seq // tm)],
        out_specs=pl.BlockSpec((tm, d), lambda i: (i, 0)),
        out_shape=jax.ShapeDtypeStruct((n, d), F32),
        compiler_params=_params("parallel"),
        name="glu_res" if glu else "proj_res",
    )(a, w, x, gate)


def _hg_norm_kernel(x_ref, g_ref, sc_ref, sh_ref, o_ref):
    def emit(r0, h):
        o_ref[pl.ds(r0, h.shape[0]), :] = h.astype(BF16)

    _norm_mod_rows(x_ref, lambda: (g_ref[...], sc_ref[0], sh_ref[0]), emit)


def _hg_norm(x, g, sc, sh, seq):
    n, d = x.shape
    tm = _tile(seq, 1024, 16)
    bspec = _batch_spec(d, seq // tm)
    return pl.pallas_call(
        _hg_norm_kernel,
        grid=(n // tm,),
        in_specs=[pl.BlockSpec((tm, d), lambda i: (i, 0)), pl.BlockSpec((1, d), lambda i: (0, 0)), bspec, bspec],
        out_specs=pl.BlockSpec((tm, d), lambda i: (i, 0)),
        out_shape=jax.ShapeDtypeStruct((n, d), BF16),
        compiler_params=_params("parallel"),
        name="hg_norm",
    )(x, g, sc, sh)


def _hg_levels(ch):
    levels = []
    m = ch // 2
    while m >= 1:
        levels.append(m)
        m //= 2
    return levels


def _hg_masks(ch):
    levels = _hg_levels(ch)
    mk = np.zeros((len(levels) + 1, ch, ch), np.float32)
    for li, m in enumerate(levels):
        for t in range(ch):
            mid = (t // (2 * m)) * 2 * m + m - 1
            if t > mid:
                mk[li, t, mid - m + 1:mid + 1] = 1.0
    mk[-1] = np.eye(ch, dtype=np.float32)
    return mk


def _cumsum_rows(x, t_idx):
    sh = 1
    while sh < x.shape[0]:
        x = x + jnp.where(t_idx >= sh, pltpu.roll(x, sh, 0), 0.0)
        sh *= 2
    return x


def _level_ref(bc, m, sub3):
    ch, dk = bc.shape
    if 2 * m >= SUBLANES:
        bp = bc.reshape(ch // (2 * m), 2 * m, dk)
        return jnp.broadcast_to(bp[:, m - 1:m, :], bp.shape).reshape(ch, dk)
    b3 = bc.reshape(ch // SUBLANES, SUBLANES, dk)
    r = None
    for start in range(0, SUBLANES, 2 * m):
        cand = jnp.broadcast_to(b3[:, start + m - 1:start + m, :], b3.shape)
        r = cand if r is None else jnp.where(sub3 >= start, cand, r)
    return r.reshape(ch, dk)


def _hg_mix_kernel(h0_ref, hb_ref, hc_ref, w_ref, la_ref, l1_ref, mask_ref, gn_ref, o_ref, st_s, pa_s, pb_s,
                   *, nb, ch):
    d = h0_ref.shape[-1]

    def project(h_ref, c0, width):
        return _dot(h_ref[...].reshape(nb * ch, d), w_ref[0, :, c0:c0 + width])

    @pl.when(pl.program_id(1) == 0)
    def _():
        st_s[...] = jnp.zeros(st_s.shape, F32)
        pa_s[...] = project(h0_ref, 0, pa_s.shape[-1])

    args = (w_ref, la_ref, l1_ref, mask_ref, gn_ref, o_ref, st_s)
    _hg_chunk(hb_ref, *args, pa_s, pb_s, project, nb, ch, 0)
    _hg_chunk(hc_ref, *args, pb_s, pa_s, project, nb, ch, ch)


def _hg_chunk(hn_ref, w_ref, la_ref, l1_ref, mask_ref, gn_ref, o_ref, st_s, p_s, pn_s, project, nb, ch, r0):
    dk = o_ref.shape[-1]
    cols = pn_s.shape[-1]
    assert nb % HG_GROUP == 0 and cols % (nb // HG_GROUP * LANES) == 0
    cpg = cols // (nb // HG_GROUP)

    la = la_ref[0]
    l1 = l1_ref[0]
    levels = _hg_levels(ch)
    n_lv = len(levels)
    gn = gn_ref[...]
    t_idx = lax.broadcasted_iota(I32, (ch, dk), 0)
    sub3 = lax.broadcasted_iota(I32, (ch // SUBLANES, SUBLANES, dk), 1)
    masks = [mask_ref[li] > 0.5 for li in range(n_lv + 1)]

    for b0 in range(0, nb, HG_GROUP):
        grp = range(b0, min(b0 + HG_GROUP, nb))
        c0 = (b0 // HG_GROUP) * cpg
        pn_s[:, c0:c0 + cpg] = project(hn_ref, c0, cpg)
        qk, qms, kms, q_in, k_out, decay = {}, {}, {}, {}, {}, {}
        for b in grp:
            q = p_s[b * ch:(b + 1) * ch, 0:dk]
            z = p_s[b * ch:(b + 1) * ch, dk:2 * dk]
            bt = l1 + jnp.minimum(z, 0.0) - jnp.log(1.0 + jnp.exp(-jnp.abs(z)))
            lf = jnp.maximum(la, bt) + jnp.log(1.0 + jnp.exp(-jnp.abs(la - bt)))
            bc = _cumsum_rows(lf, t_idx)
            qb = q.astype(BF16)
            kb = (1.0 - jnp.exp(lf)).astype(BF16)
            qk[b] = (qb, kb)
            qms[b], kms[b] = [], []
            for m in levels:
                e = jnp.exp(-jnp.abs(bc - _level_ref(bc, m, sub3))).astype(BF16)
                qms[b].append(qb * e)
                kms[b].append(kb * e)
            b_end = bc[ch - 1:ch, :]
            q_in[b] = qb * jnp.exp(bc).astype(BF16)
            k_out[b] = kb * jnp.exp(b_end - bc).astype(BF16)
            decay[b] = jnp.exp(b_end)
        scores = {}
        for b in grp:
            s = jnp.where(masks[n_lv], _dot_nt(*qk[b]), 0.0)
            for li in range(n_lv):
                s = jnp.where(masks[li], _dot_nt(qms[b][li], kms[b][li]), s)
            scores[b] = s.astype(BF16)
        outs = {}
        for b in grp:
            v = p_s[b * ch:(b + 1) * ch, 2 * dk:3 * dk].astype(BF16)
            st = st_s[b]
            outs[b] = _dot(scores[b], v) + _dot_nt(q_in[b], st.astype(BF16))
            st_s[b] = decay[b] * st + _dot_tn(v, k_out[b])
        for b in grp:
            o = outs[b]
            gate = p_s[b * ch:(b + 1) * ch, 3 * dk:4 * dk]
            o = o * lax.rsqrt(jnp.mean(o * o, axis=-1, keepdims=True) + EPS) * gn
            o_ref[b, r0:r0 + ch, :] = (o * (gate * jax.nn.sigmoid(gate))).astype(BF16)


def _hg_mix(h, w_heads, log_lb, log1m_lb, g_norm):
    nb, seq, d = h.shape
    heads, _, cols = w_heads.shape
    dk = cols // 4
    ch = _tile(seq, HG_CHUNK, 16)
    nc = seq // ch
    assert nc % 2 == 0, "two chunks per grid step"
    mk = _hg_masks(ch)
    per_head = lambda w: pl.BlockSpec((1,) + w.shape[1:], lambda hd, s: (hd, 0, 0))
    return pl.pallas_call(
        functools.partial(_hg_mix_kernel, nb=nb, ch=ch),
        grid=(heads, nc // 2),
        in_specs=[pl.BlockSpec((nb, ch, d), lambda hd, s: (0, 0, 0)),
                  pl.BlockSpec((nb, ch, d), lambda hd, s: (0, 2 * s + 1, 0)),
                  pl.BlockSpec((nb, ch, d), lambda hd, s: (0, jnp.minimum(2 * s + 2, nc - 1), 0)),
                  per_head(w_heads), per_head(log_lb), per_head(log1m_lb),
                  pl.BlockSpec(mk.shape, lambda hd, s: (0, 0, 0)),
                  pl.BlockSpec((1, dk), lambda hd, s: (0, 0))],
        out_specs=pl.BlockSpec((nb, 2 * ch, dk), lambda hd, s: (0, s, hd)),
        out_shape=jax.ShapeDtypeStruct((nb, seq, heads * dk), BF16),
        scratch_shapes=[pltpu.VMEM((nb, dk, dk), F32), pltpu.VMEM((nb * ch, cols), F32),
                        pltpu.VMEM((nb * ch, cols), F32)],
        compiler_params=_params("parallel", "arbitrary"),
        name="hg_mix",
    )(h, h, h, w_heads, log_lb, log1m_lb, jnp.asarray(mk, F32), g_norm)


def _router_kernel(x_ref, g_ref, sc_ref, sh_ref, wh_ref, wl_ref, br_ref, tri_ref,
                   hp_ref, idx_ref, gate_ref, rank_ref, cnt_ref):
    h = _norm_mod(x_ref[...], g_ref[...], sc_ref[0], sh_ref[0])
    half = h.shape[1] // 2
    _store_tiles(hp_ref, 0, _pack_pair(h[:, :half], h[:, half:]))
    hh, hl = _split(h)
    wh = wh_ref[...]
    vals = _dot(hh, wh) + _dot(hl, wh) + _dot(hh, wl_ref[...]) + br_ref[...]
    tm, n_exp = vals.shape
    lane = lax.broadcasted_iota(I32, (tm, n_exp), 1)
    tops, idxs, hots = [], [], []
    for _ in range(TOP_K):
        m = jnp.max(vals, axis=-1, keepdims=True)
        i = jnp.min(jnp.where(vals == m, lane, n_exp), axis=-1, keepdims=True)
        tops.append(m)
        idxs.append(i)
        hots.append(lane == i)
        vals = jnp.where(hots[-1], -jnp.inf, vals)
    es = [jnp.exp(t - tops[0]) for t in tops]
    den = es[0] + es[1] + es[2] + es[3]

    sel = [jnp.where(hm, 1.0, 0.0) for hm in hots]
    multi = sel[0] + sel[1] + sel[2] + sel[3]
    base = _dot(tri_ref[...], multi.astype(BF16))
    cnt_ref[0] = jnp.sum(multi, axis=0, keepdims=True)

    lane_o = lax.broadcasted_iota(I32, idx_ref.shape, 1)
    io = jnp.zeros(idx_ref.shape, I32)
    ro = jnp.zeros(idx_ref.shape, I32)
    go = jnp.zeros(gate_ref.shape, F32)
    for kk in range(TOP_K):
        rk = jnp.sum(sel[kk] * base, axis=-1, keepdims=True).astype(I32)
        io = jnp.where(lane_o == kk, idxs[kk], io)
        ro = jnp.where(lane_o == kk, rk, ro)
        go = jnp.where(lane_o == kk, es[kk] / den, go)
    idx_ref[...] = io
    rank_ref[...] = ro
    gate_ref[...] = go


def _router(x, g, sc, sh, w_r, b_r, seq):
    n, d = x.shape
    n_exp = w_r.shape[1]
    tm = _tile(seq, 512, 16)
    bspec = _batch_spec(d, seq // tm)
    pcs = d // 2 // LANES
    wh, wl = _split(w_r.astype(F32))
    tri = jnp.asarray(np.tril(np.ones((tm, tm), np.float32), -1), BF16)
    row = lambda w: pl.BlockSpec((tm, w), lambda i: (i, 0))
    full = lambda a: pl.BlockSpec(a.shape, lambda i: (0, 0))
    b2 = b_r.astype(F32).reshape(1, n_exp)
    return pl.pallas_call(
        _router_kernel,
        grid=(n // tm,),
        in_specs=[row(d), full(g), bspec, bspec, full(wh), full(wl), full(b2), full(tri)],
        out_specs=[pl.BlockSpec((tm * pcs, LANES), lambda i: (i, 0)), row(LANES), row(LANES), row(LANES),
                   pl.BlockSpec((1, 1, n_exp), lambda i: (i, 0, 0))],
        out_shape=[jax.ShapeDtypeStruct((n * pcs, LANES), I32),
                   jax.ShapeDtypeStruct((n, LANES), I32),
                   jax.ShapeDtypeStruct((n, LANES), F32),
                   jax.ShapeDtypeStruct((n, LANES), I32),
                   jax.ShapeDtypeStruct((n // tm, 1, n_exp), F32)],
        compiler_params=_params("parallel"),
        name="router",
    )(x, g, sc, sh, wh, wl, b2, tri)


def _moe_plan(top_idx, lrank, tile_counts, tmb):
    n = top_idx.shape[0]
    n_tiles, _, n_exp = tile_counts.shape
    tc = tile_counts.reshape(n_tiles, n_exp).astype(I32)
    counts = jnp.sum(tc, axis=0)
    nblk = (counts + tmb - 1) // tmb
    blk_end = jnp.cumsum(nblk)
    blk_start = blk_end - nblk
    n_used = blk_end[-1]
    n_blocks = n * TOP_K // tmb + n_exp
    blk = jnp.arange(n_blocks, dtype=I32)
    be = jnp.minimum(jnp.sum((blk[:, None] >= blk_end[None, :]).astype(I32), axis=1), n_exp - 1)
    last = jnp.sum(jnp.where(blk == n_used - 1, be, 0))
    be = jnp.where(blk < n_used, be, last).astype(I32)
    base = blk_start * tmb
    seg_dst = base[None, :] + jnp.cumsum(tc, axis=0) - tc
    seg_off = jnp.cumsum(tc, axis=1) - tc
    tm = n // n_tiles
    hot = top_idx.reshape(n_tiles, tm, TOP_K, 1) == jnp.arange(n_exp, dtype=I32)
    lpos = jnp.sum(jnp.where(hot, seg_off[:, None, None, :], 0), axis=-1) + lrank.reshape(n_tiles, tm, TOP_K)
    pad_start = base + counts
    pad_len = blk_end * tmb - pad_start
    flat = lambda a: a.reshape(-1).astype(I32)
    return dict(be=be, n_used=n_used.reshape(1).astype(I32),
                lpos=lpos.reshape(n_tiles, 1, tm * TOP_K).astype(I32),
                seg_cnt=flat(tc), seg_off=flat(seg_off), seg_dst=flat(seg_dst),
                pad_start=pad_start.astype(I32), pad_len=pad_len.astype(I32),
                n_blocks=n_blocks, tmb=tmb, n_tiles=n_tiles, n_exp=n_exp)


def _tile_at(ref, row, pcs):
    return ref.at[pl.ds(pl.multiple_of(row * pcs, pcs), pcs)]


def _segments(i, cnt_ref, off_ref, dst_ref, n_exp, pcs, copy):
    def seg(e, carry):
        cnt = cnt_ref[i * n_exp + e]

        @pl.when(cnt > 0)
        def _():
            size = pl.multiple_of(cnt * pcs, pcs)
            copy(pl.ds(pl.multiple_of(off_ref[i * n_exp + e] * pcs, pcs), size),
                 pl.ds(pl.multiple_of(dst_ref[i * n_exp + e] * pcs, pcs), size))

        return carry

    lax.fori_loop(0, n_exp, seg, 0)


def _dispatch_kernel(sc_ref, so_ref, sd_ref, ps_ref, pn_ref, nu_ref, lpos_ref, hp_ref, xs_hbm,
                     sbuf, zero_s, sems, sem, *, tmb, pcs):
    i = pl.program_id(0)
    last = pl.num_programs(0) - 1
    tm = hp_ref.shape[0] // pcs
    slot = i % 2

    def drain(s):
        pltpu.make_async_copy(sbuf.at[s], xs_hbm.at[pl.ds(0, sbuf.shape[1])], sems.at[s]).wait()

    @pl.when(i >= 2)
    def _():
        drain(slot)

    def row(r, carry):
        tile = hp_ref[pl.ds(pl.multiple_of(r * pcs, pcs), pcs), :]
        for kk in range(TOP_K):
            p = lpos_ref[0, 0, TOP_K * r + kk]
            sbuf[slot, pl.ds(pl.multiple_of(p * pcs, pcs), pcs), :] = tile
        return carry

    lax.fori_loop(0, tm, row, 0, unroll=8)

    def copy(src, dst):
        pltpu.make_async_copy(sbuf.at[slot, src], xs_hbm.at[dst], sems.at[slot]).start()

    _segments(i, sc_ref, so_ref, sd_ref, ps_ref.shape[0], pcs, copy)

    @pl.when(i == last)
    def _():
        drain(slot)

        @pl.when(i >= 1)
        def _():
            drain(1 - slot)

        zero_s[...] = jnp.zeros(zero_s.shape, I32)

        size = tmb // 2
        while size >= 1:
            def chunk(e, size=size):
                cnt = pn_ref[e]
                before = cnt - cnt % (2 * size)
                dst = pl.ds(pl.multiple_of((ps_ref[e] + before) * pcs, pcs), size * pcs)
                return (cnt // size) % 2 == 1, pltpu.make_async_copy(
                    zero_s.at[pl.ds(0, size * pcs)], xs_hbm.at[dst], sem)

            def fill(e, carry):
                take, cp = chunk(e)

                @pl.when(take)
                def _():
                    cp.start()

                return carry

            def fill_done(e, carry):
                take, cp = chunk(e)

                @pl.when(take)
                def _():
                    cp.wait()

                return carry

            lax.fori_loop(0, ps_ref.shape[0], fill, 0)
            lax.fori_loop(0, ps_ref.shape[0], fill_done, 0)
            size //= 2

        def block(bk, carry):
            cp = pltpu.make_async_copy(zero_s, _tile_at(xs_hbm, bk, tmb * pcs), sem)
            cp.start()
            cp.wait()
            return carry

        lax.fori_loop(nu_ref[0], xs_hbm.shape[0] // (tmb * pcs), block, 0)


def _dispatch(hp, plan, n):
    pcs = hp.shape[0] // n
    tmb = plan['tmb']
    n_tiles = plan['n_tiles']
    tm = n // n_tiles
    grid_spec = pltpu.PrefetchScalarGridSpec(
        num_scalar_prefetch=6,
        grid=(n_tiles,),
        in_specs=[pl.BlockSpec((1, 1, TOP_K * tm), lambda i, *_: (i, 0, 0), memory_space=pltpu.SMEM),
                  pl.BlockSpec((tm * pcs, LANES), lambda i, *_: (i, 0))],
        out_specs=pl.BlockSpec(memory_space=pl.ANY),
        scratch_shapes=[pltpu.VMEM((2, TOP_K * tm * pcs, LANES), I32),
                        pltpu.VMEM((tmb * pcs, LANES), I32),
                        pltpu.SemaphoreType.DMA((2,)), pltpu.SemaphoreType.DMA(())],
    )
    return pl.pallas_call(
        functools.partial(_dispatch_kernel, tmb=tmb, pcs=pcs),
        grid_spec=grid_spec,
        out_shape=jax.ShapeDtypeStruct((plan['n_blocks'] * tmb * pcs, LANES), I32),
        compiler_params=_params("arbitrary"),
        name="moe_dispatch",
    )(plan['seg_cnt'], plan['seg_off'], plan['seg_dst'], plan['pad_start'], plan['pad_len'], plan['n_used'],
      plan['lpos'], hp)


def _moe_kernel(be_ref, nu_ref, x_ref, wgu_ref, bgu_ref, wdn_ref, bdn_ref, y_ref, wgu_s, wdn_s, *, ff):
    i = pl.program_id(0)
    used = i < nu_ref[0]

    @pl.when(used)
    def _():
        e_new = jnp.logical_or(i == 0, be_ref[i] != be_ref[jnp.maximum(i - 1, 0)])

        @pl.when(e_new)
        def _():
            wgu_s[...] = wgu_ref[0, 0].astype(BF16)
            wdn_s[...] = wdn_ref[0, 0].astype(BF16)

        half = wgu_s.shape[0] // 2
        pcs = half // LANES
        tmb = x_ref.shape[0] // pcs
        x_lo, x_hi = _unpack_pair(_load_tiles(x_ref, 0, tmb, pcs))
        gu = (_dot(x_lo.astype(BF16), wgu_s[:half, :]) + _dot(x_hi.astype(BF16), wgu_s[half:, :])
              + bgu_ref[0, 0])
        gt = jnp.minimum(gu[:, :ff], SWIGLU_LIMIT)
        up = jnp.clip(gu[:, ff:], -SWIGLU_LIMIT, SWIGLU_LIMIT)
        act = ((up + 1.0) * gt * jax.nn.sigmoid(SWIGLU_ALPHA * gt)).astype(BF16)
        bdn = bdn_ref[0, 0]
        y_lo = _dot(act, wdn_s[:, :half]) + bdn[:, :half]
        y_hi = _dot(act, wdn_s[:, half:]) + bdn[:, half:]
        _store_tiles(y_ref, 0, _pack_pair(y_lo, y_hi))

    @pl.when(jnp.logical_not(used))
    def _():
        y_ref[...] = jnp.zeros(y_ref.shape, I32)


def _moe_experts(xs, plan, w_gu, b_gu, w_dn, b_dn, layer):
    depth, n_exp, d, ff2 = w_gu.shape
    ff = ff2 // 2
    tmb = plan['tmb']
    n_blocks = plan['n_blocks']
    rows = xs.shape[0] // n_blocks
    wmap = lambda i, be, nu: (layer, be[i], 0, 0)
    grid_spec = pltpu.PrefetchScalarGridSpec(
        num_scalar_prefetch=2,
        grid=(n_blocks,),
        in_specs=[pl.BlockSpec((rows, LANES), lambda i, be, nu: (i, 0)),
                  pl.BlockSpec((1, 1, d, ff2), wmap),
                  pl.BlockSpec((1, 1, 1, ff2), wmap),
                  pl.BlockSpec((1, 1, ff, d), wmap),
                  pl.BlockSpec((1, 1, 1, d), wmap)],
        out_specs=pl.BlockSpec((rows, LANES), lambda i, be, nu: (i, 0)),
        scratch_shapes=[pltpu.VMEM((d, ff2), BF16), pltpu.VMEM((ff, d), BF16)],
    )
    return pl.pallas_call(
        functools.partial(_moe_kernel, ff=ff),
        grid_spec=grid_spec,
        out_shape=jax.ShapeDtypeStruct(xs.shape, I32),
        compiler_params=_params("arbitrary"),
        name="moe_experts",
    )(plan['be'], plan['n_used'], xs, w_gu, b_gu.reshape(depth, n_exp, 1, ff2),
      w_dn, b_dn.reshape(depth, n_exp, 1, d))


def _combine_kernel(sc_ref, so_ref, sd_ref, lpos_ref, x_ref, gate_ref, g_ref, nf_ref, ys_hbm, o_ref,
                    ybuf, gbuf, sems, *, n_exp, final):
    i = pl.program_id(0)
    tm, d = x_ref.shape
    half = d // 2
    pcs = half // LANES
    slot = i % 2

    def fetch(t, s):
        def copy(loc, blk):
            pltpu.make_async_copy(ys_hbm.at[blk], ybuf.at[s, loc], sems.at[s]).start()

        _segments(t, sc_ref, so_ref, sd_ref, n_exp, pcs, copy)

    @pl.when(i == 0)
    def _():
        fetch(i, slot)

    @pl.when(i + 1 < pl.num_programs(0))
    def _():
        fetch(i + 1, 1 - slot)

    pltpu.make_async_copy(ys_hbm.at[pl.ds(0, ybuf.shape[1])], ybuf.at[slot], sems.at[slot]).wait()

    def row(r, carry):
        for kk in range(TOP_K):
            p = lpos_ref[0, 0, TOP_K * r + kk]
            gbuf[kk, pl.ds(pl.multiple_of(r * pcs, pcs), pcs), :] = (
                ybuf[slot, pl.ds(pl.multiple_of(p * pcs, pcs), pcs), :])
        return carry

    lax.fori_loop(0, tm, row, 0, unroll=8)

    gt = gate_ref[...]
    lo = hi = None
    for kk in range(TOP_K):
        w = gt[:, kk:kk + 1]
        y_lo, y_hi = _unpack_pair(_load_tiles(gbuf.at[kk], 0, tm, pcs))
        lo = w * y_lo if lo is None else lo + w * y_lo
        hi = w * y_hi if hi is None else hi + w * y_hi
    g = g_ref[0]
    x_lo = x_ref[:, :half] + lo * g[:, :half]
    x_hi = x_ref[:, half:] + hi * g[:, half:]
    if final:
        ms = (jnp.sum(x_lo * x_lo, axis=-1, keepdims=True) + jnp.sum(x_hi * x_hi, axis=-1, keepdims=True)) / d
        r = lax.rsqrt(ms + EPS)
        x_lo = (x_lo * r) * nf_ref[:, :half]
        x_hi = (x_hi * r) * nf_ref[:, half:]
    o_ref[:, :half] = x_lo
    o_ref[:, half:] = x_hi


def _combine(x, ys, gates, g2, plan, seq, norm_final, final):
    n, d = x.shape
    pcs = d // 2 // LANES
    nt = plan['n_tiles']
    tm = n // nt
    grid_spec = pltpu.PrefetchScalarGridSpec(
        num_scalar_prefetch=3,
        grid=(nt,),
        in_specs=[pl.BlockSpec((1, 1, TOP_K * tm), lambda i, *_: (i, 0, 0), memory_space=pltpu.SMEM),
                  pl.BlockSpec((tm, d), lambda i, *_: (i, 0)),
                  pl.BlockSpec((tm, LANES), lambda i, *_: (i, 0)),
                  _batch_spec(d, seq // tm),
                  pl.BlockSpec((1, d), lambda i, *_: (0, 0)),
                  pl.BlockSpec(memory_space=pl.ANY)],
        out_specs=pl.BlockSpec((tm, d), lambda i, *_: (i, 0)),
        scratch_shapes=[pltpu.VMEM((2, TOP_K * tm * pcs, LANES), I32),
                        pltpu.VMEM((TOP_K, tm * pcs, LANES), I32),
                        pltpu.SemaphoreType.DMA((2,))],
    )
    return pl.pallas_call(
        functools.partial(_combine_kernel, n_exp=plan['n_exp'], final=final),
        grid_spec=grid_spec,
        out_shape=jax.ShapeDtypeStruct((n, d), F32),
        compiler_params=_params("arbitrary"),
        name="moe_combine",
    )(plan['seg_cnt'], plan['seg_off'], plan['seg_dst'], plan['lpos'], x, gates, g2, norm_final, ys)


def kernel(x, c, ada_w, ada_b, norm_mix, norm_ffn, norm_final, s5_lambda_re, s5_lambda_im, s5_log_dt,
           s5_b_re, s5_b_im, s5_c_re, s5_c_im, s5_d, s5_w_glu, hg_w_in, hg_lb_raw, hg_norm, hg_w_out,
           router_w, router_b, moe_w_gate_up, moe_b_gate_up, moe_w_down, moe_b_down):
    bsz, seq, d = x.shape
    depth = ada_w.shape[0]
    n = bsz * seq
    dv = hg_norm.shape[-1]
    heads = d // dv
    fdim = (hg_w_in.shape[-1] - 2 * d) // 2
    dk = fdim // heads
    assert dk == LANES and dv == LANES, "one HGRN2 head per 128-lane tile"
    tmb = _tile(n * TOP_K, MOE_ROWS, SUBLANES)

    lb_p = jax.nn.softmax(hg_lb_raw.astype(F32), axis=0)
    lb_all = jnp.cumsum(lb_p, axis=0) - lb_p[0]
    mod = _ada_mod(c.astype(F32), ada_w, ada_b)

    xs = x.astype(F32).reshape(n, d)
    row = lambda v: v.astype(F32).reshape(1, -1)
    for i in range(depth):
        sh1, sc1, g1, sh2, sc2, g2 = [mod[i, :, k * d:(k + 1) * d].reshape(bsz, 1, d) for k in range(N_ADA)]
        j = i // 2
        if i % 2 == 0:
            bmat, cmat, lre, lim = _s5_prep(s5_lambda_re[j], s5_lambda_im[j], s5_log_dt[j],
                                            s5_b_re[j], s5_b_im[j], s5_c_re[j], s5_c_im[j])
            z = _s5_scan(xs.reshape(bsz, seq, d), row(norm_mix[i]), sc1, sh1, bmat, cmat, lre, lim, row(s5_d[j]))
            xs = _mm_res(z.reshape(n, d), s5_w_glu[j].astype(BF16), xs, g1, True, seq)
        else:
            lb = lb_all[i].reshape(heads, 1, dk)
            w_heads = hg_w_in[j].astype(BF16).reshape(d, 4, heads, dk).transpose(2, 0, 1, 3).reshape(heads, d, 4 * dk)
            h = _hg_norm(xs, row(norm_mix[i]), sc1, sh1, seq)
            o = _hg_mix(h.reshape(bsz, seq, d), w_heads, jnp.log(lb), jnp.log1p(-lb), row(hg_norm[j]))
            xs = _mm_res(o.reshape(n, d), hg_w_out[j].astype(BF16), xs, g1, False, seq)
        hp, top_idx, gates, rank, counts = _router(xs, row(norm_ffn[i]), sc2, sh2, router_w[i], router_b[i], seq)
        plan = _moe_plan(top_idx[:, :TOP_K], rank[:, :TOP_K], counts, tmb)
        xd = _dispatch(hp, plan, n)
        ys = _moe_experts(xd, plan, moe_w_gate_up, moe_b_gate_up.astype(F32),
                          moe_w_down, moe_b_down.astype(F32), i)
        xs = _combine(xs, ys, gates, g2, plan, seq, row(norm_final), final=i == depth - 1)
    return xs.reshape(bsz, seq, d).astype(x.dtype)
```

```python
import functools

import numpy as np
import jax
import jax.numpy as jnp
from jax import lax
from jax.experimental import pallas as pl
from jax.experimental.pallas import tpu as pltpu

F32 = jnp.float32
BF16 = jnp.bfloat16
I32 = jnp.int32

EPS = 1e-6
N_ADA = 6
TOP_K = 4
S5_RE_MAX = -1e-4
SWIGLU_LIMIT = 7.0
SWIGLU_ALPHA = 1.702
GELU_C0 = 0.7978845608028654
GELU_C1 = 0.044715

LANES = 128
SUBLANES = 8
HG_CHUNK = 64
HG_HEADS_PER_STEP = 2
HG_GROUP = 4
S5_CHUNK = 128
S5_LANE_TILES = 2
S5_SUB = 4
NORM_ROWS = 32
MOE_ROWS = 512
HI16 = -65536
VMEM_LIMIT = 56 * 1024 * 1024


def _params(*sem):
    return pltpu.CompilerParams(dimension_semantics=sem, vmem_limit_bytes=VMEM_LIMIT)


def _tile(n, pref, align):
    if n <= pref:
        return n
    t = (pref // align) * align
    while t > align and n % t:
        t -= align
    assert n % t == 0, (n, pref, align)
    return t


def _dot(a, b):
    return jnp.dot(a, b, preferred_element_type=F32)


def _dot_nt(a, b):
    return lax.dot_general(a, b, (((1,), (1,)), ((), ())), preferred_element_type=F32)


def _dot_tn(a, b):
    return lax.dot_general(a, b, (((0,), (0,)), ((), ())), preferred_element_type=F32)


def _split(a):
    hi = a.astype(BF16)
    lo = (a - hi.astype(F32)).astype(BF16)
    return hi, lo


def _dot3(a, w):
    ah, al = _split(a)
    wh, wl = _split(w)
    return _dot(ah, wh) + _dot(al, wh) + _dot(ah, wl)


def _norm_mod(x, g, sc, sh):
    ms = jnp.mean(x * x, axis=-1, keepdims=True)
    return ((x * lax.rsqrt(ms + EPS)) * g) * (1.0 + sc) + sh


def _norm_mod_rows(x_ref, mods, emit):
    rows = x_ref.shape[0]
    ch = min(NORM_ROWS, rows)

    def body(i, carry):
        r0 = pl.multiple_of(i * ch, ch)
        emit(r0, _norm_mod(x_ref[pl.ds(r0, ch), :], *mods()))
        return carry

    lax.fori_loop(0, rows // ch, body, 0, unroll=4 if (rows // ch) % 4 == 0 else 1)


def _batch_spec(width, tiles_per_batch):
    return pl.BlockSpec((1, 1, width), lambda i, *_: (i // tiles_per_batch, 0, 0))


def _pack_pair(lo, hi):
    lo_b = lax.bitcast_convert_type(lo.astype(BF16).astype(F32), I32)
    hi_b = lax.bitcast_convert_type(hi.astype(BF16).astype(F32), I32)
    return lax.shift_right_logical(lo_b, 16) | (hi_b & HI16)


def _unpack_pair(p):
    lo = lax.bitcast_convert_type(lax.shift_left(p, 16), F32)
    hi = lax.bitcast_convert_type(p & HI16, F32)
    return lo, hi


def _store_tiles(ref, row0, val):
    rows, width = val.shape
    pcs = width // LANES
    for jj in range(pcs):
        ref[pl.ds(row0 * pcs + jj, rows, stride=pcs), :] = val[:, jj * LANES:(jj + 1) * LANES]


def _load_tiles(ref, row0, rows, pcs):
    return jnp.concatenate([ref[pl.ds(row0 * pcs + jj, rows, stride=pcs), :] for jj in range(pcs)], axis=1)


def _ada_kernel(c_ref, w_ref, b_ref, o_ref):
    c = c_ref[...]
    o_ref[0] = _dot3(c * jax.nn.sigmoid(c), w_ref[0]) + b_ref[0]


def _ada_mod(c, ada_w, ada_b):
    depth, d, nd = ada_w.shape
    nb = c.shape[0]
    tn = _tile(nd, 768, LANES)
    return pl.pallas_call(
        _ada_kernel,
        grid=(depth, nd // tn),
        in_specs=[pl.BlockSpec((nb, d), lambda i, j: (0, 0)),
                  pl.BlockSpec((1, d, tn), lambda i, j: (i, 0, j)),
                  pl.BlockSpec((1, 1, tn), lambda i, j: (i, 0, j))],
        out_specs=pl.BlockSpec((1, nb, tn), lambda i, j: (i, 0, j)),
        out_shape=jax.ShapeDtypeStruct((depth, nb, nd), F32),
        compiler_params=_params("parallel", "parallel"),
        name="ada_mod",
    )(c, ada_w, ada_b.reshape(depth, 1, nd))


def _s5_kernel(x_ref, g_ref, sc_ref, sh_ref, bmat_ref, cmat_ref, lre_ref, lim_ref, d_ref,
               z_ref, h_s, st_s, bu_s, sa_s, z_s, *, tc, nb, hw):
    c = pl.program_id(0)
    j = pl.program_id(1)
    n_j = h_s.shape[0]

    @pl.when(j == 0)
    def _():
        for b in range(nb):
            def emit(r0, h, b=b):
                for jj in range(n_j):
                    h_s[jj, pl.ds(r0 * nb + b, h.shape[0], stride=nb), :] = h[:, jj * LANES:(jj + 1) * LANES]

            _norm_mod_rows(x_ref.at[b], lambda b=b: (g_ref[...], sc_ref[b], sh_ref[b]), emit)

    n_lt, n_sub = bu_s.shape[:2]
    ts = tc // n_sub
    rs = ts * nb

    @pl.when(c == 0)
    def _():
        for l in range(n_lt):
            st_s[j * n_lt + l] = jnp.zeros(st_s.shape[1:], F32)

    for l in range(n_lt):
        bmat = bmat_ref[l]
        for k in range(n_sub):
            bu_s[l, k] = _dot(h_s[j * n_lt + l, k * rs:(k + 1) * rs, :].astype(BF16), bmat)
    for l in range(n_lt):
        jl = j * n_lt + l
        cmat = cmat_ref[l]
        lre = jnp.broadcast_to(lre_ref[l], (nb, hw))
        lim = jnp.broadcast_to(lim_ref[l], (nb, hw))
        st = st_s[jl]
        s_re, s_im = st[:, :hw], st[:, hw:]
        for k in range(n_sub):
            for t in range(ts):
                b = bu_s[l, k, t * nb:(t + 1) * nb, :]
                s_re, s_im = (lre * s_re - lim * s_im + b[:, :hw],
                              lre * s_im + lim * s_re + b[:, hw:])
                sa_s[l, k, t * nb:(t + 1) * nb, :hw] = s_re
                sa_s[l, k, t * nb:(t + 1) * nb, hw:] = s_im
            y = (_dot(sa_s[l, k].astype(BF16), cmat)
                 + d_ref[:, l * LANES:(l + 1) * LANES] * h_s[jl, k * rs:(k + 1) * rs, :])
            z = 0.5 * y * (1.0 + jnp.tanh(GELU_C0 * (y + GELU_C1 * (y * y * y))))
            z_s[l, k * rs:(k + 1) * rs, :] = z
        st_s[jl, :, :hw] = s_re
        st_s[jl, :, hw:] = s_im
    for b in range(nb):
        for l in range(n_lt):
            z_ref[b, :, l * LANES:(l + 1) * LANES] = z_s[l, pl.ds(b, tc, stride=nb), :].astype(BF16)


def _s5_prep(lam_re, lam_im, log_dt, b_re, b_im, c_re, c_im):
    n_g, n_p = lam_re.shape
    n_h = b_re.shape[-1]
    gpt = LANES // n_h
    n_j = n_g // gpt
    lam = lax.complex(jnp.minimum(lam_re.astype(F32), S5_RE_MAX), lam_im.astype(F32))
    dt = jnp.exp(log_dt.astype(F32))[:, None]
    lam_bar = jnp.exp(lam * dt)
    b_bar = ((lam_bar - 1.0) / lam)[:, :, None] * lax.complex(b_re.astype(F32), b_im.astype(F32))
    eye = jnp.eye(gpt, dtype=F32)
    bb = b_bar.reshape(n_j, gpt, n_p, n_h)
    bre = jnp.einsum('jkph,gk->jghkp', bb.real, eye)
    bim = jnp.einsum('jkph,gk->jghkp', bb.imag, eye)
    bmat = jnp.stack([bre, bim], axis=3).reshape(n_j, LANES, 2 * gpt * n_p)
    cre = jnp.einsum('jghp,kg->jkpgh', c_re.astype(F32).reshape(n_j, gpt, n_h, n_p), eye)
    cim = jnp.einsum('jghp,kg->jkpgh', c_im.astype(F32).reshape(n_j, gpt, n_h, n_p), eye)
    cmat = jnp.stack([cre, -cim], axis=1).reshape(n_j, 2 * gpt * n_p, LANES)
    lre = lam_bar.real.reshape(n_j, 1, gpt * n_p)
    lim = lam_bar.imag.reshape(n_j, 1, gpt * n_p)
    return bmat.astype(BF16), cmat.astype(BF16), lre, lim


def _s5_scan(x, g, sc, sh, bmat, cmat, lre, lim, d_skip):
    nb, seq, d = x.shape
    n_j = d // LANES
    hw = lre.shape[-1]
    tc = _tile(seq, S5_CHUNK, 8)
    rows = tc * nb
    n_sub = S5_SUB if tc % S5_SUB == 0 else 1
    lt = S5_LANE_TILES if n_j % S5_LANE_TILES == 0 else 1
    kern = functools.partial(_s5_kernel, tc=tc, nb=nb, hw=hw)
    return pl.pallas_call(
        kern,
        grid=(seq // tc, n_j // lt),
        in_specs=[pl.BlockSpec((nb, tc, d), lambda c, j: (0, c, 0)),
                  pl.BlockSpec((1, d), lambda c, j: (0, 0)),
                  pl.BlockSpec((nb, 1, d), lambda c, j: (0, 0, 0)),
                  pl.BlockSpec((nb, 1, d), lambda c, j: (0, 0, 0)),
                  pl.BlockSpec((lt, LANES, 2 * hw), lambda c, j: (j, 0, 0)),
                  pl.BlockSpec((lt, 2 * hw, LANES), lambda c, j: (j, 0, 0)),
                  pl.BlockSpec((lt, 1, hw), lambda c, j: (j, 0, 0)),
                  pl.BlockSpec((lt, 1, hw), lambda c, j: (j, 0, 0)),
                  pl.BlockSpec((1, lt * LANES), lambda c, j: (0, j))],
        out_specs=pl.BlockSpec((nb, tc, lt * LANES), lambda c, j: (0, c, j)),
        out_shape=jax.ShapeDtypeStruct((nb, seq, d), BF16),
        scratch_shapes=[pltpu.VMEM((n_j, rows, LANES), F32),
                        pltpu.VMEM((n_j, nb, 2 * hw), F32),
                        pltpu.VMEM((lt, n_sub, rows // n_sub, 2 * hw), F32),
                        pltpu.VMEM((lt, n_sub, rows // n_sub, 2 * hw), F32),
                        pltpu.VMEM((lt, rows, LANES), F32)],
        compiler_params=_params("arbitrary", "arbitrary"),
        name="s5_scan",
    )(x, g, sc, sh, bmat, cmat, lre, lim, d_skip)


def _mm_res_kernel(a_ref, w_ref, x_ref, g_ref, o_ref, *, glu, tn):
    a = a_ref[...]
    d = o_ref.shape[1]
    for c0 in range(0, d, tn):
        p = _dot(a, w_ref[:, c0:c0 + tn])
        if glu:
            p = p * jax.nn.sigmoid(_dot(a, w_ref[:, d + c0:d + c0 + tn]))
        o_ref[:, c0:c0 + tn] = x_ref[:, c0:c0 + tn] + p * g_ref[0, :, c0:c0 + tn]


def _mm_res(a, w, x, gate, glu, seq):
    n, kdim = a.shape
    d = x.shape[1]
    tm = _tile(seq, 512, 16)
    tn = _tile(d, 512, LANES)
    return pl.pallas_call(
        functools.partial(_mm_res_kernel, glu=glu, tn=tn),
        grid=(n // tm,),
        in_specs=[pl.BlockSpec((tm, kdim), lambda i: (i, 0)),
                  pl.BlockSpec(w.shape, lambda i: (0, 0), pipeline_mode=pl.Buffered(1)),
                  pl.BlockSpec((tm, d), lambda i: (i, 0)),
                  _batch_spec(d, seq // tm)],
        out_specs=pl.BlockSpec((tm, d), lambda i: (i, 0)),
        out_shape=jax.ShapeDtypeStruct((n, d), F32),
        compiler_params=_params("parallel"),
        name="glu_res" if glu else "proj_res",
    )(a, w, x, gate)


def _hg_norm_kernel(x_ref, g_ref, sc_ref, sh_ref, o_ref):
    def emit(r0, h):
        o_ref[pl.ds(r0, h.shape[0]), :] = h.astype(BF16)

    _norm_mod_rows(x_ref, lambda: (g_ref[...], sc_ref[0], sh_ref[0]), emit)


def _hg_norm(x, g, sc, sh, seq):
    n, d = x.shape
    tm = _tile(seq, 1024, 16)
    bspec = _batch_spec(d, seq // tm)
    return pl.pallas_call(
        _hg_norm_kernel,
        grid=(n // tm,),
        in_specs=[pl.BlockSpec((tm, d), lambda i: (i, 0)), pl.BlockSpec((1, d), lambda i: (0, 0)), bspec, bspec],
        out_specs=pl.BlockSpec((tm, d), lambda i: (i, 0)),
        out_shape=jax.ShapeDtypeStruct((n, d), BF16),
        compiler_params=_params("parallel"),
        name="hg_norm",
    )(x, g, sc, sh)


def _hg_levels(ch):
    levels = []
    m = ch // 2
    while m >= 1:
        levels.append(m)
        m //= 2
    return levels


def _hg_masks(ch):
    levels = _hg_levels(ch)
    mk = np.zeros((len(levels) + 1, ch, ch), np.float32)
    for li, m in enumerate(levels):
        for t in range(ch):
            mid = (t // (2 * m)) * 2 * m + m - 1
            if t > mid:
                mk[li, t, mid - m + 1:mid + 1] = 1.0
    mk[-1] = np.eye(ch, dtype=np.float32)
    return mk


def _cumsum_rows(x, t_idx):
    sh = 1
    while sh < x.shape[0]:
        x = x + jnp.where(t_idx >= sh, pltpu.roll(x, sh, 0), 0.0)
        sh *= 2
    return x


def _level_ref(bc, m, sub3):
    ch, dk = bc.shape
    if 2 * m >= SUBLANES:
        bp = bc.reshape(ch // (2 * m), 2 * m, dk)
        return jnp.broadcast_to(bp[:, m - 1:m, :], bp.shape).reshape(ch, dk)
    b3 = bc.reshape(ch // SUBLANES, SUBLANES, dk)
    r = None
    for start in range(0, SUBLANES, 2 * m):
        cand = jnp.broadcast_to(b3[:, start + m - 1:start + m, :], b3.shape)
        r = cand if r is None else jnp.where(sub3 >= start, cand, r)
    return r.reshape(ch, dk)


def _hg_mix_kernel(h0_ref, hb_ref, hc_ref, w_ref, la_ref, l1_ref, mask_ref, gn_ref, o_ref, st_s, pa_s, pb_s,
                   *, nb, ch):
    d = h0_ref.shape[-1]

    def project(h_ref, e, c0, width):
        return _dot(h_ref[...].reshape(nb * ch, d), w_ref[e, :, c0:c0 + width])

    n_hd, _, cols = w_ref.shape

    @pl.when(pl.program_id(1) == 0)
    def _():
        st_s[...] = jnp.zeros(st_s.shape, F32)
        for e in range(n_hd):
            pa_s[:, e * cols:(e + 1) * cols] = project(h0_ref, e, 0, cols)

    args = (w_ref, la_ref, l1_ref, mask_ref, gn_ref, o_ref, st_s)
    for e in range(n_hd):
        _hg_chunk(hb_ref, *args, pa_s, pb_s, project, nb, ch, 0, e)
    for e in range(n_hd):
        _hg_chunk(hc_ref, *args, pb_s, pa_s, project, nb, ch, ch, e)


def _hg_chunk(hn_ref, w_ref, la_ref, l1_ref, mask_ref, gn_ref, o_ref, st_s, p_s, pn_s, project, nb, ch, r0, e):
    n_hd, _, cols = w_ref.shape
    dk = o_ref.shape[-1] // n_hd
    cb = e * cols
    assert nb % HG_GROUP == 0 and cols % (nb // HG_GROUP * LANES) == 0
    cpg = cols // (nb // HG_GROUP)

    la = la_ref[e]
    l1 = l1_ref[e]
    levels = _hg_levels(ch)
    n_lv = len(levels)
    gn = gn_ref[...]
    t_idx = lax.broadcasted_iota(I32, (ch, dk), 0)
    sub3 = lax.broadcasted_iota(I32, (ch // SUBLANES, SUBLANES, dk), 1)
    masks = [mask_ref[li] > 0.5 for li in range(n_lv + 1)]

    for b0 in range(0, nb, HG_GROUP):
        grp = range(b0, min(b0 + HG_GROUP, nb))
        c0 = (b0 // HG_GROUP) * cpg
        pn_s[:, cb + c0:cb + c0 + cpg] = project(hn_ref, e, c0, cpg)
        qk, qms, kms, q_in, k_out, decay = {}, {}, {}, {}, {}, {}
        for b in grp:
            q = p_s[b * ch:(b + 1) * ch, cb:cb + dk]
            z = p_s[b * ch:(b + 1) * ch, cb + dk:cb + 2 * dk]
            bt = l1 + jnp.minimum(z, 0.0) - jnp.log(1.0 + jnp.exp(-jnp.abs(z)))
            lf = jnp.maximum(la, bt) + jnp.log(1.0 + jnp.exp(-jnp.abs(la - bt)))
            bc = _cumsum_rows(lf, t_idx)
            qb = q.astype(BF16)
            kb = (1.0 - jnp.exp(lf)).astype(BF16)
            qk[b] = (qb, kb)
            qms[b], kms[b] = [], []
            for m in levels:
                decay_m = jnp.exp(-jnp.abs(bc - _level_ref(bc, m, sub3))).astype(BF16)
                qms[b].append(qb * decay_m)
                kms[b].append(kb * decay_m)
            b_end = bc[ch - 1:ch, :]
            q_in[b] = qb * jnp.exp(bc).astype(BF16)
            k_out[b] = kb * jnp.exp(b_end - bc).astype(BF16)
            decay[b] = jnp.exp(b_end)
        scores = {}
        for b in grp:
            s = jnp.where(masks[n_lv], _dot_nt(*qk[b]), 0.0)
            for li in range(n_lv):
                s = jnp.where(masks[li], _dot_nt(qms[b][li], kms[b][li]), s)
            scores[b] = s.astype(BF16)
        outs = {}
        for b in grp:
            v = p_s[b * ch:(b + 1) * ch, cb + 2 * dk:cb + 3 * dk].astype(BF16)
            st = st_s[e * nb + b]
            outs[b] = _dot(scores[b], v) + _dot_nt(q_in[b], st.astype(BF16))
            st_s[e * nb + b] = decay[b] * st + _dot_tn(v, k_out[b])
        for b in grp:
            o = outs[b]
            gate = p_s[b * ch:(b + 1) * ch, cb + 3 * dk:cb + 4 * dk]
            o = o * lax.rsqrt(jnp.mean(o * o, axis=-1, keepdims=True) + EPS) * gn
            o_ref[b, r0:r0 + ch, e * dk:(e + 1) * dk] = (o * (gate * jax.nn.sigmoid(gate))).astype(BF16)


def _hg_mix(h, w_heads, log_lb, log1m_lb, g_norm):
    nb, seq, d = h.shape
    heads, _, cols = w_heads.shape
    dk = cols // 4
    ch = _tile(seq, HG_CHUNK, 16)
    nc = seq // ch
    assert nc % 2 == 0, "two chunks per grid step"
    hp = HG_HEADS_PER_STEP if heads % HG_HEADS_PER_STEP == 0 else 1
    mk = _hg_masks(ch)
    per_head = lambda w: pl.BlockSpec((hp,) + w.shape[1:], lambda hd, s: (hd, 0, 0))
    return pl.pallas_call(
        functools.partial(_hg_mix_kernel, nb=nb, ch=ch),
        grid=(heads // hp, nc // 2),
        in_specs=[pl.BlockSpec((nb, ch, d), lambda hd, s: (0, 0, 0)),
                  pl.BlockSpec((nb, ch, d), lambda hd, s: (0, 2 * s + 1, 0)),
                  pl.BlockSpec((nb, ch, d), lambda hd, s: (0, jnp.minimum(2 * s + 2, nc - 1), 0)),
                  per_head(w_heads), per_head(log_lb), per_head(log1m_lb),
                  pl.BlockSpec(mk.shape, lambda hd, s: (0, 0, 0)),
                  pl.BlockSpec((1, dk), lambda hd, s: (0, 0))],
        out_specs=pl.BlockSpec((nb, 2 * ch, hp * dk), lambda hd, s: (0, s, hd)),
        out_shape=jax.ShapeDtypeStruct((nb, seq, heads * dk), BF16),
        scratch_shapes=[pltpu.VMEM((hp * nb, dk, dk), F32), pltpu.VMEM((nb * ch, hp * cols), F32),
                        pltpu.VMEM((nb * ch, hp * cols), F32)],
        compiler_params=_params("parallel", "arbitrary"),
        name="hg_mix",
    )(h, h, h, w_heads, log_lb, log1m_lb, jnp.asarray(mk, F32), g_norm)


def _router_kernel(x_ref, g_ref, sc_ref, sh_ref, wh_ref, wl_ref, br_ref, tri_ref,
                   hp_ref, idx_ref, gate_ref, rank_ref, cnt_ref):
    h = _norm_mod(x_ref[...], g_ref[...], sc_ref[0], sh_ref[0])
    half = h.shape[1] // 2
    _store_tiles(hp_ref, 0, _pack_pair(h[:, :half], h[:, half:]))
    hh, hl = _split(h)
    wh = wh_ref[...]
    vals = _dot(hh, wh) + _dot(hl, wh) + _dot(hh, wl_ref[...]) + br_ref[...]
    tm, n_exp = vals.shape
    lane = lax.broadcasted_iota(I32, (tm, n_exp), 1)
    tops, idxs, hots = [], [], []
    for _ in range(TOP_K):
        m = jnp.max(vals, axis=-1, keepdims=True)
        i = jnp.min(jnp.where(vals == m, lane, n_exp), axis=-1, keepdims=True)
        tops.append(m)
        idxs.append(i)
        hots.append(lane == i)
        vals = jnp.where(hots[-1], -jnp.inf, vals)
    es = [jnp.exp(t - tops[0]) for t in tops]
    den = es[0] + es[1] + es[2] + es[3]

    sel = [jnp.where(hm, 1.0, 0.0) for hm in hots]
    multi = sel[0] + sel[1] + sel[2] + sel[3]
    base = _dot(tri_ref[...], multi.astype(BF16))
    cnt_ref[0] = jnp.sum(multi, axis=0, keepdims=True)

    lane_o = lax.broadcasted_iota(I32, idx_ref.shape, 1)
    io = jnp.zeros(idx_ref.shape, I32)
    ro = jnp.zeros(idx_ref.shape, I32)
    go = jnp.zeros(gate_ref.shape, F32)
    for kk in range(TOP_K):
        rk = jnp.sum(sel[kk] * base, axis=-1, keepdims=True).astype(I32)
        io = jnp.where(lane_o == kk, idxs[kk], io)
        ro = jnp.where(lane_o == kk, rk, ro)
        go = jnp.where(lane_o == kk, es[kk] / den, go)
    idx_ref[...] = io
    rank_ref[...] = ro
    gate_ref[...] = go


def _router(x, g, sc, sh, w_r, b_r, seq):
    n, d = x.shape
    n_exp = w_r.shape[1]
    tm = _tile(seq, 512, 16)
    bspec = _batch_spec(d, seq // tm)
    pcs = d // 2 // LANES
    wh, wl = _split(w_r.astype(F32))
    tri = jnp.asarray(np.tril(np.ones((tm, tm), np.float32), -1), BF16)
    row = lambda w: pl.BlockSpec((tm, w), lambda i: (i, 0))
    full = lambda a: pl.BlockSpec(a.shape, lambda i: (0, 0))
    b2 = b_r.astype(F32).reshape(1, n_exp)
    return pl.pallas_call(
        _router_kernel,
        grid=(n // tm,),
        in_specs=[row(d), full(g), bspec, bspec, full(wh), full(wl), full(b2), full(tri)],
        out_specs=[pl.BlockSpec((tm * pcs, LANES), lambda i: (i, 0)), row(LANES), row(LANES), row(LANES),
                   pl.BlockSpec((1, 1, n_exp), lambda i: (i, 0, 0))],
        out_shape=[jax.ShapeDtypeStruct((n * pcs, LANES), I32),
                   jax.ShapeDtypeStruct((n, LANES), I32),
                   jax.ShapeDtypeStruct((n, LANES), F32),
                   jax.ShapeDtypeStruct((n, LANES), I32),
                   jax.ShapeDtypeStruct((n // tm, 1, n_exp), F32)],
        compiler_params=_params("parallel"),
        name="router",
    )(x, g, sc, sh, wh, wl, b2, tri)


def _moe_plan(top_idx, lrank, tile_counts, tmb):
    n = top_idx.shape[0]
    n_tiles, _, n_exp = tile_counts.shape
    tc = tile_counts.reshape(n_tiles, n_exp).astype(I32)
    counts = jnp.sum(tc, axis=0)
    nblk = (counts + tmb - 1) // tmb
    blk_end = jnp.cumsum(nblk)
    blk_start = blk_end - nblk
    n_used = blk_end[-1]
    n_blocks = n * TOP_K // tmb + n_exp
    blk = jnp.arange(n_blocks, dtype=I32)
    be = jnp.minimum(jnp.sum((blk[:, None] >= blk_end[None, :]).astype(I32), axis=1), n_exp - 1)
    last = jnp.sum(jnp.where(blk == n_used - 1, be, 0))
    be = jnp.where(blk < n_used, be, last).astype(I32)
    base = blk_start * tmb
    seg_dst = base[None, :] + jnp.cumsum(tc, axis=0) - tc
    seg_off = jnp.cumsum(tc, axis=1) - tc
    tm = n // n_tiles
    hot = top_idx.reshape(n_tiles, tm, TOP_K, 1) == jnp.arange(n_exp, dtype=I32)
    lpos = jnp.sum(jnp.where(hot, seg_off[:, None, None, :], 0), axis=-1) + lrank.reshape(n_tiles, tm, TOP_K)
    pad_start = base + counts
    pad_len = blk_end * tmb - pad_start
    flat = lambda a: a.reshape(-1).astype(I32)
    return dict(be=be, n_used=n_used.reshape(1).astype(I32),
                lpos=lpos.reshape(n_tiles, 1, tm * TOP_K).astype(I32),
                seg_cnt=flat(tc), seg_off=flat(seg_off), seg_dst=flat(seg_dst),
                pad_start=pad_start.astype(I32), pad_len=pad_len.astype(I32),
                n_blocks=n_blocks, tmb=tmb, n_tiles=n_tiles, n_exp=n_exp)


def _tile_at(ref, row, pcs):
    return ref.at[pl.ds(pl.multiple_of(row * pcs, pcs), pcs)]


def _segments(i, cnt_ref, off_ref, dst_ref, n_exp, pcs, copy):
    def seg(e, carry):
        cnt = cnt_ref[i * n_exp + e]

        @pl.when(cnt > 0)
        def _():
            size = pl.multiple_of(cnt * pcs, pcs)
            copy(pl.ds(pl.multiple_of(off_ref[i * n_exp + e] * pcs, pcs), size),
                 pl.ds(pl.multiple_of(dst_ref[i * n_exp + e] * pcs, pcs), size))

        return carry

    lax.fori_loop(0, n_exp, seg, 0)


def _dispatch_kernel(sc_ref, so_ref, sd_ref, ps_ref, pn_ref, nu_ref, lpos_ref, hp_ref, xs_hbm,
                     sbuf, zero_s, sems, sem, *, tmb, pcs):
    i = pl.program_id(0)
    last = pl.num_programs(0) - 1
    tm = hp_ref.shape[0] // pcs
    slot = i % 2

    def drain(s):
        pltpu.make_async_copy(sbuf.at[s], xs_hbm.at[pl.ds(0, sbuf.shape[1])], sems.at[s]).wait()

    @pl.when(i >= 2)
    def _():
        drain(slot)

    def row(r, carry):
        tile = hp_ref[pl.ds(pl.multiple_of(r * pcs, pcs), pcs), :]
        for kk in range(TOP_K):
            p = lpos_ref[0, 0, TOP_K * r + kk]
            sbuf[slot, pl.ds(pl.multiple_of(p * pcs, pcs), pcs), :] = tile
        return carry

    lax.fori_loop(0, tm, row, 0, unroll=8)

    def copy(src, dst):
        pltpu.make_async_copy(sbuf.at[slot, src], xs_hbm.at[dst], sems.at[slot]).start()

    _segments(i, sc_ref, so_ref, sd_ref, ps_ref.shape[0], pcs, copy)

    @pl.when(i == last)
    def _():
        drain(slot)

        @pl.when(i >= 1)
        def _():
            drain(1 - slot)

        zero_s[...] = jnp.zeros(zero_s.shape, I32)

        size = tmb // 2
        while size >= 1:
            def chunk(e, size=size):
                cnt = pn_ref[e]
                before = cnt - cnt % (2 * size)
                dst = pl.ds(pl.multiple_of((ps_ref[e] + before) * pcs, pcs), size * pcs)
                return (cnt // size) % 2 == 1, pltpu.make_async_copy(
                    zero_s.at[pl.ds(0, size * pcs)], xs_hbm.at[dst], sem)

            def fill(e, carry):
                take, cp = chunk(e)

                @pl.when(take)
                def _():
                    cp.start()

                return carry

            def fill_done(e, carry):
                take, cp = chunk(e)

                @pl.when(take)
                def _():
                    cp.wait()

                return carry

            lax.fori_loop(0, ps_ref.shape[0], fill, 0)
            lax.fori_loop(0, ps_ref.shape[0], fill_done, 0)
            size //= 2

        def block(bk, carry):
            cp = pltpu.make_async_copy(zero_s, _tile_at(xs_hbm, bk, tmb * pcs), sem)
            cp.start()
            cp.wait()
            return carry

        lax.fori_loop(nu_ref[0], xs_hbm.shape[0] // (tmb * pcs), block, 0)


def _dispatch(hp, plan, n):
    pcs = hp.shape[0] // n
    tmb = plan['tmb']
    n_tiles = plan['n_tiles']
    tm = n // n_tiles
    grid_spec = pltpu.PrefetchScalarGridSpec(
        num_scalar_prefetch=6,
        grid=(n_tiles,),
        in_specs=[pl.BlockSpec((1, 1, TOP_K * tm), lambda i, *_: (i, 0, 0), memory_space=pltpu.SMEM),
                  pl.BlockSpec((tm * pcs, LANES), lambda i, *_: (i, 0))],
        out_specs=pl.BlockSpec(memory_space=pl.ANY),
        scratch_shapes=[pltpu.VMEM((2, TOP_K * tm * pcs, LANES), I32),
                        pltpu.VMEM((tmb * pcs, LANES), I32),
                        pltpu.SemaphoreType.DMA((2,)), pltpu.SemaphoreType.DMA(())],
    )
    return pl.pallas_call(
        functools.partial(_dispatch_kernel, tmb=tmb, pcs=pcs),
        grid_spec=grid_spec,
        out_shape=jax.ShapeDtypeStruct((plan['n_blocks'] * tmb * pcs, LANES), I32),
        compiler_params=_params("arbitrary"),
        name="moe_dispatch",
    )(plan['seg_cnt'], plan['seg_off'], plan['seg_dst'], plan['pad_start'], plan['pad_len'], plan['n_used'],
      plan['lpos'], hp)


def _moe_kernel(be_ref, nu_ref, x_ref, wgu_ref, bgu_ref, wdn_ref, bdn_ref, y_ref, wgu_s, wdn_s, *, ff):
    i = pl.program_id(0)
    used = i < nu_ref[0]

    @pl.when(used)
    def _():
        e_new = jnp.logical_or(i == 0, be_ref[i] != be_ref[jnp.maximum(i - 1, 0)])

        @pl.when(e_new)
        def _():
            wgu_s[...] = wgu_ref[0, 0].astype(BF16)
            wdn_s[...] = wdn_ref[0, 0].astype(BF16)

        half = wgu_s.shape[0] // 2
        pcs = half // LANES
        tmb = x_ref.shape[0] // pcs
        x_lo, x_hi = _unpack_pair(_load_tiles(x_ref, 0, tmb, pcs))
        gu = (_dot(x_lo.astype(BF16), wgu_s[:half, :]) + _dot(x_hi.astype(BF16), wgu_s[half:, :])
              + bgu_ref[0, 0])
        gt = jnp.minimum(gu[:, :ff], SWIGLU_LIMIT)
        up = jnp.clip(gu[:, ff:], -SWIGLU_LIMIT, SWIGLU_LIMIT)
        act = ((up + 1.0) * gt * jax.nn.sigmoid(SWIGLU_ALPHA * gt)).astype(BF16)
        bdn = bdn_ref[0, 0]
        y_lo = _dot(act, wdn_s[:, :half]) + bdn[:, :half]
        y_hi = _dot(act, wdn_s[:, half:]) + bdn[:, half:]
        _store_tiles(y_ref, 0, _pack_pair(y_lo, y_hi))

    @pl.when(jnp.logical_not(used))
    def _():
        y_ref[...] = jnp.zeros(y_ref.shape, I32)


def _moe_experts(xs, plan, w_gu, b_gu, w_dn, b_dn, layer):
    depth, n_exp, d, ff2 = w_gu.shape
    ff = ff2 // 2
    tmb = plan['tmb']
    n_blocks = plan['n_blocks']
    rows = xs.shape[0] // n_blocks
    wmap = lambda i, be, nu: (layer, be[i], 0, 0)
    grid_spec = pltpu.PrefetchScalarGridSpec(
        num_scalar_prefetch=2,
        grid=(n_blocks,),
        in_specs=[pl.BlockSpec((rows, LANES), lambda i, be, nu: (i, 0)),
                  pl.BlockSpec((1, 1, d, ff2), wmap),
                  pl.BlockSpec((1, 1, 1, ff2), wmap),
                  pl.BlockSpec((1, 1, ff, d), wmap),
                  pl.BlockSpec((1, 1, 1, d), wmap)],
        out_specs=pl.BlockSpec((rows, LANES), lambda i, be, nu: (i, 0)),
        scratch_shapes=[pltpu.VMEM((d, ff2), BF16), pltpu.VMEM((ff, d), BF16)],
    )
    return pl.pallas_call(
        functools.partial(_moe_kernel, ff=ff),
        grid_spec=grid_spec,
        out_shape=jax.ShapeDtypeStruct(xs.shape, I32),
        compiler_params=_params("arbitrary"),
        name="moe_experts",
    )(plan['be'], plan['n_used'], xs, w_gu, b_gu.reshape(depth, n_exp, 1, ff2),
      w_dn, b_dn.reshape(depth, n_exp, 1, d))


def _combine_kernel(sc_ref, so_ref, sd_ref, lpos_ref, x_ref, gate_ref, g_ref, nf_ref, ys_hbm, o_ref,
                    ybuf, gbuf, sems, *, n_exp, final):
    i = pl.program_id(0)
    tm, d = x_ref.shape
    half = d // 2
    pcs = half // LANES
    slot = i % 2

    def fetch(t, s):
        def copy(loc, blk):
            pltpu.make_async_copy(ys_hbm.at[blk], ybuf.at[s, loc], sems.at[s]).start()

        _segments(t, sc_ref, so_ref, sd_ref, n_exp, pcs, copy)

    @pl.when(i == 0)
    def _():
        fetch(i, slot)

    @pl.when(i + 1 < pl.num_programs(0))
    def _():
        fetch(i + 1, 1 - slot)

    pltpu.make_async_copy(ys_hbm.at[pl.ds(0, ybuf.shape[1])], ybuf.at[slot], sems.at[slot]).wait()

    def row(r, carry):
        for kk in range(TOP_K):
            p = lpos_ref[0, 0, TOP_K * r + kk]
            gbuf[kk, pl.ds(pl.multiple_of(r * pcs, pcs), pcs), :] = (
                ybuf[slot, pl.ds(pl.multiple_of(p * pcs, pcs), pcs), :])
        return carry

    lax.fori_loop(0, tm, row, 0, unroll=8)

    gt = gate_ref[...]
    lo = hi = None
    for kk in range(TOP_K):
        w = gt[:, kk:kk + 1]
        y_lo, y_hi = _unpack_pair(_load_tiles(gbuf.at[kk], 0, tm, pcs))
        lo = w * y_lo if lo is None else lo + w * y_lo
        hi = w * y_hi if hi is None else hi + w * y_hi
    g = g_ref[0]
    x_lo = x_ref[:, :half] + lo * g[:, :half]
    x_hi = x_ref[:, half:] + hi * g[:, half:]
    if final:
        ms = (jnp.sum(x_lo * x_lo, axis=-1, keepdims=True) + jnp.sum(x_hi * x_hi, axis=-1, keepdims=True)) / d
        r = lax.rsqrt(ms + EPS)
        x_lo = (x_lo * r) * nf_ref[:, :half]
        x_hi = (x_hi * r) * nf_ref[:, half:]
    o_ref[:, :half] = x_lo
    o_ref[:, half:] = x_hi


def _combine(x, ys, gates, g2, plan, seq, norm_final, final):
    n, d = x.shape
    pcs = d // 2 // LANES
    nt = plan['n_tiles']
    tm = n // nt
    grid_spec = pltpu.PrefetchScalarGridSpec(
        num_scalar_prefetch=3,
        grid=(nt,),
        in_specs=[pl.BlockSpec((1, 1, TOP_K * tm), lambda i, *_: (i, 0, 0), memory_space=pltpu.SMEM),
                  pl.BlockSpec((tm, d), lambda i, *_: (i, 0)),
                  pl.BlockSpec((tm, LANES), lambda i, *_: (i, 0)),
                  _batch_spec(d, seq // tm),
                  pl.BlockSpec((1, d), lambda i, *_: (0, 0)),
                  pl.BlockSpec(memory_space=pl.ANY)],
        out_specs=pl.BlockSpec((tm, d), lambda i, *_: (i, 0)),
        scratch_shapes=[pltpu.VMEM((2, TOP_K * tm * pcs, LANES), I32),
                        pltpu.VMEM((TOP_K, tm * pcs, LANES), I32),
                        pltpu.SemaphoreType.DMA((2,))],
    )
    return pl.pallas_call(
        functools.partial(_combine_kernel, n_exp=plan['n_exp'], final=final),
        grid_spec=grid_spec,
        out_shape=jax.ShapeDtypeStruct((n, d), F32),
        compiler_params=_params("arbitrary"),
        name="moe_combine",
    )(plan['seg_cnt'], plan['seg_off'], plan['seg_dst'], plan['lpos'], x, gates, g2, norm_final, ys)


def kernel(x, c, ada_w, ada_b, norm_mix, norm_ffn, norm_final, s5_lambda_re, s5_lambda_im, s5_log_dt,
           s5_b_re, s5_b_im, s5_c_re, s5_c_im, s5_d, s5_w_glu, hg_w_in, hg_lb_raw, hg_norm, hg_w_out,
           router_w, router_b, moe_w_gate_up, moe_b_gate_up, moe_w_down, moe_b_down):
    bsz, seq, d = x.shape
    depth = ada_w.shape[0]
    n = bsz * seq
    dv = hg_norm.shape[-1]
    heads = d // dv
    fdim = (hg_w_in.shape[-1] - 2 * d) // 2
    dk = fdim // heads
    assert dk == LANES and dv == LANES, "one HGRN2 head per 128-lane tile"
    tmb = _tile(n * TOP_K, MOE_ROWS, SUBLANES)

    lb_p = jax.nn.softmax(hg_lb_raw.astype(F32), axis=0)
    lb_all = jnp.cumsum(lb_p, axis=0) - lb_p[0]
    mod = _ada_mod(c.astype(F32), ada_w, ada_b)

    xs = x.astype(F32).reshape(n, d)
    row = lambda v: v.astype(F32).reshape(1, -1)
    for i in range(depth):
        sh1, sc1, g1, sh2, sc2, g2 = [mod[i, :, k * d:(k + 1) * d].reshape(bsz, 1, d) for k in range(N_ADA)]
        j = i // 2
        if i % 2 == 0:
            bmat, cmat, lre, lim = _s5_prep(s5_lambda_re[j], s5_lambda_im[j], s5_log_dt[j],
                                            s5_b_re[j], s5_b_im[j], s5_c_re[j], s5_c_im[j])
            z = _s5_scan(xs.reshape(bsz, seq, d), row(norm_mix[i]), sc1, sh1, bmat, cmat, lre, lim, row(s5_d[j]))
            xs = _mm_res(z.reshape(n, d), s5_w_glu[j].astype(BF16), xs, g1, True, seq)
        else:
            lb = lb_all[i].reshape(heads, 1, dk)
            w_heads = hg_w_in[j].astype(BF16).reshape(d, 4, heads, dk).transpose(2, 0, 1, 3).reshape(heads, d, 4 * dk)
            h = _hg_norm(xs, row(norm_mix[i]), sc1, sh1, seq)
            o = _hg_mix(h.reshape(bsz, seq, d), w_heads, jnp.log(lb), jnp.log1p(-lb), row(hg_norm[j]))
            xs = _mm_res(o.reshape(n, d), hg_w_out[j].astype(BF16), xs, g1, False, seq)
        hp, top_idx, gates, rank, counts = _router(xs, row(norm_ffn[i]), sc2, sh2, router_w[i], router_b[i], seq)
        plan = _moe_plan(top_idx[:, :TOP_K], rank[:, :TOP_K], counts, tmb)
        xd = _dispatch(hp, plan, n)
        ys = _moe_experts(xd, plan, moe_w_gate_up, moe_b_gate_up.astype(F32),
                          moe_w_down, moe_b_down.astype(F32), i)
        xs = _combine(xs, ys, gates, g2, plan, seq, row(norm_final), final=i == depth - 1)
    return xs.reshape(bsz, seq, d).astype(x.dtype)
```

```python
import functools

import numpy as np
import jax
import jax.numpy as jnp
from jax import lax
from jax.experimental import pallas as pl
from jax.experimental.pallas import tpu as pltpu

F32 = jnp.float32
BF16 = jnp.bfloat16
I32 = jnp.int32

EPS = 1e-6
N_ADA = 6
TOP_K = 4
S5_RE_MAX = -1e-4
SWIGLU_LIMIT = 7.0
SWIGLU_ALPHA = 1.702
GELU_C0 = 0.7978845608028654
GELU_C1 = 0.044715

LANES = 128
SUBLANES = 8
HG_CHUNK = 64
HG_HEADS_PER_STEP = 2
HG_GROUP = 4
S5_CHUNK = 128
S5_LANE_TILES = 2
S5_SUB = 4
NORM_ROWS = 32
MOE_ROWS = 512
HI16 = -65536
VMEM_LIMIT = 56 * 1024 * 1024


def _params(*sem):
    return pltpu.CompilerParams(dimension_semantics=sem, vmem_limit_bytes=VMEM_LIMIT)


def _tile(n, pref, align):
    if n <= pref:
        return n
    t = (pref // align) * align
    while t > align and n % t:
        t -= align
    assert n % t == 0, (n, pref, align)
    return t


def _dot(a, b):
    return jnp.dot(a, b, preferred_element_type=F32)


def _dot_nt(a, b):
    return lax.dot_general(a, b, (((1,), (1,)), ((), ())), preferred_element_type=F32)


def _dot_tn(a, b):
    return lax.dot_general(a, b, (((0,), (0,)), ((), ())), preferred_element_type=F32)


def _split(a):
    hi = a.astype(BF16)
    lo = (a - hi.astype(F32)).astype(BF16)
    return hi, lo


def _dot3(a, w):
    ah, al = _split(a)
    wh, wl = _split(w)
    return _dot(ah, wh) + _dot(al, wh) + _dot(ah, wl)


def _norm_mod(x, g, sc, sh):
    ms = jnp.mean(x * x, axis=-1, keepdims=True)
    return ((x * lax.rsqrt(ms + EPS)) * g) * (1.0 + sc) + sh


def _norm_mod_rows(x_ref, mods, emit):
    rows = x_ref.shape[0]
    ch = min(NORM_ROWS, rows)

    def body(i, carry):
        r0 = pl.multiple_of(i * ch, ch)
        emit(r0, _norm_mod(x_ref[pl.ds(r0, ch), :], *mods()))
        return carry

    lax.fori_loop(0, rows // ch, body, 0, unroll=4 if (rows // ch) % 4 == 0 else 1)


def _batch_spec(width, tiles_per_batch):
    return pl.BlockSpec((1, 1, width), lambda i, *_: (i // tiles_per_batch, 0, 0))


def _pack_pair(lo, hi):
    lo_b = lax.bitcast_convert_type(lo.astype(BF16).astype(F32), I32)
    hi_b = lax.bitcast_convert_type(hi.astype(BF16).astype(F32), I32)
    return lax.shift_right_logical(lo_b, 16) | (hi_b & HI16)


def _unpack_pair(p):
    lo = lax.bitcast_convert_type(lax.shift_left(p, 16), F32)
    hi = lax.bitcast_convert_type(p & HI16, F32)
    return lo, hi


def _store_tiles(ref, row0, val):
    rows, width = val.shape
    pcs = width // LANES
    for jj in range(pcs):
        ref[pl.ds(row0 * pcs + jj, rows, stride=pcs), :] = val[:, jj * LANES:(jj + 1) * LANES]


def _load_tiles(ref, row0, rows, pcs):
    return jnp.concatenate([ref[pl.ds(row0 * pcs + jj, rows, stride=pcs), :] for jj in range(pcs)], axis=1)


def _ada_kernel(c_ref, w_ref, b_ref, o_ref):
    c = c_ref[...]
    o_ref[0] = _dot3(c * jax.nn.sigmoid(c), w_ref[0]) + b_ref[0]


def _ada_mod(c, ada_w, ada_b):
    depth, d, nd = ada_w.shape
    nb = c.shape[0]
    tn = _tile(nd, 768, LANES)
    return pl.pallas_call(
        _ada_kernel,
        grid=(depth, nd // tn),
        in_specs=[pl.BlockSpec((nb, d), lambda i, j: (0, 0)),
                  pl.BlockSpec((1, d, tn), lambda i, j: (i, 0, j)),
                  pl.BlockSpec((1, 1, tn), lambda i, j: (i, 0, j))],
        out_specs=pl.BlockSpec((1, nb, tn), lambda i, j: (i, 0, j)),
        out_shape=jax.ShapeDtypeStruct((depth, nb, nd), F32),
        compiler_params=_params("parallel", "parallel"),
        name="ada_mod",
    )(c, ada_w, ada_b.reshape(depth, 1, nd))


def _s5_kernel(x_ref, g_ref, sc_ref, sh_ref, bmat_ref, cmat_ref, lre_ref, lim_ref, d_ref,
               z_ref, h_s, st_s, bu_s, sa_s, z_s, *, tc, nb, hw):
    c = pl.program_id(0)
    j = pl.program_id(1)
    n_j = h_s.shape[0]

    @pl.when(j == 0)
    def _():
        for b in range(nb):
            def emit(r0, h, b=b):
                for jj in range(n_j):
                    h_s[jj, pl.ds(r0 * nb + b, h.shape[0], stride=nb), :] = h[:, jj * LANES:(jj + 1) * LANES]

            _norm_mod_rows(x_ref.at[b], lambda b=b: (g_ref[...], sc_ref[b], sh_ref[b]), emit)

    n_lt, n_sub = bu_s.shape[:2]
    ts = tc // n_sub
    rs = ts * nb

    @pl.when(c == 0)
    def _():
        for l in range(n_lt):
            st_s[j * n_lt + l] = jnp.zeros(st_s.shape[1:], F32)

    for l in range(n_lt):
        bmat = bmat_ref[l]
        for k in range(n_sub):
            bu_s[l, k] = _dot(h_s[j * n_lt + l, k * rs:(k + 1) * rs, :].astype(BF16), bmat)
    for l in range(n_lt):
        jl = j * n_lt + l
        cmat = cmat_ref[l]
        lre = jnp.broadcast_to(lre_ref[l], (nb, hw))
        lim = jnp.broadcast_to(lim_ref[l], (nb, hw))
        st = st_s[jl]
        s_re, s_im = st[:, :hw], st[:, hw:]
        for k in range(n_sub):
            for t in range(ts):
                b = bu_s[l, k, t * nb:(t + 1) * nb, :]
                s_re, s_im = (lre * s_re - lim * s_im + b[:, :hw],
                              lre * s_im + lim * s_re + b[:, hw:])
                sa_s[l, k, t * nb:(t + 1) * nb, :hw] = s_re
                sa_s[l, k, t * nb:(t + 1) * nb, hw:] = s_im
            y = (_dot(sa_s[l, k].astype(BF16), cmat)
                 + d_ref[:, l * LANES:(l + 1) * LANES] * h_s[jl, k * rs:(k + 1) * rs, :])
            z = 0.5 * y * (1.0 + jnp.tanh(GELU_C0 * (y + GELU_C1 * (y * y * y))))
            z_s[l, k * rs:(k + 1) * rs, :] = z
        st_s[jl, :, :hw] = s_re
        st_s[jl, :, hw:] = s_im
    for b in range(nb):
        for l in range(n_lt):
            z_ref[b, :, l * LANES:(l + 1) * LANES] = z_s[l, pl.ds(b, tc, stride=nb), :].astype(BF16)


def _s5_prep(lam_re, lam_im, log_dt, b_re, b_im, c_re, c_im):
    n_g, n_p = lam_re.shape
    n_h = b_re.shape[-1]
    gpt = LANES // n_h
    n_j = n_g // gpt
    lam = lax.complex(jnp.minimum(lam_re.astype(F32), S5_RE_MAX), lam_im.astype(F32))
    dt = jnp.exp(log_dt.astype(F32))[:, None]
    lam_bar = jnp.exp(lam * dt)
    b_bar = ((lam_bar - 1.0) / lam)[:, :, None] * lax.complex(b_re.astype(F32), b_im.astype(F32))
    eye = jnp.eye(gpt, dtype=F32)
    bb = b_bar.reshape(n_j, gpt, n_p, n_h)
    bre = jnp.einsum('jkph,gk->jghkp', bb.real, eye)
    bim = jnp.einsum('jkph,gk->jghkp', bb.imag, eye)
    bmat = jnp.stack([bre, bim], axis=3).reshape(n_j, LANES, 2 * gpt * n_p)
    cre = jnp.einsum('jghp,kg->jkpgh', c_re.astype(F32).reshape(n_j, gpt, n_h, n_p), eye)
    cim = jnp.einsum('jghp,kg->jkpgh', c_im.astype(F32).reshape(n_j, gpt, n_h, n_p), eye)
    cmat = jnp.stack([cre, -cim], axis=1).reshape(n_j, 2 * gpt * n_p, LANES)
    lre = lam_bar.real.reshape(n_j, 1, gpt * n_p)
    lim = lam_bar.imag.reshape(n_j, 1, gpt * n_p)
    return bmat.astype(BF16), cmat.astype(BF16), lre, lim


def _s5_scan(x, g, sc, sh, bmat, cmat, lre, lim, d_skip):
    nb, seq, d = x.shape
    n_j = d // LANES
    hw = lre.shape[-1]
    tc = _tile(seq, S5_CHUNK, 8)
    rows = tc * nb
    n_sub = S5_SUB if tc % S5_SUB == 0 else 1
    lt = S5_LANE_TILES if n_j % S5_LANE_TILES == 0 else 1
    kern = functools.partial(_s5_kernel, tc=tc, nb=nb, hw=hw)
    return pl.pallas_call(
        kern,
        grid=(seq // tc, n_j // lt),
        in_specs=[pl.BlockSpec((nb, tc, d), lambda c, j: (0, c, 0)),
                  pl.BlockSpec((1, d), lambda c, j: (0, 0)),
                  pl.BlockSpec((nb, 1, d), lambda c, j: (0, 0, 0)),
                  pl.BlockSpec((nb, 1, d), lambda c, j: (0, 0, 0)),
                  pl.BlockSpec((lt, LANES, 2 * hw), lambda c, j: (j, 0, 0)),
                  pl.BlockSpec((lt, 2 * hw, LANES), lambda c, j: (j, 0, 0)),
                  pl.BlockSpec((lt, 1, hw), lambda c, j: (j, 0, 0)),
                  pl.BlockSpec((lt, 1, hw), lambda c, j: (j, 0, 0)),
                  pl.BlockSpec((1, lt * LANES), lambda c, j: (0, j))],
        out_specs=pl.BlockSpec((nb, tc, lt * LANES), lambda c, j: (0, c, j)),
        out_shape=jax.ShapeDtypeStruct((nb, seq, d), BF16),
        scratch_shapes=[pltpu.VMEM((n_j, rows, LANES), F32),
                        pltpu.VMEM((n_j, nb, 2 * hw), F32),
                        pltpu.VMEM((lt, n_sub, rows // n_sub, 2 * hw), F32),
                        pltpu.VMEM((lt, n_sub, rows // n_sub, 2 * hw), F32),
                        pltpu.VMEM((lt, rows, LANES), F32)],
        compiler_params=_params("arbitrary", "arbitrary"),
        name="s5_scan",
    )(x, g, sc, sh, bmat, cmat, lre, lim, d_skip)


def _mm_res_kernel(a_ref, w_ref, x_ref, g_ref, o_ref, *, glu, tn):
    a = a_ref[...]
    d = o_ref.shape[1]
    for c0 in range(0, d, tn):
        p = _dot(a, w_ref[:, c0:c0 + tn])
        if glu:
            p = p * jax.nn.sigmoid(_dot(a, w_ref[:, d + c0:d + c0 + tn]))
        o_ref[:, c0:c0 + tn] = x_ref[:, c0:c0 + tn] + p * g_ref[0, :, c0:c0 + tn]


def _mm_res(a, w, x, gate, glu, seq):
    n, kdim = a.shape
    d = x.shape[1]
    tm = _tile(seq, 512, 16)
    tn = _tile(d, 512, LANES)
    return pl.pallas_call(
        functools.partial(_mm_res_kernel, glu=glu, tn=tn),
        grid=(n // tm,),
        in_specs=[pl.BlockSpec((tm, kdim), lambda i: (i, 0)),
                  pl.BlockSpec(w.shape, lambda i: (0, 0), pipeline_mode=pl.Buffered(1)),
                  pl.BlockSpec((tm, d), lambda i: (i, 0)),
                  _batch_spec(d, seq // tm)],
        out_specs=pl.BlockSpec((tm, d), lambda i: (i, 0)),
        out_shape=jax.ShapeDtypeStruct((n, d), F32),
        compiler_params=_params("parallel"),
        name="glu_res" if glu else "proj_res",
    )(a, w, x, gate)


def _hg_norm_kernel(x_ref, g_ref, sc_ref, sh_ref, o_ref):
    def emit(r0, h):
        o_ref[pl.ds(r0, h.shape[0]), :] = h.astype(BF16)

    _norm_mod_rows(x_ref, lambda: (g_ref[...], sc_ref[0], sh_ref[0]), emit)


def _hg_norm(x, g, sc, sh, seq):
    n, d = x.shape
    tm = _tile(seq, 1024, 16)
    bspec = _batch_spec(d, seq // tm)
    return pl.pallas_call(
        _hg_norm_kernel,
        grid=(n // tm,),
        in_specs=[pl.BlockSpec((tm, d), lambda i: (i, 0)), pl.BlockSpec((1, d), lambda i: (0, 0)), bspec, bspec],
        out_specs=pl.BlockSpec((tm, d), lambda i: (i, 0)),
        out_shape=jax.ShapeDtypeStruct((n, d), BF16),
        compiler_params=_params("parallel"),
        name="hg_norm",
    )(x, g, sc, sh)


def _hg_levels(ch):
    levels = []
    m = ch // 2
    while m >= 1:
        levels.append(m)
        m //= 2
    return levels


def _hg_masks(ch):
    levels = _hg_levels(ch)
    mk = np.zeros((len(levels) + 1, ch, ch), np.float32)
    for li, m in enumerate(levels):
        for t in range(ch):
            mid = (t // (2 * m)) * 2 * m + m - 1
            if t > mid:
                mk[li, t, mid - m + 1:mid + 1] = 1.0
    mk[-1] = np.eye(ch, dtype=np.float32)
    return mk


def _cumsum_rows(x, t_idx):
    sh = 1
    while sh < x.shape[0]:
        x = x + jnp.where(t_idx >= sh, pltpu.roll(x, sh, 0), 0.0)
        sh *= 2
    return x


def _level_ref(bc, m, sub3):
    ch, dk = bc.shape
    if 2 * m >= SUBLANES:
        bp = bc.reshape(ch // (2 * m), 2 * m, dk)
        return jnp.broadcast_to(bp[:, m - 1:m, :], bp.shape).reshape(ch, dk)
    b3 = bc.reshape(ch // SUBLANES, SUBLANES, dk)
    r = None
    for start in range(0, SUBLANES, 2 * m):
        cand = jnp.broadcast_to(b3[:, start + m - 1:start + m, :], b3.shape)
        r = cand if r is None else jnp.where(sub3 >= start, cand, r)
    return r.reshape(ch, dk)


def _hg_mix_kernel(h0_ref, hb_ref, hc_ref, w_ref, la_ref, l1_ref, mask_ref, gn_ref, o_ref, st_s, pa_s, pb_s,
                   *, nb, ch):
    d = h0_ref.shape[-1]

    def project(h_ref, e, c0, width):
        return _dot(h_ref[...].reshape(nb * ch, d), w_ref[e, :, c0:c0 + width])

    n_hd, _, cols = w_ref.shape

    @pl.when(pl.program_id(1) == 0)
    def _():
        st_s[...] = jnp.zeros(st_s.shape, F32)
        for e in range(n_hd):
            pa_s[:, e * cols:(e + 1) * cols] = project(h0_ref, e, 0, cols)

    args = (w_ref, la_ref, l1_ref, mask_ref, gn_ref, o_ref, st_s)
    for e in range(n_hd):
        _hg_chunk(hb_ref, *args, pa_s, pb_s, project, nb, ch, 0, e)
    for e in range(n_hd):
        _hg_chunk(hc_ref, *args, pb_s, pa_s, project, nb, ch, ch, e)


def _hg_chunk(hn_ref, w_ref, la_ref, l1_ref, mask_ref, gn_ref, o_ref, st_s, p_s, pn_s, project, nb, ch, r0, e):
    n_hd, _, cols = w_ref.shape
    dk = o_ref.shape[-1] // n_hd
    cb = e * cols
    assert nb % HG_GROUP == 0 and cols % (nb // HG_GROUP * LANES) == 0
    cpg = cols // (nb // HG_GROUP)

    la = la_ref[e]
    l1 = l1_ref[e]
    levels = _hg_levels(ch)
    n_lv = len(levels)
    gn = gn_ref[...]
    t_idx = lax.broadcasted_iota(I32, (ch, dk), 0)
    sub3 = lax.broadcasted_iota(I32, (ch // SUBLANES, SUBLANES, dk), 1)
    masks = [mask_ref[li] > 0.5 for li in range(n_lv + 1)]

    for b0 in range(0, nb, HG_GROUP):
        grp = range(b0, min(b0 + HG_GROUP, nb))
        c0 = (b0 // HG_GROUP) * cpg
        pn_s[:, cb + c0:cb + c0 + cpg] = project(hn_ref, e, c0, cpg)
        qk, qms, kms, q_in, k_out, decay = {}, {}, {}, {}, {}, {}
        for b in grp:
            q = p_s[b * ch:(b + 1) * ch, cb:cb + dk]
            z = p_s[b * ch:(b + 1) * ch, cb + dk:cb + 2 * dk]
            bt = l1 + jnp.minimum(z, 0.0) - jnp.log(1.0 + jnp.exp(-jnp.abs(z)))
            lf = jnp.maximum(la, bt) + jnp.log(1.0 + jnp.exp(-jnp.abs(la - bt)))
            bc = _cumsum_rows(lf, t_idx)
            qb = q.astype(BF16)
            kb = (1.0 - jnp.exp(lf)).astype(BF16)
            qk[b] = (qb, kb)
            qms[b], kms[b] = [], []
            for m in levels:
                decay_m = jnp.exp(-jnp.abs(bc - _level_ref(bc, m, sub3))).astype(BF16)
                qms[b].append(qb * decay_m)
                kms[b].append(kb * decay_m)
            b_end = bc[ch - 1:ch, :]
            q_in[b] = qb * jnp.exp(bc).astype(BF16)
            k_out[b] = kb * jnp.exp(b_end - bc).astype(BF16)
            decay[b] = jnp.exp(b_end)
        scores = {}
        for b in grp:
            s = jnp.where(masks[n_lv], _dot_nt(*qk[b]), 0.0)
            for li in range(n_lv):
                s = jnp.where(masks[li], _dot_nt(qms[b][li], kms[b][li]), s)
            scores[b] = s.astype(BF16)
        outs = {}
        for b in grp:
            v = p_s[b * ch:(b + 1) * ch, cb + 2 * dk:cb + 3 * dk].astype(BF16)
            st = st_s[e * nb + b]
            outs[b] = _dot(scores[b], v) + _dot_nt(q_in[b], st.astype(BF16))
            st_s[e * nb + b] = decay[b] * st + _dot_tn(v, k_out[b])
        for b in grp:
            o = outs[b]
            gate = p_s[b * ch:(b + 1) * ch, cb + 3 * dk:cb + 4 * dk]
            o = o * lax.rsqrt(jnp.mean(o * o, axis=-1, keepdims=True) + EPS) * gn
            o_ref[b, r0:r0 + ch, e * dk:(e + 1) * dk] = (o * (gate * jax.nn.sigmoid(gate))).astype(BF16)


def _hg_mix(h, w_heads, log_lb, log1m_lb, g_norm):
    nb, seq, d = h.shape
    heads, _, cols = w_heads.shape
    dk = cols // 4
    ch = _tile(seq, HG_CHUNK, 16)
    nc = seq // ch
    assert nc % 2 == 0, "two chunks per grid step"
    hp = HG_HEADS_PER_STEP if heads % HG_HEADS_PER_STEP == 0 else 1
    mk = _hg_masks(ch)
    per_head = lambda w: pl.BlockSpec((hp,) + w.shape[1:], lambda hd, s: (hd, 0, 0))
    return pl.pallas_call(
        functools.partial(_hg_mix_kernel, nb=nb, ch=ch),
        grid=(heads // hp, nc // 2),
        in_specs=[pl.BlockSpec((nb, ch, d), lambda hd, s: (0, 0, 0)),
                  pl.BlockSpec((nb, ch, d), lambda hd, s: (0, 2 * s + 1, 0)),
                  pl.BlockSpec((nb, ch, d), lambda hd, s: (0, jnp.minimum(2 * s + 2, nc - 1), 0)),
                  per_head(w_heads), per_head(log_lb), per_head(log1m_lb),
                  pl.BlockSpec(mk.shape, lambda hd, s: (0, 0, 0)),
                  pl.BlockSpec((1, dk), lambda hd, s: (0, 0))],
        out_specs=pl.BlockSpec((nb, 2 * ch, hp * dk), lambda hd, s: (0, s, hd)),
        out_shape=jax.ShapeDtypeStruct((nb, seq, heads * dk), BF16),
        scratch_shapes=[pltpu.VMEM((hp * nb, dk, dk), F32), pltpu.VMEM((nb * ch, hp * cols), F32),
                        pltpu.VMEM((nb * ch, hp * cols), F32)],
        compiler_params=_params("parallel", "arbitrary"),
        name="hg_mix",
    )(h, h, h, w_heads, log_lb, log1m_lb, jnp.asarray(mk, F32), g_norm)


def _router_kernel(x_ref, g_ref, sc_ref, sh_ref, wh_ref, wl_ref, br_ref, tri_ref,
                   hp_ref, idx_ref, gate_ref, rank_ref, cnt_ref):
    h = _norm_mod(x_ref[...], g_ref[...], sc_ref[0], sh_ref[0])
    half = h.shape[1] // 2
    _store_tiles(hp_ref, 0, _pack_pair(h[:, :half], h[:, half:]))
    hh, hl = _split(h)
    wh = wh_ref[...]
    vals = _dot(hh, wh) + _dot(hl, wh) + _dot(hh, wl_ref[...]) + br_ref[...]
    tm, n_exp = vals.shape
    lane = lax.broadcasted_iota(I32, (tm, n_exp), 1)
    tops, idxs, hots = [], [], []
    for _ in range(TOP_K):
        m = jnp.max(vals, axis=-1, keepdims=True)
        i = jnp.min(jnp.where(vals == m, lane, n_exp), axis=-1, keepdims=True)
        tops.append(m)
        idxs.append(i)
        hots.append(lane == i)
        vals = jnp.where(hots[-1], -jnp.inf, vals)
    es = [jnp.exp(t - tops[0]) for t in tops]
    den = es[0] + es[1] + es[2] + es[3]

    sel = [jnp.where(hm, 1.0, 0.0) for hm in hots]
    multi = sel[0] + sel[1] + sel[2] + sel[3]
    base = _dot(tri_ref[...], multi.astype(BF16))
    cnt_ref[0] = jnp.sum(multi, axis=0, keepdims=True)

    lane_o = lax.broadcasted_iota(I32, idx_ref.shape, 1)
    io = jnp.zeros(idx_ref.shape, I32)
    ro = jnp.zeros(idx_ref.shape, I32)
    go = jnp.zeros(gate_ref.shape, F32)
    for kk in range(TOP_K):
        rk = jnp.sum(sel[kk] * base, axis=-1, keepdims=True).astype(I32)
        io = jnp.where(lane_o == kk, idxs[kk], io)
        ro = jnp.where(lane_o == kk, rk, ro)
        go = jnp.where(lane_o == kk, es[kk] / den, go)
    idx_ref[...] = io
    rank_ref[...] = ro
    gate_ref[...] = go


def _router(x, g, sc, sh, w_r, b_r, seq):
    n, d = x.shape
    n_exp = w_r.shape[1]
    tm = _tile(seq, 512, 16)
    bspec = _batch_spec(d, seq // tm)
    pcs = d // 2 // LANES
    wh, wl = _split(w_r.astype(F32))
    tri = jnp.asarray(np.tril(np.ones((tm, tm), np.float32), -1), BF16)
    row = lambda w: pl.BlockSpec((tm, w), lambda i: (i, 0))
    full = lambda a: pl.BlockSpec(a.shape, lambda i: (0, 0))
    b2 = b_r.astype(F32).reshape(1, n_exp)
    return pl.pallas_call(
        _router_kernel,
        grid=(n // tm,),
        in_specs=[row(d), full(g), bspec, bspec, full(wh), full(wl), full(b2), full(tri)],
        out_specs=[pl.BlockSpec((tm * pcs, LANES), lambda i: (i, 0)), row(LANES), row(LANES), row(LANES),
                   pl.BlockSpec((1, 1, n_exp), lambda i: (i, 0, 0))],
        out_shape=[jax.ShapeDtypeStruct((n * pcs, LANES), I32),
                   jax.ShapeDtypeStruct((n, LANES), I32),
                   jax.ShapeDtypeStruct((n, LANES), F32),
                   jax.ShapeDtypeStruct((n, LANES), I32),
                   jax.ShapeDtypeStruct((n // tm, 1, n_exp), F32)],
        compiler_params=_params("parallel"),
        name="router",
    )(x, g, sc, sh, wh, wl, b2, tri)


def _moe_plan(top_idx, lrank, tile_counts, tmb):
    n = top_idx.shape[0]
    n_tiles, _, n_exp = tile_counts.shape
    tc = tile_counts.reshape(n_tiles, n_exp).astype(I32)
    counts = jnp.sum(tc, axis=0)
    nblk = (counts + tmb - 1) // tmb
    blk_end = jnp.cumsum(nblk)
    blk_start = blk_end - nblk
    n_used = blk_end[-1]
    n_blocks = n * TOP_K // tmb + n_exp
    blk = jnp.arange(n_blocks, dtype=I32)
    be = jnp.minimum(jnp.sum((blk[:, None] >= blk_end[None, :]).astype(I32), axis=1), n_exp - 1)
    last = jnp.sum(jnp.where(blk == n_used - 1, be, 0))
    be = jnp.where(blk < n_used, be, last).astype(I32)
    nxt = jnp.sum(jnp.where(blk[None, :] == jnp.minimum(blk_end[be], n_used - 1)[:, None], be[None, :], 0), axis=1)
    base = blk_start * tmb
    seg_dst = base[None, :] + jnp.cumsum(tc, axis=0) - tc
    seg_off = jnp.cumsum(tc, axis=1) - tc
    tm = n // n_tiles
    hot = top_idx.reshape(n_tiles, tm, TOP_K, 1) == jnp.arange(n_exp, dtype=I32)
    lpos = jnp.sum(jnp.where(hot, seg_off[:, None, None, :], 0), axis=-1) + lrank.reshape(n_tiles, tm, TOP_K)
    pad_start = base + counts
    pad_len = blk_end * tmb - pad_start
    flat = lambda a: a.reshape(-1).astype(I32)
    return dict(be=be, nxt=nxt.astype(I32), n_used=n_used.reshape(1).astype(I32),
                lpos=lpos.reshape(n_tiles, 1, tm * TOP_K).astype(I32),
                seg_cnt=flat(tc), seg_off=flat(seg_off), seg_dst=flat(seg_dst),
                pad_start=pad_start.astype(I32), pad_len=pad_len.astype(I32),
                n_blocks=n_blocks, tmb=tmb, n_tiles=n_tiles, n_exp=n_exp)


def _tile_at(ref, row, pcs):
    return ref.at[pl.ds(pl.multiple_of(row * pcs, pcs), pcs)]


def _segments(i, cnt_ref, off_ref, dst_ref, n_exp, pcs, copy):
    def seg(e, carry):
        cnt = cnt_ref[i * n_exp + e]

        @pl.when(cnt > 0)
        def _():
            size = pl.multiple_of(cnt * pcs, pcs)
            copy(pl.ds(pl.multiple_of(off_ref[i * n_exp + e] * pcs, pcs), size),
                 pl.ds(pl.multiple_of(dst_ref[i * n_exp + e] * pcs, pcs), size))

        return carry

    lax.fori_loop(0, n_exp, seg, 0)


def _dispatch_kernel(sc_ref, so_ref, sd_ref, ps_ref, pn_ref, nu_ref, lpos_ref, hp_ref, xs_hbm,
                     sbuf, zero_s, sems, sem, *, tmb, pcs):
    i = pl.program_id(0)
    last = pl.num_programs(0) - 1
    tm = hp_ref.shape[0] // pcs
    slot = i % 2

    def drain(s):
        pltpu.make_async_copy(sbuf.at[s], xs_hbm.at[pl.ds(0, sbuf.shape[1])], sems.at[s]).wait()

    @pl.when(i >= 2)
    def _():
        drain(slot)

    def row(r, carry):
        tile = hp_ref[pl.ds(pl.multiple_of(r * pcs, pcs), pcs), :]
        for kk in range(TOP_K):
            p = lpos_ref[0, 0, TOP_K * r + kk]
            sbuf[slot, pl.ds(pl.multiple_of(p * pcs, pcs), pcs), :] = tile
        return carry

    lax.fori_loop(0, tm, row, 0, unroll=8)

    def copy(src, dst):
        pltpu.make_async_copy(sbuf.at[slot, src], xs_hbm.at[dst], sems.at[slot]).start()

    _segments(i, sc_ref, so_ref, sd_ref, ps_ref.shape[0], pcs, copy)

    @pl.when(i == last)
    def _():
        drain(slot)

        @pl.when(i >= 1)
        def _():
            drain(1 - slot)

        zero_s[...] = jnp.zeros(zero_s.shape, I32)

        size = tmb // 2
        while size >= 1:
            def chunk(e, size=size):
                cnt = pn_ref[e]
                before = cnt - cnt % (2 * size)
                dst = pl.ds(pl.multiple_of((ps_ref[e] + before) * pcs, pcs), size * pcs)
                return (cnt // size) % 2 == 1, pltpu.make_async_copy(
                    zero_s.at[pl.ds(0, size * pcs)], xs_hbm.at[dst], sem)

            def fill(e, carry):
                take, cp = chunk(e)

                @pl.when(take)
                def _():
                    cp.start()

                return carry

            def fill_done(e, carry):
                take, cp = chunk(e)

                @pl.when(take)
                def _():
                    cp.wait()

                return carry

            lax.fori_loop(0, ps_ref.shape[0], fill, 0)
            lax.fori_loop(0, ps_ref.shape[0], fill_done, 0)
            size //= 2

        def block(bk, carry):
            cp = pltpu.make_async_copy(zero_s, _tile_at(xs_hbm, bk, tmb * pcs), sem)
            cp.start()
            cp.wait()
            return carry

        lax.fori_loop(nu_ref[0], xs_hbm.shape[0] // (tmb * pcs), block, 0)


def _dispatch(hp, plan, n):
    pcs = hp.shape[0] // n
    tmb = plan['tmb']
    n_tiles = plan['n_tiles']
    tm = n // n_tiles
    grid_spec = pltpu.PrefetchScalarGridSpec(
        num_scalar_prefetch=6,
        grid=(n_tiles,),
        in_specs=[pl.BlockSpec((1, 1, TOP_K * tm), lambda i, *_: (i, 0, 0), memory_space=pltpu.SMEM),
                  pl.BlockSpec((tm * pcs, LANES), lambda i, *_: (i, 0))],
        out_specs=pl.BlockSpec(memory_space=pl.ANY),
        scratch_shapes=[pltpu.VMEM((2, TOP_K * tm * pcs, LANES), I32),
                        pltpu.VMEM((tmb * pcs, LANES), I32),
                        pltpu.SemaphoreType.DMA((2,)), pltpu.SemaphoreType.DMA(())],
    )
    return pl.pallas_call(
        functools.partial(_dispatch_kernel, tmb=tmb, pcs=pcs),
        grid_spec=grid_spec,
        out_shape=jax.ShapeDtypeStruct((plan['n_blocks'] * tmb * pcs, LANES), I32),
        compiler_params=_params("arbitrary"),
        name="moe_dispatch",
    )(plan['seg_cnt'], plan['seg_off'], plan['seg_dst'], plan['pad_start'], plan['pad_len'], plan['n_used'],
      plan['lpos'], hp)


def _moe_kernel(be_ref, nx_ref, nu_ref, x_ref, wgu_hbm, bgu_ref, wdn_hbm, bdn_ref, y_ref,
                wgu_f, wdn_f, wgu_s, wdn_s, sems, *, ff, layer):
    i = pl.program_id(0)
    used = i < nu_ref[0]

    def fetch(e):
        return (pltpu.make_async_copy(wgu_hbm.at[layer, e], wgu_f, sems.at[0]),
                pltpu.make_async_copy(wdn_hbm.at[layer, e], wdn_f, sems.at[1]))

    @pl.when(used)
    def _():
        e = be_ref[i]
        e_new = jnp.logical_or(i == 0, e != be_ref[jnp.maximum(i - 1, 0)])

        @pl.when(i == 0)
        def _():
            for cp in fetch(e):
                cp.start()

        @pl.when(e_new)
        def _():
            for cp in fetch(e):
                cp.wait()
            wgu_s[...] = wgu_f[...].astype(BF16)
            wdn_s[...] = wdn_f[...].astype(BF16)
            nxt = nx_ref[i]

            @pl.when(nxt != e)
            def _():
                for cp in fetch(nxt):
                    cp.start()

        half = wgu_s.shape[0] // 2
        pcs = half // LANES
        tmb = x_ref.shape[0] // pcs
        x_lo, x_hi = _unpack_pair(_load_tiles(x_ref, 0, tmb, pcs))
        gu = (_dot(x_lo.astype(BF16), wgu_s[:half, :]) + _dot(x_hi.astype(BF16), wgu_s[half:, :])
              + bgu_ref[0, 0])
        gt = jnp.minimum(gu[:, :ff], SWIGLU_LIMIT)
        up = jnp.clip(gu[:, ff:], -SWIGLU_LIMIT, SWIGLU_LIMIT)
        act = ((up + 1.0) * gt * jax.nn.sigmoid(SWIGLU_ALPHA * gt)).astype(BF16)
        bdn = bdn_ref[0, 0]
        y_lo = _dot(act, wdn_s[:, :half]) + bdn[:, :half]
        y_hi = _dot(act, wdn_s[:, half:]) + bdn[:, half:]
        _store_tiles(y_ref, 0, _pack_pair(y_lo, y_hi))

    @pl.when(jnp.logical_not(used))
    def _():
        y_ref[...] = jnp.zeros(y_ref.shape, I32)


def _moe_experts(xs, plan, w_gu, b_gu, w_dn, b_dn, layer):
    depth, n_exp, d, ff2 = w_gu.shape
    ff = ff2 // 2
    tmb = plan['tmb']
    n_blocks = plan['n_blocks']
    rows = xs.shape[0] // n_blocks
    wmap = lambda i, be, nx, nu: (layer, be[i], 0, 0)
    grid_spec = pltpu.PrefetchScalarGridSpec(
        num_scalar_prefetch=3,
        grid=(n_blocks,),
        in_specs=[pl.BlockSpec((rows, LANES), lambda i, *_: (i, 0)),
                  pl.BlockSpec(memory_space=pl.ANY),
                  pl.BlockSpec((1, 1, 1, ff2), wmap),
                  pl.BlockSpec(memory_space=pl.ANY),
                  pl.BlockSpec((1, 1, 1, d), wmap)],
        out_specs=pl.BlockSpec((rows, LANES), lambda i, *_: (i, 0)),
        scratch_shapes=[pltpu.VMEM((d, ff2), F32), pltpu.VMEM((ff, d), F32),
                        pltpu.VMEM((d, ff2), BF16), pltpu.VMEM((ff, d), BF16),
                        pltpu.SemaphoreType.DMA((2,))],
    )
    return pl.pallas_call(
        functools.partial(_moe_kernel, ff=ff, layer=layer),
        grid_spec=grid_spec,
        out_shape=jax.ShapeDtypeStruct(xs.shape, I32),
        compiler_params=_params("arbitrary"),
        name="moe_experts",
    )(plan['be'], plan['nxt'], plan['n_used'], xs, w_gu, b_gu.reshape(depth, n_exp, 1, ff2),
      w_dn, b_dn.reshape(depth, n_exp, 1, d))


def _combine_kernel(sc_ref, so_ref, sd_ref, lpos_ref, x_ref, gate_ref, g_ref, nf_ref, ys_hbm, o_ref,
                    ybuf, gbuf, sems, *, n_exp, final):
    i = pl.program_id(0)
    tm, d = x_ref.shape
    half = d // 2
    pcs = half // LANES
    slot = i % 2

    def fetch(t, s):
        def copy(loc, blk):
            pltpu.make_async_copy(ys_hbm.at[blk], ybuf.at[s, loc], sems.at[s]).start()

        _segments(t, sc_ref, so_ref, sd_ref, n_exp, pcs, copy)

    @pl.when(i == 0)
    def _():
        fetch(i, slot)

    @pl.when(i + 1 < pl.num_programs(0))
    def _():
        fetch(i + 1, 1 - slot)

    pltpu.make_async_copy(ys_hbm.at[pl.ds(0, ybuf.shape[1])], ybuf.at[slot], sems.at[slot]).wait()

    def row(r, carry):
        for kk in range(TOP_K):
            p = lpos_ref[0, 0, TOP_K * r + kk]
            gbuf[kk, pl.ds(pl.multiple_of(r * pcs, pcs), pcs), :] = (
                ybuf[slot, pl.ds(pl.multiple_of(p * pcs, pcs), pcs), :])
        return carry

    lax.fori_loop(0, tm, row, 0, unroll=8)

    gt = gate_ref[...]
    lo = hi = None
    for kk in range(TOP_K):
        w = gt[:, kk:kk + 1]
        y_lo, y_hi = _unpack_pair(_load_tiles(gbuf.at[kk], 0, tm, pcs))
        lo = w * y_lo if lo is None else lo + w * y_lo
        hi = w * y_hi if hi is None else hi + w * y_hi
    g = g_ref[0]
    x_lo = x_ref[:, :half] + lo * g[:, :half]
    x_hi = x_ref[:, half:] + hi * g[:, half:]
    if final:
        ms = (jnp.sum(x_lo * x_lo, axis=-1, keepdims=True) + jnp.sum(x_hi * x_hi, axis=-1, keepdims=True)) / d
        r = lax.rsqrt(ms + EPS)
        x_lo = (x_lo * r) * nf_ref[:, :half]
        x_hi = (x_hi * r) * nf_ref[:, half:]
    o_ref[:, :half] = x_lo
    o_ref[:, half:] = x_hi


def _combine(x, ys, gates, g2, plan, seq, norm_final, final):
    n, d = x.shape
    pcs = d // 2 // LANES
    nt = plan['n_tiles']
    tm = n // nt
    grid_spec = pltpu.PrefetchScalarGridSpec(
        num_scalar_prefetch=3,
        grid=(nt,),
        in_specs=[pl.BlockSpec((1, 1, TOP_K * tm), lambda i, *_: (i, 0, 0), memory_space=pltpu.SMEM),
                  pl.BlockSpec((tm, d), lambda i, *_: (i, 0)),
                  pl.BlockSpec((tm, LANES), lambda i, *_: (i, 0)),
                  _batch_spec(d, seq // tm),
                  pl.BlockSpec((1, d), lambda i, *_: (0, 0)),
                  pl.BlockSpec(memory_space=pl.ANY)],
        out_specs=pl.BlockSpec((tm, d), lambda i, *_: (i, 0)),
        scratch_shapes=[pltpu.VMEM((2, TOP_K * tm * pcs, LANES), I32),
                        pltpu.VMEM((TOP_K, tm * pcs, LANES), I32),
                        pltpu.SemaphoreType.DMA((2,))],
    )
    return pl.pallas_call(
        functools.partial(_combine_kernel, n_exp=plan['n_exp'], final=final),
        grid_spec=grid_spec,
        out_shape=jax.ShapeDtypeStruct((n, d), F32),
        compiler_params=_params("arbitrary"),
        name="moe_combine",
    )(plan['seg_cnt'], plan['seg_off'], plan['seg_dst'], plan['lpos'], x, gates, g2, norm_final, ys)


def kernel(x, c, ada_w, ada_b, norm_mix, norm_ffn, norm_final, s5_lambda_re, s5_lambda_im, s5_log_dt,
           s5_b_re, s5_b_im, s5_c_re, s5_c_im, s5_d, s5_w_glu, hg_w_in, hg_lb_raw, hg_norm, hg_w_out,
           router_w, router_b, moe_w_gate_up, moe_b_gate_up, moe_w_down, moe_b_down):
    bsz, seq, d = x.shape
    depth = ada_w.shape[0]
    n = bsz * seq
    dv = hg_norm.shape[-1]
    heads = d // dv
    fdim = (hg_w_in.shape[-1] - 2 * d) // 2
    dk = fdim // heads
    assert dk == LANES and dv == LANES, "one HGRN2 head per 128-lane tile"
    tmb = _tile(n * TOP_K, MOE_ROWS, SUBLANES)

    lb_p = jax.nn.softmax(hg_lb_raw.astype(F32), axis=0)
    lb_all = jnp.cumsum(lb_p, axis=0) - lb_p[0]
    mod = _ada_mod(c.astype(F32), ada_w, ada_b)

    xs = x.astype(F32).reshape(n, d)
    row = lambda v: v.astype(F32).reshape(1, -1)
    for i in range(depth):
        sh1, sc1, g1, sh2, sc2, g2 = [mod[i, :, k * d:(k + 1) * d].reshape(bsz, 1, d) for k in range(N_ADA)]
        j = i // 2
        if i % 2 == 0:
            bmat, cmat, lre, lim = _s5_prep(s5_lambda_re[j], s5_lambda_im[j], s5_log_dt[j],
                                            s5_b_re[j], s5_b_im[j], s5_c_re[j], s5_c_im[j])
            z = _s5_scan(xs.reshape(bsz, seq, d), row(norm_mix[i]), sc1, sh1, bmat, cmat, lre, lim, row(s5_d[j]))
            xs = _mm_res(z.reshape(n, d), s5_w_glu[j].astype(BF16), xs, g1, True, seq)
        else:
            lb = lb_all[i].reshape(heads, 1, dk)
            w_heads = hg_w_in[j].astype(BF16).reshape(d, 4, heads, dk).transpose(2, 0, 1, 3).reshape(heads, d, 4 * dk)
            h = _hg_norm(xs, row(norm_mix[i]), sc1, sh1, seq)
            o = _hg_mix(h.reshape(bsz, seq, d), w_heads, jnp.log(lb), jnp.log1p(-lb), row(hg_norm[j]))
            xs = _mm_res(o.reshape(n, d), hg_w_out[j].astype(BF16), xs, g1, False, seq)
        hp, top_idx, gates, rank, counts = _router(xs, row(norm_ffn[i]), sc2, sh2, router_w[i], router_b[i], seq)
        plan = _moe_plan(top_idx[:, :TOP_K], rank[:, :TOP_K], counts, tmb)
        xd = _dispatch(hp, plan, n)
        ys = _moe_experts(xd, plan, moe_w_gate_up, moe_b_gate_up.astype(F32),
                          moe_w_down, moe_b_down.astype(F32), i)
        xs = _combine(xs, ys, gates, g2, plan, seq, row(norm_final), final=i == depth - 1)
    return xs.reshape(bsz, seq, d).astype(x.dtype)
```

```python
import functools

import numpy as np
import jax
import jax.numpy as jnp
from jax import lax
from jax.experimental import pallas as pl
from jax.experimental.pallas import tpu as pltpu

F32 = jnp.float32
BF16 = jnp.bfloat16
I32 = jnp.int32

EPS = 1e-6
N_ADA = 6
TOP_K = 4
S5_RE_MAX = -1e-4
SWIGLU_LIMIT = 7.0
SWIGLU_ALPHA = 1.702
GELU_C0 = 0.7978845608028654
GELU_C1 = 0.044715

LANES = 128
SUBLANES = 8
HG_CHUNK = 64
HG_HEADS_PER_STEP = 4
HG_GROUP = 4
S5_CHUNK = 128
S5_LANE_TILES = 2
S5_SUB = 4
NORM_ROWS = 32
MOE_ROWS = 512
HI16 = -65536
VMEM_LIMIT = 56 * 1024 * 1024


def _params(*sem):
    return pltpu.CompilerParams(dimension_semantics=sem, vmem_limit_bytes=VMEM_LIMIT)


def _tile(n, pref, align):
    if n <= pref:
        return n
    t = (pref // align) * align
    while t > align and n % t:
        t -= align
    assert n % t == 0, (n, pref, align)
    return t


def _dot(a, b):
    return jnp.dot(a, b, preferred_element_type=F32)


def _dot_nt(a, b):
    return lax.dot_general(a, b, (((1,), (1,)), ((), ())), preferred_element_type=F32)


def _dot_tn(a, b):
    return lax.dot_general(a, b, (((0,), (0,)), ((), ())), preferred_element_type=F32)


def _split(a):
    hi = a.astype(BF16)
    lo = (a - hi.astype(F32)).astype(BF16)
    return hi, lo


def _dot3(a, w):
    ah, al = _split(a)
    wh, wl = _split(w)
    return _dot(ah, wh) + _dot(al, wh) + _dot(ah, wl)


def _norm_mod(x, g, sc, sh):
    ms = jnp.mean(x * x, axis=-1, keepdims=True)
    return ((x * lax.rsqrt(ms + EPS)) * g) * (1.0 + sc) + sh


def _norm_mod_rows(x_ref, mods, emit):
    rows = x_ref.shape[0]
    ch = min(NORM_ROWS, rows)

    def body(i, carry):
        r0 = pl.multiple_of(i * ch, ch)
        emit(r0, _norm_mod(x_ref[pl.ds(r0, ch), :], *mods()))
        return carry

    lax.fori_loop(0, rows // ch, body, 0, unroll=4 if (rows // ch) % 4 == 0 else 1)


def _batch_spec(width, tiles_per_batch):
    return pl.BlockSpec((1, 1, width), lambda i, *_: (i // tiles_per_batch, 0, 0))


def _pack_pair(lo, hi):
    lo_b = lax.bitcast_convert_type(lo.astype(BF16).astype(F32), I32)
    hi_b = lax.bitcast_convert_type(hi.astype(BF16).astype(F32), I32)
    return lax.shift_right_logical(lo_b, 16) | (hi_b & HI16)


def _unpack_pair(p):
    lo = lax.bitcast_convert_type(lax.shift_left(p, 16), F32)
    hi = lax.bitcast_convert_type(p & HI16, F32)
    return lo, hi


def _store_tiles(ref, row0, val):
    rows, width = val.shape
    pcs = width // LANES
    for jj in range(pcs):
        ref[pl.ds(row0 * pcs + jj, rows, stride=pcs), :] = val[:, jj * LANES:(jj + 1) * LANES]


def _load_tiles(ref, row0, rows, pcs):
    return jnp.concatenate([ref[pl.ds(row0 * pcs + jj, rows, stride=pcs), :] for jj in range(pcs)], axis=1)


def _ada_kernel(c_ref, w_ref, b_ref, o_ref):
    c = c_ref[...]
    o_ref[0] = _dot3(c * jax.nn.sigmoid(c), w_ref[0]) + b_ref[0]


def _ada_mod(c, ada_w, ada_b):
    depth, d, nd = ada_w.shape
    nb = c.shape[0]
    tn = _tile(nd, 768, LANES)
    return pl.pallas_call(
        _ada_kernel,
        grid=(depth, nd // tn),
        in_specs=[pl.BlockSpec((nb, d), lambda i, j: (0, 0)),
                  pl.BlockSpec((1, d, tn), lambda i, j: (i, 0, j)),
                  pl.BlockSpec((1, 1, tn), lambda i, j: (i, 0, j))],
        out_specs=pl.BlockSpec((1, nb, tn), lambda i, j: (i, 0, j)),
        out_shape=jax.ShapeDtypeStruct((depth, nb, nd), F32),
        compiler_params=_params("parallel", "parallel"),
        name="ada_mod",
    )(c, ada_w, ada_b.reshape(depth, 1, nd))


def _s5_kernel(x_ref, g_ref, sc_ref, sh_ref, bmat_ref, cmat_ref, lre_ref, lim_ref, d_ref,
               z_ref, h_s, st_s, bu_s, sa_s, z_s, *, tc, nb, hw):
    c = pl.program_id(0)
    j = pl.program_id(1)
    n_j = h_s.shape[0]

    @pl.when(j == 0)
    def _():
        for b in range(nb):
            def emit(r0, h, b=b):
                for jj in range(n_j):
                    h_s[jj, pl.ds(r0 * nb + b, h.shape[0], stride=nb), :] = h[:, jj * LANES:(jj + 1) * LANES]

            _norm_mod_rows(x_ref.at[b], lambda b=b: (g_ref[...], sc_ref[b], sh_ref[b]), emit)

    n_lt, n_sub = bu_s.shape[:2]
    ts = tc // n_sub
    rs = ts * nb

    @pl.when(c == 0)
    def _():
        for l in range(n_lt):
            st_s[j * n_lt + l] = jnp.zeros(st_s.shape[1:], F32)

    for l in range(n_lt):
        bmat = bmat_ref[l]
        for k in range(n_sub):
            bu_s[l, k] = _dot(h_s[j * n_lt + l, k * rs:(k + 1) * rs, :].astype(BF16), bmat)
    for l in range(n_lt):
        jl = j * n_lt + l
        cmat = cmat_ref[l]
        lre = jnp.broadcast_to(lre_ref[l], (nb, hw))
        lim = jnp.broadcast_to(lim_ref[l], (nb, hw))
        st = st_s[jl]
        s_re, s_im = st[:, :hw], st[:, hw:]
        for k in range(n_sub):
            for t in range(ts):
                b = bu_s[l, k, t * nb:(t + 1) * nb, :]
                s_re, s_im = (lre * s_re - lim * s_im + b[:, :hw],
                              lre * s_im + lim * s_re + b[:, hw:])
                sa_s[l, k, t * nb:(t + 1) * nb, :hw] = s_re
                sa_s[l, k, t * nb:(t + 1) * nb, hw:] = s_im
            y = (_dot(sa_s[l, k].astype(BF16), cmat)
                 + d_ref[:, l * LANES:(l + 1) * LANES] * h_s[jl, k * rs:(k + 1) * rs, :])
            z = 0.5 * y * (1.0 + jnp.tanh(GELU_C0 * (y + GELU_C1 * (y * y * y))))
            z_s[l, k * rs:(k + 1) * rs, :] = z
        st_s[jl, :, :hw] = s_re
        st_s[jl, :, hw:] = s_im
    for b in range(nb):
        for l in range(n_lt):
            z_ref[b, :, l * LANES:(l + 1) * LANES] = z_s[l, pl.ds(b, tc, stride=nb), :].astype(BF16)


def _s5_prep(lam_re, lam_im, log_dt, b_re, b_im, c_re, c_im):
    n_g, n_p = lam_re.shape
    n_h = b_re.shape[-1]
    gpt = LANES // n_h
    n_j = n_g // gpt
    lam = lax.complex(jnp.minimum(lam_re.astype(F32), S5_RE_MAX), lam_im.astype(F32))
    dt = jnp.exp(log_dt.astype(F32))[:, None]
    lam_bar = jnp.exp(lam * dt)
    b_bar = ((lam_bar - 1.0) / lam)[:, :, None] * lax.complex(b_re.astype(F32), b_im.astype(F32))
    eye = jnp.eye(gpt, dtype=F32)
    bb = b_bar.reshape(n_j, gpt, n_p, n_h)
    bre = jnp.einsum('jkph,gk->jghkp', bb.real, eye)
    bim = jnp.einsum('jkph,gk->jghkp', bb.imag, eye)
    bmat = jnp.stack([bre, bim], axis=3).reshape(n_j, LANES, 2 * gpt * n_p)
    cre = jnp.einsum('jghp,kg->jkpgh', c_re.astype(F32).reshape(n_j, gpt, n_h, n_p), eye)
    cim = jnp.einsum('jghp,kg->jkpgh', c_im.astype(F32).reshape(n_j, gpt, n_h, n_p), eye)
    cmat = jnp.stack([cre, -cim], axis=1).reshape(n_j, 2 * gpt * n_p, LANES)
    lre = lam_bar.real.reshape(n_j, 1, gpt * n_p)
    lim = lam_bar.imag.reshape(n_j, 1, gpt * n_p)
    return bmat.astype(BF16), cmat.astype(BF16), lre, lim


def _s5_scan(x, g, sc, sh, bmat, cmat, lre, lim, d_skip):
    nb, seq, d = x.shape
    n_j = d // LANES
    hw = lre.shape[-1]
    tc = _tile(seq, S5_CHUNK, 8)
    rows = tc * nb
    n_sub = S5_SUB if tc % S5_SUB == 0 else 1
    lt = S5_LANE_TILES if n_j % S5_LANE_TILES == 0 else 1
    kern = functools.partial(_s5_kernel, tc=tc, nb=nb, hw=hw)
    return pl.pallas_call(
        kern,
        grid=(seq // tc, n_j // lt),
        in_specs=[pl.BlockSpec((nb, tc, d), lambda c, j: (0, c, 0)),
                  pl.BlockSpec((1, d), lambda c, j: (0, 0)),
                  pl.BlockSpec((nb, 1, d), lambda c, j: (0, 0, 0)),
                  pl.BlockSpec((nb, 1, d), lambda c, j: (0, 0, 0)),
                  pl.BlockSpec((lt, LANES, 2 * hw), lambda c, j: (j, 0, 0)),
                  pl.BlockSpec((lt, 2 * hw, LANES), lambda c, j: (j, 0, 0)),
                  pl.BlockSpec((lt, 1, hw), lambda c, j: (j, 0, 0)),
                  pl.BlockSpec((lt, 1, hw), lambda c, j: (j, 0, 0)),
                  pl.BlockSpec((1, lt * LANES), lambda c, j: (0, j))],
        out_specs=pl.BlockSpec((nb, tc, lt * LANES), lambda c, j: (0, c, j)),
        out_shape=jax.ShapeDtypeStruct((nb, seq, d), BF16),
        scratch_shapes=[pltpu.VMEM((n_j, rows, LANES), F32),
                        pltpu.VMEM((n_j, nb, 2 * hw), F32),
                        pltpu.VMEM((lt, n_sub, rows // n_sub, 2 * hw), F32),
                        pltpu.VMEM((lt, n_sub, rows // n_sub, 2 * hw), F32),
                        pltpu.VMEM((lt, rows, LANES), F32)],
        compiler_params=_params("arbitrary", "arbitrary"),
        name="s5_scan",
    )(x, g, sc, sh, bmat, cmat, lre, lim, d_skip)


def _mm_res_kernel(a_ref, w_ref, x_ref, g_ref, o_ref, *, glu, tn):
    a = a_ref[...]
    d = o_ref.shape[1]
    for c0 in range(0, d, tn):
        p = _dot(a, w_ref[:, c0:c0 + tn])
        if glu:
            p = p * jax.nn.sigmoid(_dot(a, w_ref[:, d + c0:d + c0 + tn]))
        o_ref[:, c0:c0 + tn] = x_ref[:, c0:c0 + tn] + p * g_ref[0, :, c0:c0 + tn]


def _mm_res(a, w, x, gate, glu, seq):
    n, kdim = a.shape
    d = x.shape[1]
    tm = _tile(seq, 512, 16)
    tn = _tile(d, 512, LANES)
    return pl.pallas_call(
        functools.partial(_mm_res_kernel, glu=glu, tn=tn),
        grid=(n // tm,),
        in_specs=[pl.BlockSpec((tm, kdim), lambda i: (i, 0)),
                  pl.BlockSpec(w.shape, lambda i: (0, 0), pipeline_mode=pl.Buffered(1)),
                  pl.BlockSpec((tm, d), lambda i: (i, 0)),
                  _batch_spec(d, seq // tm)],
        out_specs=pl.BlockSpec((tm, d), lambda i: (i, 0)),
        out_shape=jax.ShapeDtypeStruct((n, d), F32),
        compiler_params=_params("parallel"),
        name="glu_res" if glu else "proj_res",
    )(a, w, x, gate)


def _hg_norm_kernel(x_ref, g_ref, sc_ref, sh_ref, o_ref):
    def emit(r0, h):
        o_ref[pl.ds(r0, h.shape[0]), :] = h.astype(BF16)

    _norm_mod_rows(x_ref, lambda: (g_ref[...], sc_ref[0], sh_ref[0]), emit)


def _hg_norm(x, g, sc, sh, seq):
    n, d = x.shape
    tm = _tile(seq, 1024, 16)
    bspec = _batch_spec(d, seq // tm)
    return pl.pallas_call(
        _hg_norm_kernel,
        grid=(n // tm,),
        in_specs=[pl.BlockSpec((tm, d), lambda i: (i, 0)), pl.BlockSpec((1, d), lambda i: (0, 0)), bspec, bspec],
        out_specs=pl.BlockSpec((tm, d), lambda i: (i, 0)),
        out_shape=jax.ShapeDtypeStruct((n, d), BF16),
        compiler_params=_params("parallel"),
        name="hg_norm",
    )(x, g, sc, sh)


def _hg_levels(ch):
    levels = []
    m = ch // 2
    while m >= 1:
        levels.append(m)
        m //= 2
    return levels


def _hg_masks(ch):
    levels = _hg_levels(ch)
    mk = np.zeros((len(levels) + 1, ch, ch), np.float32)
    for li, m in enumerate(levels):
        for t in range(ch):
            mid = (t // (2 * m)) * 2 * m + m - 1
            if t > mid:
                mk[li, t, mid - m + 1:mid + 1] = 1.0
    mk[-1] = np.eye(ch, dtype=np.float32)
    return mk


def _cumsum_rows(x, t_idx):
    sh = 1
    while sh < x.shape[0]:
        x = x + jnp.where(t_idx >= sh, pltpu.roll(x, sh, 0), 0.0)
        sh *= 2
    return x


def _level_ref(bc, m, sub3):
    ch, dk = bc.shape
    if 2 * m >= SUBLANES:
        bp = bc.reshape(ch // (2 * m), 2 * m, dk)
        return jnp.broadcast_to(bp[:, m - 1:m, :], bp.shape).reshape(ch, dk)
    b3 = bc.reshape(ch // SUBLANES, SUBLANES, dk)
    r = None
    for start in range(0, SUBLANES, 2 * m):
        cand = jnp.broadcast_to(b3[:, start + m - 1:start + m, :], b3.shape)
        r = cand if r is None else jnp.where(sub3 >= start, cand, r)
    return r.reshape(ch, dk)


def _hg_mix_kernel(h0_ref, hb_ref, hc_ref, w_ref, la_ref, l1_ref, mask_ref, gn_ref, o_ref, st_s, pa_s, pb_s,
                   *, nb, ch):
    d = h0_ref.shape[-1]

    def project(h_ref, e, c0, width):
        return _dot(h_ref[...].reshape(nb * ch, d), w_ref[e, :, c0:c0 + width])

    n_hd, _, cols = w_ref.shape

    @pl.when(pl.program_id(1) == 0)
    def _():
        st_s[...] = jnp.zeros(st_s.shape, F32)
        for e in range(n_hd):
            pa_s[:, e * cols:(e + 1) * cols] = project(h0_ref, e, 0, cols)

    args = (w_ref, la_ref, l1_ref, mask_ref, gn_ref, o_ref, st_s)
    for e in range(n_hd):
        _hg_chunk(hb_ref, *args, pa_s, pb_s, project, nb, ch, 0, e)
    for e in range(n_hd):
        _hg_chunk(hc_ref, *args, pb_s, pa_s, project, nb, ch, ch, e)


def _hg_chunk(hn_ref, w_ref, la_ref, l1_ref, mask_ref, gn_ref, o_ref, st_s, p_s, pn_s, project, nb, ch, r0, e):
    n_hd, _, cols = w_ref.shape
    dk = o_ref.shape[-1] // n_hd
    cb = e * cols
    assert nb % HG_GROUP == 0 and cols % (nb // HG_GROUP * LANES) == 0
    cpg = cols // (nb // HG_GROUP)

    la = la_ref[e]
    l1 = l1_ref[e]
    levels = _hg_levels(ch)
    n_lv = len(levels)
    gn = gn_ref[...]
    t_idx = lax.broadcasted_iota(I32, (ch, dk), 0)
    sub3 = lax.broadcasted_iota(I32, (ch // SUBLANES, SUBLANES, dk), 1)
    masks = [mask_ref[li] > 0.5 for li in range(n_lv + 1)]

    for b0 in range(0, nb, HG_GROUP):
        grp = range(b0, min(b0 + HG_GROUP, nb))
        c0 = (b0 // HG_GROUP) * cpg
        pn_s[:, cb + c0:cb + c0 + cpg] = project(hn_ref, e, c0, cpg)
        qk, qms, kms, q_in, k_out, decay = {}, {}, {}, {}, {}, {}
        for b in grp:
            q = p_s[b * ch:(b + 1) * ch, cb:cb + dk]
            z = p_s[b * ch:(b + 1) * ch, cb + dk:cb + 2 * dk]
            bt = l1 + jnp.minimum(z, 0.0) - jnp.log(1.0 + jnp.exp(-jnp.abs(z)))
            lf = jnp.maximum(la, bt) + jnp.log(1.0 + jnp.exp(-jnp.abs(la - bt)))
            bc = _cumsum_rows(lf, t_idx)
            qb = q.astype(BF16)
            kb = (1.0 - jnp.exp(lf)).astype(BF16)
            qk[b] = (qb, kb)
            qms[b], kms[b] = [], []
            for m in levels:
                decay_m = jnp.exp(-jnp.abs(bc - _level_ref(bc, m, sub3))).astype(BF16)
                qms[b].append(qb * decay_m)
                kms[b].append(kb * decay_m)
            b_end = bc[ch - 1:ch, :]
            q_in[b] = qb * jnp.exp(bc).astype(BF16)
            k_out[b] = kb * jnp.exp(b_end - bc).astype(BF16)
            decay[b] = jnp.exp(b_end)
        scores = {}
        for b in grp:
            s = jnp.where(masks[n_lv], _dot_nt(*qk[b]), 0.0)
            for li in range(n_lv):
                s = jnp.where(masks[li], _dot_nt(qms[b][li], kms[b][li]), s)
            scores[b] = s.astype(BF16)
        outs = {}
        for b in grp:
            v = p_s[b * ch:(b + 1) * ch, cb + 2 * dk:cb + 3 * dk].astype(BF16)
            st = st_s[e * nb + b]
            outs[b] = _dot(scores[b], v) + _dot_nt(q_in[b], st.astype(BF16))
            st_s[e * nb + b] = decay[b] * st + _dot_tn(v, k_out[b])
        for b in grp:
            o = outs[b]
            gate = p_s[b * ch:(b + 1) * ch, cb + 3 * dk:cb + 4 * dk]
            o = o * lax.rsqrt(jnp.mean(o * o, axis=-1, keepdims=True) + EPS) * gn
            o_ref[b, r0:r0 + ch, e * dk:(e + 1) * dk] = (o * (gate * jax.nn.sigmoid(gate))).astype(BF16)


def _hg_mix(h, w_heads, log_lb, log1m_lb, g_norm):
    nb, seq, d = h.shape
    heads, _, cols = w_heads.shape
    dk = cols // 4
    ch = _tile(seq, HG_CHUNK, 16)
    nc = seq // ch
    assert nc % 2 == 0, "two chunks per grid step"
    hp = HG_HEADS_PER_STEP if heads % HG_HEADS_PER_STEP == 0 else 1
    mk = _hg_masks(ch)
    per_head = lambda w: pl.BlockSpec((hp,) + w.shape[1:], lambda hd, s: (hd, 0, 0))
    return pl.pallas_call(
        functools.partial(_hg_mix_kernel, nb=nb, ch=ch),
        grid=(heads // hp, nc // 2),
        in_specs=[pl.BlockSpec((nb, ch, d), lambda hd, s: (0, 0, 0)),
                  pl.BlockSpec((nb, ch, d), lambda hd, s: (0, 2 * s + 1, 0)),
                  pl.BlockSpec((nb, ch, d), lambda hd, s: (0, jnp.minimum(2 * s + 2, nc - 1), 0)),
                  per_head(w_heads), per_head(log_lb), per_head(log1m_lb),
                  pl.BlockSpec(mk.shape, lambda hd, s: (0, 0, 0)),
                  pl.BlockSpec((1, dk), lambda hd, s: (0, 0))],
        out_specs=pl.BlockSpec((nb, 2 * ch, hp * dk), lambda hd, s: (0, s, hd)),
        out_shape=jax.ShapeDtypeStruct((nb, seq, heads * dk), BF16),
        scratch_shapes=[pltpu.VMEM((hp * nb, dk, dk), F32), pltpu.VMEM((nb * ch, hp * cols), F32),
                        pltpu.VMEM((nb * ch, hp * cols), F32)],
        compiler_params=_params("parallel", "arbitrary"),
        name="hg_mix",
    )(h, h, h, w_heads, log_lb, log1m_lb, jnp.asarray(mk, F32), g_norm)


def _router_kernel(x_ref, g_ref, sc_ref, sh_ref, wh_ref, wl_ref, br_ref, tri_ref,
                   hp_ref, idx_ref, gate_ref, rank_ref, cnt_ref):
    h = _norm_mod(x_ref[...], g_ref[...], sc_ref[0], sh_ref[0])
    half = h.shape[1] // 2
    _store_tiles(hp_ref, 0, _pack_pair(h[:, :half], h[:, half:]))
    hh, hl = _split(h)
    wh = wh_ref[...]
    vals = _dot(hh, wh) + _dot(hl, wh) + _dot(hh, wl_ref[...]) + br_ref[...]
    tm, n_exp = vals.shape
    lane = lax.broadcasted_iota(I32, (tm, n_exp), 1)
    tops, idxs, hots = [], [], []
    for _ in range(TOP_K):
        m = jnp.max(vals, axis=-1, keepdims=True)
        i = jnp.min(jnp.where(vals == m, lane, n_exp), axis=-1, keepdims=True)
        tops.append(m)
        idxs.append(i)
        hots.append(lane == i)
        vals = jnp.where(hots[-1], -jnp.inf, vals)
    es = [jnp.exp(t - tops[0]) for t in tops]
    den = es[0] + es[1] + es[2] + es[3]

    sel = [jnp.where(hm, 1.0, 0.0) for hm in hots]
    multi = sel[0] + sel[1] + sel[2] + sel[3]
    base = _dot(tri_ref[...], multi.astype(BF16))
    cnt_ref[0] = jnp.sum(multi, axis=0, keepdims=True)

    lane_o = lax.broadcasted_iota(I32, idx_ref.shape, 1)
    io = jnp.zeros(idx_ref.shape, I32)
    ro = jnp.zeros(idx_ref.shape, I32)
    go = jnp.zeros(gate_ref.shape, F32)
    for kk in range(TOP_K):
        rk = jnp.sum(sel[kk] * base, axis=-1, keepdims=True).astype(I32)
        io = jnp.where(lane_o == kk, idxs[kk], io)
        ro = jnp.where(lane_o == kk, rk, ro)
        go = jnp.where(lane_o == kk, es[kk] / den, go)
    idx_ref[...] = io
    rank_ref[...] = ro
    gate_ref[...] = go


def _router(x, g, sc, sh, w_r, b_r, seq):
    n, d = x.shape
    n_exp = w_r.shape[1]
    tm = _tile(seq, 512, 16)
    bspec = _batch_spec(d, seq // tm)
    pcs = d // 2 // LANES
    wh, wl = _split(w_r.astype(F32))
    tri = jnp.asarray(np.tril(np.ones((tm, tm), np.float32), -1), BF16)
    row = lambda w: pl.BlockSpec((tm, w), lambda i: (i, 0))
    full = lambda a: pl.BlockSpec(a.shape, lambda i: (0, 0))
    b2 = b_r.astype(F32).reshape(1, n_exp)
    return pl.pallas_call(
        _router_kernel,
        grid=(n // tm,),
        in_specs=[row(d), full(g), bspec, bspec, full(wh), full(wl), full(b2), full(tri)],
        out_specs=[pl.BlockSpec((tm * pcs, LANES), lambda i: (i, 0)), row(LANES), row(LANES), row(LANES),
                   pl.BlockSpec((1, 1, n_exp), lambda i: (i, 0, 0))],
        out_shape=[jax.ShapeDtypeStruct((n * pcs, LANES), I32),
                   jax.ShapeDtypeStruct((n, LANES), I32),
                   jax.ShapeDtypeStruct((n, LANES), F32),
                   jax.ShapeDtypeStruct((n, LANES), I32),
                   jax.ShapeDtypeStruct((n // tm, 1, n_exp), F32)],
        compiler_params=_params("parallel"),
        name="router",
    )(x, g, sc, sh, wh, wl, b2, tri)


def _moe_plan(top_idx, lrank, tile_counts, tmb):
    n = top_idx.shape[0]
    n_tiles, _, n_exp = tile_counts.shape
    tc = tile_counts.reshape(n_tiles, n_exp).astype(I32)
    counts = jnp.sum(tc, axis=0)
    nblk = (counts + tmb - 1) // tmb
    blk_end = jnp.cumsum(nblk)
    blk_start = blk_end - nblk
    n_used = blk_end[-1]
    n_blocks = n * TOP_K // tmb + n_exp
    blk = jnp.arange(n_blocks, dtype=I32)
    be = jnp.minimum(jnp.sum((blk[:, None] >= blk_end[None, :]).astype(I32), axis=1), n_exp - 1)
    last = jnp.sum(jnp.where(blk == n_used - 1, be, 0))
    be = jnp.where(blk < n_used, be, last).astype(I32)
    nxt = jnp.sum(jnp.where(blk[None, :] == jnp.minimum(blk_end[be], n_used - 1)[:, None], be[None, :], 0), axis=1)
    base = blk_start * tmb
    seg_dst = base[None, :] + jnp.cumsum(tc, axis=0) - tc
    seg_off = jnp.cumsum(tc, axis=1) - tc
    tm = n // n_tiles
    hot = top_idx.reshape(n_tiles, tm, TOP_K, 1) == jnp.arange(n_exp, dtype=I32)
    lpos = jnp.sum(jnp.where(hot, seg_off[:, None, None, :], 0), axis=-1) + lrank.reshape(n_tiles, tm, TOP_K)
    pad_start = base + counts
    pad_len = blk_end * tmb - pad_start
    flat = lambda a: a.reshape(-1).astype(I32)
    return dict(be=be, nxt=nxt.astype(I32), n_used=n_used.reshape(1).astype(I32),
                lpos=lpos.reshape(n_tiles, 1, tm * TOP_K).astype(I32),
                seg_cnt=flat(tc), seg_off=flat(seg_off), seg_dst=flat(seg_dst),
                pad_start=pad_start.astype(I32), pad_len=pad_len.astype(I32),
                n_blocks=n_blocks, tmb=tmb, n_tiles=n_tiles, n_exp=n_exp)


def _tile_at(ref, row, pcs):
    return ref.at[pl.ds(pl.multiple_of(row * pcs, pcs), pcs)]


def _segments(i, cnt_ref, off_ref, dst_ref, n_exp, pcs, copy):
    def seg(e, carry):
        cnt = cnt_ref[i * n_exp + e]

        @pl.when(cnt > 0)
        def _():
            size = pl.multiple_of(cnt * pcs, pcs)
            copy(pl.ds(pl.multiple_of(off_ref[i * n_exp + e] * pcs, pcs), size),
                 pl.ds(pl.multiple_of(dst_ref[i * n_exp + e] * pcs, pcs), size))

        return carry

    lax.fori_loop(0, n_exp, seg, 0)


def _dispatch_kernel(sc_ref, so_ref, sd_ref, ps_ref, pn_ref, nu_ref, lpos_ref, hp_ref, xs_hbm,
                     sbuf, zero_s, sems, sem, *, tmb, pcs):
    i = pl.program_id(0)
    last = pl.num_programs(0) - 1
    tm = hp_ref.shape[0] // pcs
    slot = i % 2

    def drain(s):
        pltpu.make_async_copy(sbuf.at[s], xs_hbm.at[pl.ds(0, sbuf.shape[1])], sems.at[s]).wait()

    @pl.when(i >= 2)
    def _():
        drain(slot)

    def row(r, carry):
        tile = hp_ref[pl.ds(pl.multiple_of(r * pcs, pcs), pcs), :]
        for kk in range(TOP_K):
            p = lpos_ref[0, 0, TOP_K * r + kk]
            sbuf[slot, pl.ds(pl.multiple_of(p * pcs, pcs), pcs), :] = tile
        return carry

    lax.fori_loop(0, tm, row, 0, unroll=8)

    def copy(src, dst):
        pltpu.make_async_copy(sbuf.at[slot, src], xs_hbm.at[dst], sems.at[slot]).start()

    _segments(i, sc_ref, so_ref, sd_ref, ps_ref.shape[0], pcs, copy)

    @pl.when(i == last)
    def _():
        drain(slot)

        @pl.when(i >= 1)
        def _():
            drain(1 - slot)

        zero_s[...] = jnp.zeros(zero_s.shape, I32)

        size = tmb // 2
        while size >= 1:
            def chunk(e, size=size):
                cnt = pn_ref[e]
                before = cnt - cnt % (2 * size)
                dst = pl.ds(pl.multiple_of((ps_ref[e] + before) * pcs, pcs), size * pcs)
                return (cnt // size) % 2 == 1, pltpu.make_async_copy(
                    zero_s.at[pl.ds(0, size * pcs)], xs_hbm.at[dst], sem)

            def fill(e, carry):
                take, cp = chunk(e)

                @pl.when(take)
                def _():
                    cp.start()

                return carry

            def fill_done(e, carry):
                take, cp = chunk(e)

                @pl.when(take)
                def _():
                    cp.wait()

                return carry

            lax.fori_loop(0, ps_ref.shape[0], fill, 0)
            lax.fori_loop(0, ps_ref.shape[0], fill_done, 0)
            size //= 2

        def block(bk, carry):
            cp = pltpu.make_async_copy(zero_s, _tile_at(xs_hbm, bk, tmb * pcs), sem)
            cp.start()
            cp.wait()
            return carry

        lax.fori_loop(nu_ref[0], xs_hbm.shape[0] // (tmb * pcs), block, 0)


def _dispatch(hp, plan, n):
    pcs = hp.shape[0] // n
    tmb = plan['tmb']
    n_tiles = plan['n_tiles']
    tm = n // n_tiles
    grid_spec = pltpu.PrefetchScalarGridSpec(
        num_scalar_prefetch=6,
        grid=(n_tiles,),
        in_specs=[pl.BlockSpec((1, 1, TOP_K * tm), lambda i, *_: (i, 0, 0), memory_space=pltpu.SMEM),
                  pl.BlockSpec((tm * pcs, LANES), lambda i, *_: (i, 0))],
        out_specs=pl.BlockSpec(memory_space=pl.ANY),
        scratch_shapes=[pltpu.VMEM((2, TOP_K * tm * pcs, LANES), I32),
                        pltpu.VMEM((tmb * pcs, LANES), I32),
                        pltpu.SemaphoreType.DMA((2,)), pltpu.SemaphoreType.DMA(())],
    )
    return pl.pallas_call(
        functools.partial(_dispatch_kernel, tmb=tmb, pcs=pcs),
        grid_spec=grid_spec,
        out_shape=jax.ShapeDtypeStruct((plan['n_blocks'] * tmb * pcs, LANES), I32),
        compiler_params=_params("arbitrary"),
        name="moe_dispatch",
    )(plan['seg_cnt'], plan['seg_off'], plan['seg_dst'], plan['pad_start'], plan['pad_len'], plan['n_used'],
      plan['lpos'], hp)


def _moe_kernel(be_ref, nx_ref, nu_ref, x_ref, wgu_hbm, bgu_ref, wdn_hbm, bdn_ref, y_ref,
                wgu_f, wdn_f, wgu_s, wdn_s, sems, *, ff, layer):
    i = pl.program_id(0)
    used = i < nu_ref[0]

    def fetch(e):
        return (pltpu.make_async_copy(wgu_hbm.at[layer, e], wgu_f, sems.at[0]),
                pltpu.make_async_copy(wdn_hbm.at[layer, e], wdn_f, sems.at[1]))

    @pl.when(used)
    def _():
        e = be_ref[i]
        e_new = jnp.logical_or(i == 0, e != be_ref[jnp.maximum(i - 1, 0)])

        @pl.when(i == 0)
        def _():
            for cp in fetch(e):
                cp.start()

        @pl.when(e_new)
        def _():
            for cp in fetch(e):
                cp.wait()
            wgu_s[...] = wgu_f[...].astype(BF16)
            wdn_s[...] = wdn_f[...].astype(BF16)
            nxt = nx_ref[i]

            @pl.when(nxt != e)
            def _():
                for cp in fetch(nxt):
                    cp.start()

        half = wgu_s.shape[0] // 2
        pcs = half // LANES
        tmb = x_ref.shape[0] // pcs
        x_lo, x_hi = _unpack_pair(_load_tiles(x_ref, 0, tmb, pcs))
        gu = (_dot(x_lo.astype(BF16), wgu_s[:half, :]) + _dot(x_hi.astype(BF16), wgu_s[half:, :])
              + bgu_ref[0, 0])
        gt = jnp.minimum(gu[:, :ff], SWIGLU_LIMIT)
        up = jnp.clip(gu[:, ff:], -SWIGLU_LIMIT, SWIGLU_LIMIT)
        act = ((up + 1.0) * gt * jax.nn.sigmoid(SWIGLU_ALPHA * gt)).astype(BF16)
        bdn = bdn_ref[0, 0]
        y_lo = _dot(act, wdn_s[:, :half]) + bdn[:, :half]
        y_hi = _dot(act, wdn_s[:, half:]) + bdn[:, half:]
        _store_tiles(y_ref, 0, _pack_pair(y_lo, y_hi))

    @pl.when(jnp.logical_not(used))
    def _():
        y_ref[...] = jnp.zeros(y_ref.shape, I32)


def _moe_experts(xs, plan, w_gu, b_gu, w_dn, b_dn, layer):
    depth, n_exp, d, ff2 = w_gu.shape
    ff = ff2 // 2
    tmb = plan['tmb']
    n_blocks = plan['n_blocks']
    rows = xs.shape[0] // n_blocks
    wmap = lambda i, be, nx, nu: (layer, be[i], 0, 0)
    grid_spec = pltpu.PrefetchScalarGridSpec(
        num_scalar_prefetch=3,
        grid=(n_blocks,),
        in_specs=[pl.BlockSpec((rows, LANES), lambda i, *_: (i, 0)),
                  pl.BlockSpec(memory_space=pl.ANY),
                  pl.BlockSpec((1, 1, 1, ff2), wmap),
                  pl.BlockSpec(memory_space=pl.ANY),
                  pl.BlockSpec((1, 1, 1, d), wmap)],
        out_specs=pl.BlockSpec((rows, LANES), lambda i, *_: (i, 0)),
        scratch_shapes=[pltpu.VMEM((d, ff2), F32), pltpu.VMEM((ff, d), F32),
                        pltpu.VMEM((d, ff2), BF16), pltpu.VMEM((ff, d), BF16),
                        pltpu.SemaphoreType.DMA((2,))],
    )
    return pl.pallas_call(
        functools.partial(_moe_kernel, ff=ff, layer=layer),
        grid_spec=grid_spec,
        out_shape=jax.ShapeDtypeStruct(xs.shape, I32),
        compiler_params=_params("arbitrary"),
        name="moe_experts",
    )(plan['be'], plan['nxt'], plan['n_used'], xs, w_gu, b_gu.reshape(depth, n_exp, 1, ff2),
      w_dn, b_dn.reshape(depth, n_exp, 1, d))


def _combine_kernel(sc_ref, so_ref, sd_ref, lpos_ref, x_ref, gate_ref, g_ref, nf_ref, ys_hbm, o_ref,
                    ybuf, gbuf, sems, *, n_exp, final):
    i = pl.program_id(0)
    tm, d = x_ref.shape
    half = d // 2
    pcs = half // LANES
    slot = i % 2

    def fetch(t, s):
        def copy(loc, blk):
            pltpu.make_async_copy(ys_hbm.at[blk], ybuf.at[s, loc], sems.at[s]).start()

        _segments(t, sc_ref, so_ref, sd_ref, n_exp, pcs, copy)

    @pl.when(i == 0)
    def _():
        fetch(i, slot)

    @pl.when(i + 1 < pl.num_programs(0))
    def _():
        fetch(i + 1, 1 - slot)

    pltpu.make_async_copy(ys_hbm.at[pl.ds(0, ybuf.shape[1])], ybuf.at[slot], sems.at[slot]).wait()

    def row(r, carry):
        for kk in range(TOP_K):
            p = lpos_ref[0, 0, TOP_K * r + kk]
            gbuf[kk, pl.ds(pl.multiple_of(r * pcs, pcs), pcs), :] = (
                ybuf[slot, pl.ds(pl.multiple_of(p * pcs, pcs), pcs), :])
        return carry

    lax.fori_loop(0, tm, row, 0, unroll=8)

    gt = gate_ref[...]
    lo = hi = None
    for kk in range(TOP_K):
        w = gt[:, kk:kk + 1]
        y_lo, y_hi = _unpack_pair(_load_tiles(gbuf.at[kk], 0, tm, pcs))
        lo = w * y_lo if lo is None else lo + w * y_lo
        hi = w * y_hi if hi is None else hi + w * y_hi
    g = g_ref[0]
    x_lo = x_ref[:, :half] + lo * g[:, :half]
    x_hi = x_ref[:, half:] + hi * g[:, half:]
    if final:
        ms = (jnp.sum(x_lo * x_lo, axis=-1, keepdims=True) + jnp.sum(x_hi * x_hi, axis=-1, keepdims=True)) / d
        r = lax.rsqrt(ms + EPS)
        x_lo = (x_lo * r) * nf_ref[:, :half]
        x_hi = (x_hi * r) * nf_ref[:, half:]
    o_ref[:, :half] = x_lo
    o_ref[:, half:] = x_hi


def _combine(x, ys, gates, g2, plan, seq, norm_final, final):
    n, d = x.shape
    pcs = d // 2 // LANES
    nt = plan['n_tiles']
    tm = n // nt
    grid_spec = pltpu.PrefetchScalarGridSpec(
        num_scalar_prefetch=3,
        grid=(nt,),
        in_specs=[pl.BlockSpec((1, 1, TOP_K * tm), lambda i, *_: (i, 0, 0), memory_space=pltpu.SMEM),
                  pl.BlockSpec((tm, d), lambda i, *_: (i, 0)),
                  pl.BlockSpec((tm, LANES), lambda i, *_: (i, 0)),
                  _batch_spec(d, seq // tm),
                  pl.BlockSpec((1, d), lambda i, *_: (0, 0)),
                  pl.BlockSpec(memory_space=pl.ANY)],
        out_specs=pl.BlockSpec((tm, d), lambda i, *_: (i, 0)),
        scratch_shapes=[pltpu.VMEM((2, TOP_K * tm * pcs, LANES), I32),
                        pltpu.VMEM((TOP_K, tm * pcs, LANES), I32),
                        pltpu.SemaphoreType.DMA((2,))],
    )
    return pl.pallas_call(
        functools.partial(_combine_kernel, n_exp=plan['n_exp'], final=final),
        grid_spec=grid_spec,
        out_shape=jax.ShapeDtypeStruct((n, d), F32),
        compiler_params=_params("arbitrary"),
        name="moe_combine",
    )(plan['seg_cnt'], plan['seg_off'], plan['seg_dst'], plan['lpos'], x, gates, g2, norm_final, ys)


def kernel(x, c, ada_w, ada_b, norm_mix, norm_ffn, norm_final, s5_lambda_re, s5_lambda_im, s5_log_dt,
           s5_b_re, s5_b_im, s5_c_re, s5_c_im, s5_d, s5_w_glu, hg_w_in, hg_lb_raw, hg_norm, hg_w_out,
           router_w, router_b, moe_w_gate_up, moe_b_gate_up, moe_w_down, moe_b_down):
    bsz, seq, d = x.shape
    depth = ada_w.shape[0]
    n = bsz * seq
    dv = hg_norm.shape[-1]
    heads = d // dv
    fdim = (hg_w_in.shape[-1] - 2 * d) // 2
    dk = fdim // heads
    assert dk == LANES and dv == LANES, "one HGRN2 head per 128-lane tile"
    tmb = _tile(n * TOP_K, MOE_ROWS, SUBLANES)

    lb_p = jax.nn.softmax(hg_lb_raw.astype(F32), axis=0)
    lb_all = jnp.cumsum(lb_p, axis=0) - lb_p[0]
    mod = _ada_mod(c.astype(F32), ada_w, ada_b)

    xs = x.astype(F32).reshape(n, d)
    row = lambda v: v.astype(F32).reshape(1, -1)
    for i in range(depth):
        sh1, sc1, g1, sh2, sc2, g2 = [mod[i, :, k * d:(k + 1) * d].reshape(bsz, 1, d) for k in range(N_ADA)]
        j = i // 2
        if i % 2 == 0:
            bmat, cmat, lre, lim = _s5_prep(s5_lambda_re[j], s5_lambda_im[j], s5_log_dt[j],
                                            s5_b_re[j], s5_b_im[j], s5_c_re[j], s5_c_im[j])
            z = _s5_scan(xs.reshape(bsz, seq, d), row(norm_mix[i]), sc1, sh1, bmat, cmat, lre, lim, row(s5_d[j]))
            xs = _mm_res(z.reshape(n, d), s5_w_glu[j].astype(BF16), xs, g1, True, seq)
        else:
            lb = lb_all[i].reshape(heads, 1, dk)
            w_heads = hg_w_in[j].astype(BF16).reshape(d, 4, heads, dk).transpose(2, 0, 1, 3).reshape(heads, d, 4 * dk)
            h = _hg_norm(xs, row(norm_mix[i]), sc1, sh1, seq)
            o = _hg_mix(h.reshape(bsz, seq, d), w_heads, jnp.log(lb), jnp.log1p(-lb), row(hg_norm[j]))
            xs = _mm_res(o.reshape(n, d), hg_w_out[j].astype(BF16), xs, g1, False, seq)
        hp, top_idx, gates, rank, counts = _router(xs, row(norm_ffn[i]), sc2, sh2, router_w[i], router_b[i], seq)
        plan = _moe_plan(top_idx[:, :TOP_K], rank[:, :TOP_K], counts, tmb)
        xd = _dispatch(hp, plan, n)
        ys = _moe_experts(xd, plan, moe_w_gate_up, moe_b_gate_up.astype(F32),
                          moe_w_down, moe_b_down.astype(F32), i)
        xs = _combine(xs, ys, gates, g2, plan, seq, row(norm_final), final=i == depth - 1)
    return xs.reshape(bsz, seq, d).astype(x.dtype)
```
